```python
import math
import jax, jax.numpy as jnp
from jax import lax
import numpy as np

D_MODEL = 1024
BATCH = 32
SEQ = 256
DEPTH = 2
DEC_BATCH = 2
DEC_SEQ = 1024
PAST_LEN = 512

GRID_W = 64
NA_HEADS = 8
HEAD_DIM = 64
NA_WIDTH = NA_HEADS * HEAD_DIM
WIN_R = 8
WIN_C = 16
S5_WIDTH = D_MODEL - NA_WIDTH
S5_GROUP = 16
S5_GROUPS = S5_WIDTH // S5_GROUP
S5_STATE = 64
FNET_GROUPS = 4
FNET_GROUP_WIDTH = D_MODEL // FNET_GROUPS
D_FF = -(-8 * D_MODEL // (3 * 256)) * 256
N_EVEN = (DEPTH + 1) // 2
N_ODD = DEPTH // 2
Q_BLOCK = 128
EPS = 1e-6

kernel_name = 'hybrid_natten_s5_fnet_diffusion_step'


def _rmsnorm(x, g):
    xf = x.astype(jnp.float32)
    y = xf * lax.rsqrt(jnp.mean(xf * xf, axis=-1, keepdims=True) + EPS)
    return (y * g.astype(jnp.float32)).astype(x.dtype)


def _modulation(cond, w_mod, b_mod):
    m = jax.nn.silu(cond) @ w_mod + b_mod
    return [t[:, None, :] for t in jnp.split(m, 6, axis=-1)]


def _modulate(h, shift, scale):
    return h * (1 + scale) + shift


def _swiglu(h, wg, wu, wd):
    return (jax.nn.silu(h @ wg) * (h @ wu)) @ wd


def _even_proj(h, w_in):
    B, L, _ = h.shape
    z = h @ w_in
    q, k, v, u = jnp.split(z, [NA_WIDTH, 2 * NA_WIDTH, 3 * NA_WIDTH], axis=-1)
    shp = (B, L, NA_HEADS, HEAD_DIM)
    return q.reshape(shp), k.reshape(shp), v.reshape(shp), u


def _ctx_attention(q, k, v):
    B, L, H, Dh = q.shape
    nb = L // Q_BLOCK
    scale = Dh ** -0.5
    qb = q.reshape(B, nb, Q_BLOCK, H, Dh).transpose(1, 0, 2, 3, 4)

    def block(qi):
        s = jnp.einsum('bqhd,bkhd->bhqk', qi, k).astype(jnp.float32) * scale
        p = jax.nn.softmax(s, axis=-1).astype(v.dtype)
        return jnp.einsum('bhqk,bkhd->bqhd', p, v)

    o = lax.map(block, qb)
    return o.transpose(1, 0, 2, 3, 4).reshape(B, L, H, Dh)


def _latent_na_attention(q, k, v, ck, cv, rpb):
    B, T, H, Dh = q.shape
    rows = T // GRID_W
    kr = min(WIN_R, rows)
    kc = WIN_C
    scale = Dh ** -0.5
    qg = q.reshape(B, rows, GRID_W, H, Dh).transpose(1, 0, 2, 3, 4)
    kg = k.reshape(B, rows, GRID_W, H, Dh)
    vg = v.reshape(B, rows, GRID_W, H, Dh)
    cols = np.arange(GRID_W)
    col_start = np.clip(cols - kc // 2, 0, GRID_W - kc)
    col_idx = col_start[:, None] + np.arange(kc)[None, :]
    dc_idx = col_idx - cols[:, None] + (WIN_C - 1)

    def row_block(args):
        r, qr = args
        rs = jnp.clip(r - kr // 2, 0, rows - kr)
        kband = lax.dynamic_slice_in_dim(kg, rs, kr, axis=1)
        vband = lax.dynamic_slice_in_dim(vg, rs, kr, axis=1)
        kwin = kband[:, :, col_idx]
        vwin = vband[:, :, col_idx]
        dr_idx = rs + jnp.arange(kr) - r + (WIN_R - 1)
        bias = rpb[:, dr_idx][:, :, dc_idx]
        bias = bias.transpose(0, 2, 1, 3).astype(jnp.float32)
        s_loc = jnp.einsum('bqhd,biqjhd->bhqij', qr, kwin).astype(jnp.float32) * scale + bias[None]
        s_ctx = jnp.einsum('bqhd,blhd->bhql', qr, ck).astype(jnp.float32) * scale
        s = jnp.concatenate([s_loc.reshape(B, H, GRID_W, kr * kc), s_ctx], axis=-1)
        p = jax.nn.softmax(s, axis=-1).astype(v.dtype)
        p_loc = p[..., :kr * kc].reshape(B, H, GRID_W, kr, kc)
        p_ctx = p[..., kr * kc:]
        return (jnp.einsum('bhqij,biqjhd->bqhd', p_loc, vwin)
                + jnp.einsum('bhql,blhd->bqhd', p_ctx, cv))

    o = lax.map(row_block, (jnp.arange(rows), qg))
    return o.transpose(1, 0, 2, 3, 4).reshape(B, T, H, Dh)


def _scan_combine(e1, e2):
    a1r, a1i, b1r, b1i = e1
    a2r, a2i, b2r, b2i = e2
    return (a2r * a1r - a2i * a1i,
            a2r * a1i + a2i * a1r,
            a2r * b1r - a2i * b1i + b2r,
            a2r * b1i + a2i * b1r + b2i)


def _s5_direction(u, lam_re, lam_im, log_step, b_re, b_im, c_re, c_im, s0_re, s0_im, reverse):
    f32 = jnp.float32
    lr = jnp.minimum(lam_re.astype(f32), -1e-4)
    li = lam_im.astype(f32)
    dt = jnp.exp(log_step.astype(f32))[:, None]
    mag = jnp.exp(lr * dt)
    ar = mag * jnp.cos(li * dt)
    ai = mag * jnp.sin(li * dt)
    den = lr * lr + li * li
    nr = ar - 1.0
    ni = ai
    cr = (nr * lr + ni * li) / den
    ci = (ni * lr - nr * li) / den
    br_, bi_ = b_re.astype(f32), b_im.astype(f32)
    bbr = cr[..., None] * br_ - ci[..., None] * bi_
    bbi = cr[..., None] * bi_ + ci[..., None] * br_
    bur = jnp.einsum('blgh,gph->blgp', u, bbr)
    bui = jnp.einsum('blgh,gph->blgp', u, bbi)
    L = u.shape[1]
    first = L - 1 if reverse else 0
    last = 0 if reverse else L - 1
    s0r = s0_re.astype(f32)
    s0i = s0_im.astype(f32)
    bur = bur.at[:, first].add(ar * s0r - ai * s0i)
    bui = bui.at[:, first].add(ar * s0i + ai * s0r)
    Ar = jnp.broadcast_to(ar, bur.shape)
    Ai = jnp.broadcast_to(ai, bur.shape)
    _, _, xr, xi = lax.associative_scan(_scan_combine, (Ar, Ai, bur, bui), axis=1, reverse=reverse)
    y = (jnp.einsum('blgp,ghp->blgh', xr, c_re.astype(f32))
         - jnp.einsum('blgp,ghp->blgh', xi, c_im.astype(f32)))
    return y, xr[:, last], xi[:, last]


def _s5_mixer(u, s0, lam_re, lam_im, log_step, b_re, b_im, c_re, c_im, d, glu_w, glu_b):
    B, L, _ = u.shape
    uf = u.astype(jnp.float32).reshape(B, L, S5_GROUPS, S5_GROUP)
    yf, fr, fi = _s5_direction(uf, lam_re[0], lam_im[0], log_step[0], b_re[0], b_im[0],
                               c_re[0], c_im[0], s0[:, 0, 0], s0[:, 0, 1], False)
    yb, br, bi = _s5_direction(uf, lam_re[1], lam_im[1], log_step[1], b_re[1], b_im[1],
                               c_re[1], c_im[1], s0[:, 1, 0], s0[:, 1, 1], True)
    y = (yf + yb).reshape(B, L, S5_WIDTH) + d.astype(jnp.float32) * uf.reshape(B, L, S5_WIDTH)
    y = jax.nn.gelu(y)
    y = y * jax.nn.sigmoid(y @ glu_w.astype(jnp.float32) + glu_b.astype(jnp.float32))
    state = jnp.stack([jnp.stack([fr, fi], axis=1), jnp.stack([br, bi], axis=1)], axis=1)
    return y.astype(u.dtype), state


def _fourier_mixer(h, w_in, w_out):
    B, L, _ = h.shape
    z = (h @ w_in).astype(jnp.float32).reshape(B, L, FNET_GROUPS, FNET_GROUP_WIDTH)
    f = jnp.fft.fft2(z, axes=(1, 3), norm='ortho').real
    return f.reshape(B, L, D_MODEL).astype(h.dtype) @ w_out


def setup_inputs(seed: int = 0) -> dict:
    key = jax.random.key(seed)
    ks = iter(jax.random.split(key, 32))
    f32 = jnp.float32

    def nrm(shape, scale):
        return jax.random.normal(next(ks), shape, f32) * scale

    dS = D_MODEL ** -0.5
    inp = {}
    inp['x_prompt'] = nrm((BATCH, SEQ, D_MODEL), 1.0)
    inp['x_sample'] = nrm((DEC_BATCH, DEC_SEQ, D_MODEL), 1.0)
    inp['cache_na_k'] = nrm((DEC_BATCH, N_EVEN, PAST_LEN, NA_HEADS, HEAD_DIM), 1.0)
    inp['cache_na_v'] = nrm((DEC_BATCH, N_EVEN, PAST_LEN, NA_HEADS, HEAD_DIM), 1.0)
    inp['state_s5'] = nrm((DEC_BATCH, N_EVEN, 2, 2, S5_GROUPS, S5_STATE), 0.5)
    inp['c'] = nrm((DEC_BATCH, D_MODEL), 1.0)
    inp['c_ctx'] = nrm((D_MODEL,), 1.0)
    inp['w_mod'] = nrm((DEPTH, D_MODEL, 6 * D_MODEL), 0.5 * dS)
    inp['b_mod'] = nrm((DEPTH, 6 * D_MODEL), 0.01)
    inp['norm_mix_g'] = 1.0 + nrm((DEPTH, D_MODEL), 0.01)
    inp['norm_ffn_g'] = 1.0 + nrm((DEPTH, D_MODEL), 0.01)
    inp['w_in_even'] = nrm((N_EVEN, D_MODEL, 3 * NA_WIDTH + S5_WIDTH), dS)
    inp['na_rpb'] = nrm((N_EVEN, NA_HEADS, 2 * WIN_R - 1, 2 * WIN_C - 1), 0.1)
    inp['s5_lam_re'] = -0.5 + nrm((N_EVEN, 2, S5_GROUPS, S5_STATE), 0.01)
    inp['s5_lam_im'] = jnp.pi * jnp.arange(S5_STATE, dtype=f32) + nrm((N_EVEN, 2, S5_GROUPS, S5_STATE), 0.01)
    inp['s5_log_step'] = jax.random.uniform(next(ks), (N_EVEN, 2, S5_GROUPS), f32,
                                            minval=math.log(0.001), maxval=math.log(0.1))
    inp['s5_b_re'] = nrm((N_EVEN, 2, S5_GROUPS, S5_STATE, S5_GROUP), (2 * S5_GROUP) ** -0.5)
    inp['s5_b_im'] = nrm((N_EVEN, 2, S5_GROUPS, S5_STATE, S5_GROUP), (2 * S5_GROUP) ** -0.5)
    inp['s5_c_re'] = nrm((N_EVEN, 2, S5_GROUPS, S5_GROUP, S5_STATE), (2 * S5_STATE) ** -0.5)
    inp['s5_c_im'] = nrm((N_EVEN, 2, S5_GROUPS, S5_GROUP, S5_STATE), (2 * S5_STATE) ** -0.5)
    inp['s5_d'] = nrm((N_EVEN, S5_WIDTH), 1.0)
    inp['s5_glu_w'] = nrm((N_EVEN, S5_WIDTH, S5_WIDTH), S5_WIDTH ** -0.5)
    inp['s5_glu_b'] = nrm((N_EVEN, S5_WIDTH), 0.01)
    inp['w_out_even'] = nrm((N_EVEN, D_MODEL, D_MODEL), dS)
    inp['w_in_odd'] = nrm((N_ODD, D_MODEL, D_MODEL), dS)
    inp['w_out_odd'] = nrm((N_ODD, D_MODEL, D_MODEL), dS)
    inp['ffn_w_gate'] = nrm((DEPTH, D_MODEL, D_FF), dS)
    inp['ffn_w_up'] = nrm((DEPTH, D_MODEL, D_FF), dS)
    inp['ffn_w_down'] = nrm((DEPTH, D_FF, D_MODEL), D_FF ** -0.5)
    inp['final_norm_g'] = 1.0 + nrm((D_MODEL,), 0.01)
    return inp


def reference(x_prompt, x_sample, cache_na_k, cache_na_v, state_s5, c, c_ctx,
              w_mod, b_mod, norm_mix_g, norm_ffn_g, w_in_even, na_rpb,
              s5_lam_re, s5_lam_im, s5_log_step, s5_b_re, s5_b_im, s5_c_re, s5_c_im,
              s5_d, s5_glu_w, s5_glu_b, w_out_even, w_in_odd, w_out_odd,
              ffn_w_gate, ffn_w_up, ffn_w_down, final_norm_g):
    xp = x_prompt
    xs = x_sample
    Bp, Lp, _ = xp.shape
    Bs, Ls, _ = xs.shape
    new_k, new_v, new_s = [], [], []
    for layer in range(DEPTH):
        mp = _modulation(c_ctx[None], w_mod[layer], b_mod[layer])
        ms = _modulation(c, w_mod[layer], b_mod[layer])
        hp = _modulate(_rmsnorm(xp, norm_mix_g[layer]), mp[0], mp[1])
        hs = _modulate(_rmsnorm(xs, norm_mix_g[layer]), ms[0], ms[1])
        if layer % 2 == 0:
            e = layer // 2
            s5p = (s5_lam_re[e], s5_lam_im[e], s5_log_step[e], s5_b_re[e], s5_b_im[e],
                   s5_c_re[e], s5_c_im[e], s5_d[e], s5_glu_w[e], s5_glu_b[e])
            qp, kp, vp, up = _even_proj(hp, w_in_even[e])
            ap = _ctx_attention(qp, kp, vp)
            s0 = jnp.zeros((Bp, 2, 2, S5_GROUPS, S5_STATE), jnp.float32)
            yp, stp = _s5_mixer(up, s0, *s5p)
            op = jnp.concatenate([ap.reshape(Bp, Lp, NA_WIDTH), yp], axis=-1) @ w_out_even[e]
            new_k.append(kp)
            new_v.append(vp)
            new_s.append(stp)
            qs, ks_, vs, us = _even_proj(hs, w_in_even[e])
            as_ = _latent_na_attention(qs, ks_, vs, cache_na_k[:, e], cache_na_v[:, e], na_rpb[e])
            ys, _ = _s5_mixer(us, state_s5[:, e].astype(jnp.float32), *s5p)
            os_ = jnp.concatenate([as_.reshape(Bs, Ls, NA_WIDTH), ys], axis=-1) @ w_out_even[e]
        else:
            o = layer // 2
            op = _fourier_mixer(hp, w_in_odd[o], w_out_odd[o])
            os_ = _fourier_mixer(hs, w_in_odd[o], w_out_odd[o])
        xp = xp + mp[2] * op
        xs = xs + ms[2] * os_
        hp = _modulate(_rmsnorm(xp, norm_ffn_g[layer]), mp[3], mp[4])
        hs = _modulate(_rmsnorm(xs, norm_ffn_g[layer]), ms[3], ms[4])
        xp = xp + mp[5] * _swiglu(hp, ffn_w_gate[layer], ffn_w_up[layer], ffn_w_down[layer])
        xs = xs + ms[5] * _swiglu(hs, ffn_w_gate[layer], ffn_w_up[layer], ffn_w_down[layer])
    y_prompt = _rmsnorm(xp, final_norm_g)
    y_sample = _rmsnorm(xs, final_norm_g)
    new_cache_na_k = jnp.stack(new_k, axis=1)
    new_cache_na_v = jnp.stack(new_v, axis=1)
    new_state_s5 = jnp.stack(new_s, axis=1)
    return (y_prompt, y_sample, new_cache_na_k, new_cache_na_v, new_state_s5)
```

```python
import functools
import math

import numpy as np
import jax
import jax.numpy as jnp
from jax import lax
from jax.experimental import pallas as pl
from jax.experimental.pallas import tpu as pltpu

F32 = jnp.float32
BF16 = jnp.bfloat16

D_MODEL = 1024
NA_HEADS = 8
HEAD_DIM = 64
NA_WIDTH = NA_HEADS * HEAD_DIM
GRID_W = 64
WIN_R = 8
WIN_C = 16
S5_GROUP = 16
S5_GROUPS = 32
S5_STATE = 64
S5_WIDTH = S5_GROUPS * S5_GROUP
FNET_GROUPS = 4
FNET_GROUP_WIDTH = D_MODEL // FNET_GROUPS
D_FF = 2816
EPS = 1e-6

CHUNK = 16
CW = CHUNK * S5_GROUP
N_COND = 3
COND_ROWS = 8
NEG = -1e30
VMEM_LIMIT = 56 * 2 ** 20

_NT = (((1,), (1,)), ((), ()))


def _params(sem, vmem=VMEM_LIMIT):
    return pltpu.CompilerParams(dimension_semantics=sem, vmem_limit_bytes=vmem)


def _dot(a, b):
    return jnp.dot(a, b, preferred_element_type=F32)


def _dot_nt(a, b):
    return lax.dot_general(a, b, _NT, preferred_element_type=F32)


def _norm_mod(x, g, shift, scale):
    ms = jnp.mean(x * x, axis=-1, keepdims=True)
    return (x * lax.rsqrt(ms + EPS) * g) * (1.0 + scale) + shift


def _mod_spec(layer, which, row_fn):
    return pl.BlockSpec((1, 1, D_MODEL),
                        lambda *idx: (layer * COND_ROWS * 6 + row_fn(*idx) * 6 + which, 0, 0))


def _full_spec(shape):
    nd = len(shape)
    return pl.BlockSpec(shape, lambda *idx: (0,) * nd)


def _resident_spec(shape):
    nd = len(shape)
    return pl.BlockSpec(shape, lambda *idx: (0,) * nd, pipeline_mode=pl.Buffered(1))


def _mod_kernel(ct_ref, w_ref, b_ref, o_ref):
    ct = ct_ref[...]
    st = ct * jax.nn.sigmoid(ct)
    w = w_ref[0]
    o_ref[0] = jnp.zeros(o_ref.shape[1:], F32)
    for r in range(N_COND):
        o_ref[0, r:r + 1, :] = jnp.sum(w * st[:, r:r + 1], axis=0, keepdims=True) + b_ref[0]


def _modulation(cond_t, w_mod, b_mod):
    depth, _, n = w_mod.shape
    tn = 1024
    return pl.pallas_call(
        _mod_kernel,
        grid=(depth, n // tn),
        in_specs=[_full_spec((D_MODEL, COND_ROWS)),
                  pl.BlockSpec((1, D_MODEL, tn), lambda l, j: (l, 0, j)),
                  pl.BlockSpec((1, 1, tn), lambda l, j: (l, 0, j))],
        out_specs=pl.BlockSpec((1, COND_ROWS, tn), lambda l, j: (l, 0, j)),
        out_shape=jax.ShapeDtypeStruct((depth, COND_ROWS, n), F32),
        compiler_params=_params(("parallel", "parallel")),
        name="modulation",
    )(cond_t, w_mod, b_mod.reshape(depth, 1, n))


def _inproj_kernel(x_ref, g_ref, sh_ref, sc_ref, w_ref, *o_refs):
    h = _norm_mod(x_ref[...], g_ref[...], sh_ref[0], sc_ref[0]).astype(BF16)
    z = _dot(h, w_ref[...])
    off = 0
    for o in o_refs:
        n = o.shape[-1]
        o[...] = z[:, off:off + n]
        off += n


def _inproj(x, g, mod, layer, row_fn, w, widths, tm, name):
    t = x.shape[0]
    return pl.pallas_call(
        _inproj_kernel,
        grid=(t // tm,),
        in_specs=[pl.BlockSpec((tm, D_MODEL), lambda i: (i, 0)),
                  _full_spec((1, D_MODEL)),
                  _mod_spec(layer, 0, row_fn), _mod_spec(layer, 1, row_fn),
                  _resident_spec(w.shape)],
        out_specs=[pl.BlockSpec((tm, n), lambda i: (i, 0)) for n in widths],
        out_shape=[jax.ShapeDtypeStruct((t, n), F32) for n in widths],
        compiler_params=_params(("parallel",)),
        name=name,
    )(x, g, mod, mod, w)


def _ctx_attn_kernel(q_ref, k_ref, v_ref, o_ref):
    scale = HEAD_DIM ** -0.5
    for h in range(NA_HEADS):
        sl = slice(h * HEAD_DIM, (h + 1) * HEAD_DIM)
        q = q_ref[:, sl].astype(BF16)
        k = k_ref[:, sl].astype(BF16)
        v = v_ref[:, sl].astype(BF16)
        s = _dot_nt(q, k) * scale
        p = jnp.exp(s - jnp.max(s, axis=-1, keepdims=True))
        den = jnp.sum(p, axis=-1, keepdims=True)
        o_ref[:, sl] = _dot(p.astype(BF16), v) / den


def _ctx_attention(q, k, v, seq):
    t = q.shape[0]
    spec = pl.BlockSpec((seq, NA_WIDTH), lambda b: (b, 0))
    return pl.pallas_call(
        _ctx_attn_kernel,
        grid=(t // seq,),
        in_specs=[spec, spec, spec],
        out_specs=spec,
        out_shape=jax.ShapeDtypeStruct((t, NA_WIDTH), F32),
        compiler_params=_params(("parallel",)),
        name="ctx_attention",
    )(q, k, v)


def _na_bias_kernel(rpb_ref, o_ref, *, rows):
    h = pl.program_id(0)
    kr_win = min(WIN_R, rows)
    qc = lax.broadcasted_iota(jnp.int32, (GRID_W, 2 * GRID_W), 0)
    lane = lax.broadcasted_iota(jnp.int32, (GRID_W, 2 * GRID_W), 1)
    kc = lane & (GRID_W - 1)
    cs = jnp.clip(qc - WIN_C // 2, 0, GRID_W - WIN_C)
    col_ok = (kc >= cs) & (kc < cs + WIN_C)
    dc = kc - qc + (WIN_C - 1)
    left = lane < GRID_W
    neg = jnp.full((GRID_W, 2 * GRID_W), NEG, F32)
    toeplitz = []
    for dr in range(2 * WIN_R - 1):
        t = neg
        for d in range(2 * WIN_C - 1):
            t = jnp.where(dc == d, rpb_ref[h, dr, d], t)
        toeplitz.append(jnp.where(col_ok, t, neg))
    for r in range(rows):
        rs = min(max(r - kr_win // 2, 0), rows - kr_win)

        def blk(kr):
            return toeplitz[kr - r + WIN_R - 1] if rs <= kr < rs + kr_win else None

        for j in range(rows // 2):
            a, b = blk(2 * j), blk(2 * j + 1)
            if a is None and b is None:
                pair = neg
            else:
                pair = jnp.where(left, neg if a is None else a, neg if b is None else b)
            o_ref[0, r * GRID_W:(r + 1) * GRID_W, j * 2 * GRID_W:(j + 1) * 2 * GRID_W] = pair


def _na_bias(rpb, seq):
    rows = seq // GRID_W
    return pl.pallas_call(
        functools.partial(_na_bias_kernel, rows=rows),
        grid=(NA_HEADS,),
        in_specs=[pl.BlockSpec(memory_space=pltpu.SMEM)],
        out_specs=pl.BlockSpec((1, seq, seq), lambda h: (h, 0, 0)),
        out_shape=jax.ShapeDtypeStruct((NA_HEADS, seq, seq), F32),
        compiler_params=_params(("parallel",)),
        name="na_bias",
    )(rpb)


def _na_attn_kernel(q_ref, k_ref, v_ref, ck_ref, cv_ref, bias_ref, o_ref):
    scale = HEAD_DIM ** -0.5
    for hl in range(2):
        sl = slice(hl * HEAD_DIM, (hl + 1) * HEAD_DIM)
        q = q_ref[0, :, sl].astype(BF16)
        s_loc = _dot_nt(q, k_ref[0, :, sl].astype(BF16)) * scale + bias_ref[hl]
        s_ctx = _dot_nt(q, ck_ref[0, :, sl].astype(BF16)) * scale
        m = jnp.maximum(jnp.max(s_loc, axis=-1, keepdims=True), jnp.max(s_ctx, axis=-1, keepdims=True))
        p_loc = jnp.exp(s_loc - m)
        p_ctx = jnp.exp(s_ctx - m)
        den = jnp.sum(p_loc, axis=-1, keepdims=True) + jnp.sum(p_ctx, axis=-1, keepdims=True)
        o = _dot(p_loc.astype(BF16), v_ref[0, :, sl].astype(BF16)) + _dot(p_ctx.astype(BF16), cv_ref[0, :, sl].astype(BF16))
        o_ref[0, :, sl] = o / den


def _na_attention(q, k, v, ck, cv, bias):
    nb, seq, _ = q.shape
    past = ck.shape[1]
    tq = 256
    lat = pl.BlockSpec((1, seq, 2 * HEAD_DIM), lambda hp, qb, b: (b, 0, hp))
    ctx = pl.BlockSpec((1, past, 2 * HEAD_DIM), lambda hp, qb, b: (b, 0, hp))
    qo = pl.BlockSpec((1, tq, 2 * HEAD_DIM), lambda hp, qb, b: (b, qb, hp))
    return pl.pallas_call(
        _na_attn_kernel,
        grid=(NA_HEADS // 2, seq // tq, nb),
        in_specs=[qo, lat, lat, ctx, ctx,
                  pl.BlockSpec((2, tq, seq), lambda hp, qb, b: (hp, qb, 0))],
        out_specs=qo,
        out_shape=jax.ShapeDtypeStruct((nb, seq, NA_WIDTH), F32),
        compiler_params=_params(("parallel", "parallel", "arbitrary")),
        name="na_attention",
    )(q, k, v, ck, cv, bias)


def _s5_prep_kernel(lamr_ref, lami_ref, logdt_ref, btr_ref, bti_ref, cr_ref, ci_ref,
                    m_ref, wz_ref, wyt_ref, ar_ref, ai_ref):
    lr = jnp.minimum(lamr_ref[0], -1e-4)
    li = lami_ref[0]
    dt = jnp.exp(logdt_ref[0])
    a = lr * dt
    th = li * dt
    mag = jnp.exp(a)
    lbr = mag * jnp.cos(th)
    lbi = mag * jnp.sin(th)
    den = lr * lr + li * li
    nr = lbr - 1.0
    coef_r = (nr * lr + lbi * li) / den
    coef_i = (lbi * lr - nr * li) / den
    btr, bti = btr_ref[0], bti_ref[0]
    bbr = coef_r * btr - coef_i * bti
    bbi = coef_r * bti + coef_i * btr
    cr, ci = cr_ref[0], ci_ref[0]

    step = lax.broadcasted_iota(jnp.int32, (CHUNK, 2 * S5_STATE), 0).astype(F32)
    fwd = lax.broadcasted_iota(jnp.int32, (CHUNK, 2 * S5_STATE), 1) < S5_STATE

    def powers(e):
        pm = jnp.exp(a * e)
        return pm * jnp.cos(th * e), pm * jnp.sin(th * e)

    pzr, pzi = powers(jnp.where(fwd, CHUNK - 1 - step, step))
    pyr, pyi = powers(jnp.where(fwd, step + 1.0, CHUNK - step))
    p16r, p16i = powers(jnp.full((1, 2 * S5_STATE), float(CHUNK), F32))
    ar_ref[0] = p16r
    ai_ref[0] = p16i

    def outer(p, c):
        return (p[:, None, :] * c[None, :, :]).reshape(CW, 2 * S5_STATE)

    wzr = outer(pzr, bbr) - outer(pzi, bbi)
    wzi = outer(pzr, bbi) + outer(pzi, bbr)
    wz_ref[0, :, 0:128] = wzr.astype(wz_ref.dtype)
    wz_ref[0, :, 128:256] = wzi.astype(wz_ref.dtype)
    wyt_ref[0, :, 0:128] = (outer(pyr, cr) - outer(pyi, ci)).astype(wyt_ref.dtype)
    wyt_ref[0, :, 128:256] = (-(outer(pyi, cr) + outer(pyr, ci))).astype(wyt_ref.dtype)

    ones = jnp.ones((CHUNK, 2 * S5_STATE), F32)
    cwr = outer(ones, cr)
    cwi = outer(ones, ci)
    fwd_rows = lax.broadcasted_iota(jnp.int32, (CW, 2 * S5_STATE), 1) < S5_STATE
    zero = jnp.zeros((CW, 2 * S5_STATE), F32)
    hp = lax.Precision.HIGHEST

    def kk(sel):
        re = jnp.where(sel, wzr, zero)
        im = jnp.where(sel, wzi, zero)
        return (lax.dot_general(re, cwr, _NT, precision=hp, preferred_element_type=F32)
                - lax.dot_general(im, cwi, _NT, precision=hp, preferred_element_type=F32))

    kk_f = kk(fwd_rows)
    kk_b = kk(jnp.logical_not(fwd_rows))
    col_step = lax.broadcasted_iota(jnp.int32, (CW, CW), 1) // S5_GROUP
    m = jnp.zeros((CW, CW), F32)
    for t in range(CHUNK):
        up = (CHUNK - 1 - t) * S5_GROUP
        dn = t * S5_GROUP
        piece_f = kk_f if up == 0 else jnp.concatenate([kk_f[up:], jnp.zeros((up, CW), F32)], axis=0)
        piece_b = kk_b if dn == 0 else jnp.concatenate([jnp.zeros((dn, CW), F32), kk_b[:CW - dn]], axis=0)
        m = jnp.where(col_step == t, piece_f + piece_b, m)
    m_ref[0] = m.astype(m_ref.dtype)


def _s5_prep(lam_re, lam_im, log_step, b_re, b_im, c_re, c_im):
    g = S5_GROUPS
    lanes = 2 * S5_STATE

    def dirs_last(x):
        return x.transpose(1, 0, 2).reshape(g, 1, lanes)

    lamr, lami = dirs_last(lam_re), dirs_last(lam_im)
    logdt = dirs_last(jnp.broadcast_to(log_step[:, :, None], (2, g, S5_STATE)))
    bt = [x.transpose(1, 3, 0, 2).reshape(g, S5_GROUP, lanes) for x in (b_re, b_im)]
    ct = [x.transpose(1, 2, 0, 3).reshape(g, S5_GROUP, lanes) for x in (c_re, c_im)]
    vec = pl.BlockSpec((1, 1, lanes), lambda i: (i, 0, 0))
    mat = pl.BlockSpec((1, S5_GROUP, lanes), lambda i: (i, 0, 0))
    big = pl.BlockSpec((1, CW, CW), lambda i: (i, 0, 0))
    return pl.pallas_call(
        _s5_prep_kernel,
        grid=(g,),
        in_specs=[vec, vec, vec, mat, mat, mat, mat],
        out_specs=[big, big, big, vec, vec],
        out_shape=[jax.ShapeDtypeStruct((g, CW, CW), BF16)] * 3 + [jax.ShapeDtypeStruct((g, 1, lanes), F32)] * 2,
        compiler_params=_params(("parallel",)),
        name="s5_prep",
    )(lamr, lami, logdt, bt[0], bt[1], ct[0], ct[1])


def _s5_state_in_kernel(u_ref, wz_ref, z_ref):
    z_ref[0] = _dot(u_ref[0].astype(BF16), wz_ref[0])


def _s5_state_in(u, wz):
    g, r, _ = u.shape
    spec = pl.BlockSpec((1, r, CW), lambda i: (i, 0, 0))
    return pl.pallas_call(
        _s5_state_in_kernel,
        grid=(g,),
        in_specs=[spec, pl.BlockSpec((1, CW, CW), lambda i: (i, 0, 0))],
        out_specs=spec,
        out_shape=jax.ShapeDtypeStruct((g, r, CW), F32),
        compiler_params=_params(("parallel",)),
        name="s5_state_in",
    )(u, wz)


def _s5_scan_kernel(z_ref, ar_ref, ai_ref, s0_ref, sp_ref, fin_ref, *, nc):
    ar, ai = ar_ref[0], ai_ref[0]
    s = S5_STATE
    fwd = lax.broadcasted_iota(jnp.int32, ar.shape, 1) < s

    def body(k, carry):
        sr, si = carry
        kb = nc - 1 - k
        sp_ref[0, k, :, 0:s] = sr[:, 0:s]
        sp_ref[0, kb, :, s:2 * s] = sr[:, s:2 * s]
        sp_ref[0, k, :, 2 * s:3 * s] = si[:, 0:s]
        sp_ref[0, kb, :, 3 * s:4 * s] = si[:, s:2 * s]
        zr = jnp.where(fwd, z_ref[0, k, :, 0:2 * s], z_ref[0, kb, :, 0:2 * s])
        zi = jnp.where(fwd, z_ref[0, k, :, 2 * s:4 * s], z_ref[0, kb, :, 2 * s:4 * s])
        return ar * sr - ai * si + zr, ar * si + ai * sr + zi

    sr, si = lax.fori_loop(0, nc, body, (s0_ref[0, :, 0:2 * s], s0_ref[0, :, 2 * s:4 * s]))
    fin_ref[0, :, 0:2 * s] = sr
    fin_ref[0, :, 2 * s:4 * s] = si


def _s5_scan(z, ar, ai, s0):
    gp, nc, r, w = z.shape
    zs = pl.BlockSpec((1, nc, r, w), lambda i: (i, 0, 0, 0))
    cs = pl.BlockSpec((1, r, 2 * S5_STATE), lambda i: (i, 0, 0))
    ss = pl.BlockSpec((1, r, w), lambda i: (i, 0, 0))
    return pl.pallas_call(
        functools.partial(_s5_scan_kernel, nc=nc),
        grid=(gp,),
        in_specs=[zs, cs, cs, ss],
        out_specs=[zs, ss],
        out_shape=[jax.ShapeDtypeStruct(z.shape, F32), jax.ShapeDtypeStruct((gp, r, w), F32)],
        compiler_params=_params(("parallel",)),
        name="s5_scan",
    )(z, ar, ai, s0)


def _s5_out_kernel(u_ref, sp_ref, m_ref, wyt_ref, d_ref, y_ref):
    u = u_ref[0]
    y = _dot(u.astype(BF16), m_ref[0]) + _dot_nt(sp_ref[0].astype(BF16), wyt_ref[0]) + d_ref[0] * u
    y_ref[0] = jax.nn.gelu(y)


def _s5_out(u, sp, m, wyt, dvec):
    g, r, _ = u.shape
    spec = pl.BlockSpec((1, r, CW), lambda i: (i, 0, 0))
    wspec = pl.BlockSpec((1, CW, CW), lambda i: (i, 0, 0))
    return pl.pallas_call(
        _s5_out_kernel,
        grid=(g,),
        in_specs=[spec, spec, wspec, wspec, pl.BlockSpec((1, 1, CW), lambda i: (i, 0, 0))],
        out_specs=spec,
        out_shape=jax.ShapeDtypeStruct((g, r, CW), F32),
        compiler_params=_params(("parallel",)),
        name="s5_out",
    )(u, sp, m, wyt, dvec)


def _s5_mixer(u, s0, ops, dvec, nb, seq):
    m, wz, wyt, ar, ai = ops
    g, nc = S5_GROUPS, seq // CHUNK
    r = nc * nb
    uc = u.reshape(nb, nc, CHUNK, g, S5_GROUP).transpose(3, 1, 0, 2, 4).reshape(g, r, CW)
    z = _s5_state_in(uc, wz)
    if nb >= 8:
        z4 = z.reshape(g, nc, nb, CW)
        arb = jnp.broadcast_to(ar, (g, nb, 2 * S5_STATE))
        aib = jnp.broadcast_to(ai, (g, nb, 2 * S5_STATE))
        if s0 is None:
            s0l = jnp.zeros((g, nb, CW), F32)
        else:
            s0l = s0.transpose(3, 0, 2, 1, 4).reshape(g, nb, CW)
        sp4, fin = _s5_scan(z4, arb, aib, s0l)
        sp = sp4.reshape(g, r, CW)
        state = fin.reshape(g, nb, 2, 2, S5_STATE).transpose(1, 3, 2, 0, 4)
    else:
        z4 = z.reshape(g, nc, nb, CW).transpose(1, 2, 0, 3).reshape(1, nc, nb * g, CW)
        arb = jnp.broadcast_to(ar.reshape(1, g, 2 * S5_STATE), (nb, g, 2 * S5_STATE)).reshape(1, nb * g, 2 * S5_STATE)
        aib = jnp.broadcast_to(ai.reshape(1, g, 2 * S5_STATE), (nb, g, 2 * S5_STATE)).reshape(1, nb * g, 2 * S5_STATE)
        if s0 is None:
            s0l = jnp.zeros((1, nb * g, CW), F32)
        else:
            s0l = s0.transpose(0, 3, 2, 1, 4).reshape(1, nb * g, CW)
        sp4, fin = _s5_scan(z4, arb, aib, s0l)
        sp = sp4.reshape(nc, nb, g, CW).transpose(2, 0, 1, 3).reshape(g, r, CW)
        state = fin.reshape(nb, g, 2, 2, S5_STATE).transpose(0, 3, 2, 1, 4)
    yc = _s5_out(uc, sp, m, wyt, dvec)
    y = yc.reshape(g, nc, nb, CHUNK, S5_GROUP).transpose(2, 1, 3, 0, 4).reshape(nb * seq, S5_WIDTH)
    return y, state


def _even_out_kernel(a_ref, y_ref, x_ref, gw_ref, gb_ref, wo_ref, gate_ref, o_ref):
    y = y_ref[...]
    y = y * jax.nn.sigmoid(_dot(y.astype(BF16), gw_ref[...]) + gb_ref[...])
    o = _dot(a_ref[...].astype(BF16), wo_ref[0:NA_WIDTH, :]) + _dot(y.astype(BF16), wo_ref[NA_WIDTH:, :])
    o_ref[...] = x_ref[...] + gate_ref[0] * o


def _even_out(a, y, x, glu_w, glu_b, w_out, mod, layer, row_fn, tm):
    t = x.shape[0]
    half = pl.BlockSpec((tm, NA_WIDTH), lambda i: (i, 0))
    full = pl.BlockSpec((tm, D_MODEL), lambda i: (i, 0))
    return pl.pallas_call(
        _even_out_kernel,
        grid=(t // tm,),
        in_specs=[half, half, full, _resident_spec(glu_w.shape), _full_spec((1, S5_WIDTH)),
                  _resident_spec(w_out.shape), _mod_spec(layer, 2, row_fn)],
        out_specs=full,
        out_shape=jax.ShapeDtypeStruct((t, D_MODEL), F32),
        compiler_params=_params(("parallel",)),
        name="even_out",
    )(a, y, x, glu_w, glu_b, w_out, mod)


def _dft_consts(seq):
    def cs(n):
        k = np.arange(n)
        ang = 2.0 * np.pi * ((k[:, None] * k[None, :]) % n) / n
        return np.cos(ang), np.sin(ang)

    cc, sc = cs(FNET_GROUP_WIDTH)
    cl, sl = cs(seq)
    chan = jnp.asarray(np.concatenate([cc, sc], axis=1), dtype=F32)
    pos = jnp.asarray(np.concatenate([cl, -sl], axis=1), dtype=F32)
    return chan.astype(BF16), pos.astype(BF16)


def _fourier_kernel(x_ref, g_ref, sh_ref, sc_ref, win_ref, chan_ref, pos_ref, wout_ref, gate_ref, o_ref, *, seq):
    x = x_ref[...]
    h = _norm_mod(x, g_ref[...], sh_ref[0], sc_ref[0]).astype(BF16)
    z = _dot(h, win_ref[...]).astype(BF16)
    w = FNET_GROUP_WIDTH
    pos = pos_ref[...]
    parts = []
    for c in range(FNET_GROUPS):
        ab = _dot(z[:, c * w:(c + 1) * w], chan_ref[...]).astype(BF16)
        stacked = jnp.concatenate([ab[:, :w], ab[:, w:]], axis=0)
        parts.append(_dot(pos, stacked))
    f = jnp.concatenate(parts, axis=1) * (1.0 / math.sqrt(seq * w))
    o_ref[...] = x + gate_ref[0] * _dot(f.astype(BF16), wout_ref[...])


def _fourier_mixer(x, g, mod, layer, row_fn, w_in, w_out, seq):
    t = x.shape[0]
    chan, pos = _dft_consts(seq)
    blk = pl.BlockSpec((seq, D_MODEL), lambda b: (b, 0))
    return pl.pallas_call(
        functools.partial(_fourier_kernel, seq=seq),
        grid=(t // seq,),
        in_specs=[blk, _full_spec((1, D_MODEL)), _mod_spec(layer, 0, row_fn), _mod_spec(layer, 1, row_fn),
                  _resident_spec(w_in.shape), _resident_spec(chan.shape), _resident_spec(pos.shape),
                  _resident_spec(w_out.shape), _mod_spec(layer, 2, row_fn)],
        out_specs=blk,
        out_shape=jax.ShapeDtypeStruct((t, D_MODEL), F32),
        compiler_params=_params(("parallel",)),
        name="fourier_mixer",
    )(x, g, mod, mod, w_in, chan, pos, w_out, mod)


def _ffn_kernel(x_ref, g_ref, sh_ref, sc_ref, gate_ref, wg_ref, wu_ref, wd_ref, fg_ref, o_ref, *, tf, final):
    x = x_ref[...]
    h = _norm_mod(x, g_ref[...], sh_ref[0], sc_ref[0]).astype(BF16)
    acc = jnp.zeros(x.shape, F32)
    for j in range(D_FF // tf):
        sl = slice(j * tf, (j + 1) * tf)
        gg = _dot(h, wg_ref[:, sl])
        uu = _dot(h, wu_ref[:, sl])
        act = (gg * jax.nn.sigmoid(gg) * uu).astype(BF16)
        acc = acc + _dot(act, wd_ref[sl, :])
    y = x + gate_ref[0] * acc
    if final:
        ms = jnp.mean(y * y, axis=-1, keepdims=True)
        y = y * lax.rsqrt(ms + EPS) * fg_ref[...]
    o_ref[...] = y


def _ffn(x, g, mod, layer, row_fn, wg, wu, wd, final_g, final, tm):
    t = x.shape[0]
    blk = pl.BlockSpec((tm, D_MODEL), lambda i: (i, 0))
    return pl.pallas_call(
        functools.partial(_ffn_kernel, tf=256, final=final),
        grid=(t // tm,),
        in_specs=[blk, _full_spec((1, D_MODEL)),
                  _mod_spec(layer, 3, row_fn), _mod_spec(layer, 4, row_fn), _mod_spec(layer, 5, row_fn),
                  _resident_spec(wg.shape), _resident_spec(wu.shape), _resident_spec(wd.shape),
                  _full_spec((1, D_MODEL))],
        out_specs=blk,
        out_shape=jax.ShapeDtypeStruct((t, D_MODEL), F32),
        compiler_params=_params(("parallel",)),
        name="ffn",
    )(x, g, mod, mod, mod, wg, wu, wd, final_g)


def kernel(x_prompt, x_sample, cache_na_k, cache_na_v, state_s5, c, c_ctx, w_mod, b_mod, norm_mix_g, norm_ffn_g, w_in_even, na_rpb, s5_lam_re, s5_lam_im, s5_log_step, s5_b_re, s5_b_im, s5_c_re, s5_c_im, s5_d, s5_glu_w, s5_glu_b, w_out_even, w_in_odd, w_out_odd, ffn_w_gate, ffn_w_up, ffn_w_down, final_norm_g):
    bp, lp, _ = x_prompt.shape
    bs, ls, _ = x_sample.shape
    past = cache_na_k.shape[2]
    depth = w_mod.shape[0]
    assert bs + 1 <= N_COND and depth == 2

    cond = jnp.zeros((COND_ROWS, D_MODEL), F32).at[0].set(c_ctx).at[1:1 + bs].set(c)
    mod = _modulation(cond.T, w_mod, b_mod).reshape(depth * COND_ROWS * 6, 1, D_MODEL)

    tm = 512
    streams = [
        dict(x=x_prompt.reshape(bp * lp, D_MODEL), nb=bp, seq=lp, tok_row=lambda i: 0, seq_row=lambda b: 0),
        dict(x=x_sample.reshape(bs * ls, D_MODEL), nb=bs, seq=ls,
             tok_row=lambda i: 1 + (i * tm) // ls, seq_row=lambda b: 1 + b),
    ]

    e = 0
    w_in = w_in_even[e].astype(BF16)
    w_out = w_out_even[e].astype(BF16)
    glu_w = s5_glu_w[e].astype(BF16)
    glu_b = s5_glu_b[e].reshape(1, S5_WIDTH)
    g_mix = norm_mix_g[0].reshape(1, D_MODEL)
    s5_ops = _s5_prep(s5_lam_re[e], s5_lam_im[e], s5_log_step[e], s5_b_re[e], s5_b_im[e], s5_c_re[e], s5_c_im[e])
    dvec = jnp.tile(s5_d[e].reshape(S5_GROUPS, 1, S5_GROUP), (1, CHUNK, 1)).reshape(S5_GROUPS, 1, CW)
    bias = _na_bias(na_rpb[e], ls)

    widths = (NA_WIDTH, NA_WIDTH, NA_WIDTH, S5_WIDTH)
    new_k = new_v = new_s = None
    for si, st in enumerate(streams):
        nb, seq = st["nb"], st["seq"]
        q, k, v, u = _inproj(st["x"], g_mix, mod, 0, st["tok_row"], w_in, widths, tm, "inproj_even")
        if si == 0:
            a = _ctx_attention(q, k, v, seq)
            s0 = None
            new_k = k.reshape(nb, 1, seq, NA_HEADS, HEAD_DIM)
            new_v = v.reshape(nb, 1, seq, NA_HEADS, HEAD_DIM)
        else:
            a = _na_attention(q.reshape(nb, seq, NA_WIDTH), k.reshape(nb, seq, NA_WIDTH), v.reshape(nb, seq, NA_WIDTH),
                              cache_na_k[:, e].reshape(nb, past, NA_WIDTH), cache_na_v[:, e].reshape(nb, past, NA_WIDTH),
                              bias).reshape(nb * seq, NA_WIDTH)
            s0 = state_s5[:, e].astype(F32)
        y, state = _s5_mixer(u, s0, s5_ops, dvec, nb, seq)
        if si == 0:
            new_s = state[:, None]
        st["x"] = _even_out(a, y, st["x"], glu_w, glu_b, w_out, mod, 0, st["tok_row"], tm)

    for layer in range(depth):
        wg = ffn_w_gate[layer].astype(BF16)
        wu = ffn_w_up[layer].astype(BF16)
        wd = ffn_w_down[layer].astype(BF16)
        g_ffn = norm_ffn_g[layer].reshape(1, D_MODEL)
        if layer == 1:
            w_in1 = w_in_odd[0].astype(BF16)
            w_out1 = w_out_odd[0].astype(BF16)
            g_mix1 = norm_mix_g[1].reshape(1, D_MODEL)
            for st in streams:
                st["x"] = _fourier_mixer(st["x"], g_mix1, mod, 1, st["seq_row"], w_in1, w_out1, st["seq"])
        for st in streams:
            st["x"] = _ffn(st["x"], g_ffn, mod, layer, st["tok_row"], wg, wu, wd,
                           final_norm_g.reshape(1, D_MODEL), layer == depth - 1, tm)

    y_prompt = streams[0]["x"].reshape(bp, lp, D_MODEL)
    y_sample = streams[1]["x"].reshape(bs, ls, D_MODEL)
    return (y_prompt, y_sample, new_k, new_v, new_s)
```

```python
import functools
import math

import numpy as np
import jax
import jax.numpy as jnp
from jax import lax
from jax.experimental import pallas as pl
from jax.experimental.pallas import tpu as pltpu

F32 = jnp.float32
BF16 = jnp.bfloat16

D_MODEL = 1024
NA_HEADS = 8
HEAD_DIM = 64
NA_WIDTH = NA_HEADS * HEAD_DIM
GRID_W = 64
WIN_R = 8
WIN_C = 16
S5_GROUP = 16
S5_GROUPS = 32
S5_STATE = 64
S5_WIDTH = S5_GROUPS * S5_GROUP
FNET_GROUPS = 4
FNET_GROUP_WIDTH = D_MODEL // FNET_GROUPS
D_FF = 2816
EPS = 1e-6

CHUNK = 16
CW = CHUNK * S5_GROUP
GROUPS_PER_BLOCK = 128 // S5_GROUP
N_COND = 3
COND_ROWS = 8
NEG = -1e30
VMEM_LIMIT = 56 * 2 ** 20

_NT = (((1,), (1,)), ((), ()))


def _params(sem, vmem=VMEM_LIMIT):
    return pltpu.CompilerParams(dimension_semantics=sem, vmem_limit_bytes=vmem)


def _dot(a, b):
    return jnp.dot(a, b, preferred_element_type=F32)


def _dot_nt(a, b):
    return lax.dot_general(a, b, _NT, preferred_element_type=F32)


def _norm_mod(x, g, shift, scale):
    ms = jnp.mean(x * x, axis=-1, keepdims=True)
    return (x * lax.rsqrt(ms + EPS) * g) * (1.0 + scale) + shift


def _mod_spec(layer, which, row_fn):
    return pl.BlockSpec((1, 1, D_MODEL),
                        lambda *idx: (layer * COND_ROWS * 6 + row_fn(*idx) * 6 + which, 0, 0))


def _full_spec(shape):
    nd = len(shape)
    return pl.BlockSpec(shape, lambda *idx: (0,) * nd)


def _resident_spec(shape):
    nd = len(shape)
    return pl.BlockSpec(shape, lambda *idx: (0,) * nd, pipeline_mode=pl.Buffered(1))


def _mod_kernel(ct_ref, w_ref, b_ref, o_ref):
    ct = ct_ref[...]
    st = ct * jax.nn.sigmoid(ct)
    w = w_ref[0]
    o_ref[0] = jnp.zeros(o_ref.shape[1:], F32)
    for r in range(N_COND):
        o_ref[0, r:r + 1, :] = jnp.sum(w * st[:, r:r + 1], axis=0, keepdims=True) + b_ref[0]


def _modulation(cond_t, w_mod, b_mod):
    depth, _, n = w_mod.shape
    tn = 1024
    return pl.pallas_call(
        _mod_kernel,
        grid=(depth, n // tn),
        in_specs=[_full_spec((D_MODEL, COND_ROWS)),
                  pl.BlockSpec((1, D_MODEL, tn), lambda l, j: (l, 0, j)),
                  pl.BlockSpec((1, 1, tn), lambda l, j: (l, 0, j))],
        out_specs=pl.BlockSpec((1, COND_ROWS, tn), lambda l, j: (l, 0, j)),
        out_shape=jax.ShapeDtypeStruct((depth, COND_ROWS, n), F32),
        compiler_params=_params(("parallel", "parallel")),
        name="modulation",
    )(cond_t, w_mod, b_mod.reshape(depth, 1, n))


def _inproj_kernel(x_ref, g_ref, sh_ref, sc_ref, w_ref, *o_refs):
    h = _norm_mod(x_ref[...], g_ref[...], sh_ref[0], sc_ref[0]).astype(BF16)
    z = _dot(h, w_ref[...])
    off = 0
    for o in o_refs:
        n = o.shape[-1]
        o[...] = z[:, off:off + n]
        off += n


def _inproj(x, g, mod, layer, row_fn, w, widths, tm, name):
    t = x.shape[0]
    return pl.pallas_call(
        _inproj_kernel,
        grid=(t // tm,),
        in_specs=[pl.BlockSpec((tm, D_MODEL), lambda i: (i, 0)),
                  _full_spec((1, D_MODEL)),
                  _mod_spec(layer, 0, row_fn), _mod_spec(layer, 1, row_fn),
                  _resident_spec(w.shape)],
        out_specs=[pl.BlockSpec((tm, n), lambda i: (i, 0)) for n in widths],
        out_shape=[jax.ShapeDtypeStruct((t, n), F32) for n in widths],
        compiler_params=_params(("parallel",)),
        name=name,
    )(x, g, mod, mod, w)


def _ctx_attn_kernel(q_ref, k_ref, v_ref, o_ref):
    scale = HEAD_DIM ** -0.5
    for h in range(NA_HEADS):
        sl = slice(h * HEAD_DIM, (h + 1) * HEAD_DIM)
        q = q_ref[:, sl].astype(BF16)
        k = k_ref[:, sl].astype(BF16)
        v = v_ref[:, sl].astype(BF16)
        s = _dot_nt(q, k) * scale
        p = jnp.exp(s - jnp.max(s, axis=-1, keepdims=True))
        den = jnp.sum(p, axis=-1, keepdims=True)
        o_ref[:, sl] = _dot(p.astype(BF16), v) / den


def _ctx_attention(q, k, v, seq):
    t = q.shape[0]
    spec = pl.BlockSpec((seq, NA_WIDTH), lambda b: (b, 0))
    return pl.pallas_call(
        _ctx_attn_kernel,
        grid=(t // seq,),
        in_specs=[spec, spec, spec],
        out_specs=spec,
        out_shape=jax.ShapeDtypeStruct((t, NA_WIDTH), F32),
        compiler_params=_params(("parallel",)),
        name="ctx_attention",
    )(q, k, v)


def _na_bias_kernel(rpb_ref, o_ref, *, rows):
    h = pl.program_id(0)
    kr_win = min(WIN_R, rows)
    qc = lax.broadcasted_iota(jnp.int32, (GRID_W, 2 * GRID_W), 0)
    lane = lax.broadcasted_iota(jnp.int32, (GRID_W, 2 * GRID_W), 1)
    kc = lane & (GRID_W - 1)
    cs = jnp.clip(qc - WIN_C // 2, 0, GRID_W - WIN_C)
    col_ok = (kc >= cs) & (kc < cs + WIN_C)
    dc = kc - qc + (WIN_C - 1)
    left = lane < GRID_W
    neg = jnp.full((GRID_W, 2 * GRID_W), NEG, F32)
    toeplitz = []
    for dr in range(2 * WIN_R - 1):
        t = neg
        for d in range(2 * WIN_C - 1):
            t = jnp.where(dc == d, rpb_ref[h, dr, d], t)
        toeplitz.append(jnp.where(col_ok, t, neg))
    for r in range(rows):
        rs = min(max(r - kr_win // 2, 0), rows - kr_win)

        def blk(kr):
            return toeplitz[kr - r + WIN_R - 1] if rs <= kr < rs + kr_win else None

        for j in range(rows // 2):
            a, b = blk(2 * j), blk(2 * j + 1)
            if a is None and b is None:
                pair = neg
            else:
                pair = jnp.where(left, neg if a is None else a, neg if b is None else b)
            o_ref[0, r * GRID_W:(r + 1) * GRID_W, j * 2 * GRID_W:(j + 1) * 2 * GRID_W] = pair


def _na_bias(rpb, seq):
    rows = seq // GRID_W
    return pl.pallas_call(
        functools.partial(_na_bias_kernel, rows=rows),
        grid=(NA_HEADS,),
        in_specs=[pl.BlockSpec(memory_space=pltpu.SMEM)],
        out_specs=pl.BlockSpec((1, seq, seq), lambda h: (h, 0, 0)),
        out_shape=jax.ShapeDtypeStruct((NA_HEADS, seq, seq), F32),
        compiler_params=_params(("parallel",)),
        name="na_bias",
    )(rpb)


def _na_attn_kernel(q_ref, k_ref, v_ref, ck_ref, cv_ref, bias_ref, o_ref):
    scale = HEAD_DIM ** -0.5
    for hl in range(2):
        sl = slice(hl * HEAD_DIM, (hl + 1) * HEAD_DIM)
        q = q_ref[0, :, sl].astype(BF16)
        s_loc = _dot_nt(q, k_ref[0, :, sl].astype(BF16)) * scale + bias_ref[hl]
        s_ctx = _dot_nt(q, ck_ref[0, :, sl].astype(BF16)) * scale
        m = jnp.maximum(jnp.max(s_loc, axis=-1, keepdims=True), jnp.max(s_ctx, axis=-1, keepdims=True))
        p_loc = jnp.exp(s_loc - m)
        p_ctx = jnp.exp(s_ctx - m)
        den = jnp.sum(p_loc, axis=-1, keepdims=True) + jnp.sum(p_ctx, axis=-1, keepdims=True)
        o = _dot(p_loc.astype(BF16), v_ref[0, :, sl].astype(BF16)) + _dot(p_ctx.astype(BF16), cv_ref[0, :, sl].astype(BF16))
        o_ref[0, :, sl] = o / den


def _na_attention(q, k, v, ck, cv, bias):
    nb, seq, _ = q.shape
    past = ck.shape[1]
    tq = 256
    lat = pl.BlockSpec((1, seq, 2 * HEAD_DIM), lambda hp, qb, b: (b, 0, hp))
    ctx = pl.BlockSpec((1, past, 2 * HEAD_DIM), lambda hp, qb, b: (b, 0, hp))
    qo = pl.BlockSpec((1, tq, 2 * HEAD_DIM), lambda hp, qb, b: (b, qb, hp))
    return pl.pallas_call(
        _na_attn_kernel,
        grid=(NA_HEADS // 2, seq // tq, nb),
        in_specs=[qo, lat, lat, ctx, ctx,
                  pl.BlockSpec((2, tq, seq), lambda hp, qb, b: (hp, qb, 0))],
        out_specs=qo,
        out_shape=jax.ShapeDtypeStruct((nb, seq, NA_WIDTH), F32),
        compiler_params=_params(("parallel", "parallel", "arbitrary")),
        name="na_attention",
    )(q, k, v, ck, cv, bias)


def _s5_prep_kernel(lamr_ref, lami_ref, logdt_ref, btr_ref, bti_ref, cr_ref, ci_ref,
                    m_ref, wz_ref, wyt_ref, ar_ref, ai_ref):
    lr = jnp.minimum(lamr_ref[0], -1e-4)
    li = lami_ref[0]
    dt = jnp.exp(logdt_ref[0])
    a = lr * dt
    th = li * dt
    mag = jnp.exp(a)
    lbr = mag * jnp.cos(th)
    lbi = mag * jnp.sin(th)
    den = lr * lr + li * li
    nr = lbr - 1.0
    coef_r = (nr * lr + lbi * li) / den
    coef_i = (lbi * lr - nr * li) / den
    btr, bti = btr_ref[0], bti_ref[0]
    bbr = coef_r * btr - coef_i * bti
    bbi = coef_r * bti + coef_i * btr
    cr, ci = cr_ref[0], ci_ref[0]

    step = lax.broadcasted_iota(jnp.int32, (CHUNK, 2 * S5_STATE), 0).astype(F32)
    fwd = lax.broadcasted_iota(jnp.int32, (CHUNK, 2 * S5_STATE), 1) < S5_STATE

    def powers(e):
        pm = jnp.exp(a * e)
        return pm * jnp.cos(th * e), pm * jnp.sin(th * e)

    pzr, pzi = powers(jnp.where(fwd, CHUNK - 1 - step, step))
    pyr, pyi = powers(jnp.where(fwd, step + 1.0, CHUNK - step))
    p16r, p16i = powers(jnp.full((1, 2 * S5_STATE), float(CHUNK), F32))
    ar_ref[0] = p16r
    ai_ref[0] = p16i

    def outer(p, c):
        return (p[:, None, :] * c[None, :, :]).reshape(CW, 2 * S5_STATE)

    wzr = outer(pzr, bbr) - outer(pzi, bbi)
    wzi = outer(pzr, bbi) + outer(pzi, bbr)
    wz_ref[0, :, 0:128] = wzr.astype(wz_ref.dtype)
    wz_ref[0, :, 128:256] = wzi.astype(wz_ref.dtype)
    wyt_ref[0, :, 0:128] = (outer(pyr, cr) - outer(pyi, ci)).astype(wyt_ref.dtype)
    wyt_ref[0, :, 128:256] = (-(outer(pyi, cr) + outer(pyr, ci))).astype(wyt_ref.dtype)

    ones = jnp.ones((CHUNK, 2 * S5_STATE), F32)
    cwr = outer(ones, cr)
    cwi = outer(ones, ci)
    fwd_rows = lax.broadcasted_iota(jnp.int32, (CW, 2 * S5_STATE), 1) < S5_STATE
    zero = jnp.zeros((CW, 2 * S5_STATE), F32)
    hp = lax.Precision.HIGHEST

    def kk(sel):
        re = jnp.where(sel, wzr, zero)
        im = jnp.where(sel, wzi, zero)
        return (lax.dot_general(re, cwr, _NT, precision=hp, preferred_element_type=F32)
                - lax.dot_general(im, cwi, _NT, precision=hp, preferred_element_type=F32))

    kk_f = kk(fwd_rows)
    kk_b = kk(jnp.logical_not(fwd_rows))
    col_step = lax.broadcasted_iota(jnp.int32, (CW, CW), 1) // S5_GROUP
    m = jnp.zeros((CW, CW), F32)
    for t in range(CHUNK):
        up = (CHUNK - 1 - t) * S5_GROUP
        dn = t * S5_GROUP
        piece_f = kk_f if up == 0 else jnp.concatenate([kk_f[up:], jnp.zeros((up, CW), F32)], axis=0)
        piece_b = kk_b if dn == 0 else jnp.concatenate([jnp.zeros((dn, CW), F32), kk_b[:CW - dn]], axis=0)
        m = jnp.where(col_step == t, piece_f + piece_b, m)
    m_ref[0] = m.astype(m_ref.dtype)


def _s5_prep(lam_re, lam_im, log_step, b_re, b_im, c_re, c_im):
    g = S5_GROUPS
    lanes = 2 * S5_STATE

    def dirs_last(x):
        return x.transpose(1, 0, 2).reshape(g, 1, lanes)

    lamr, lami = dirs_last(lam_re), dirs_last(lam_im)
    logdt = dirs_last(jnp.broadcast_to(log_step[:, :, None], (2, g, S5_STATE)))
    bt = [x.transpose(1, 3, 0, 2).reshape(g, S5_GROUP, lanes) for x in (b_re, b_im)]
    ct = [x.transpose(1, 2, 0, 3).reshape(g, S5_GROUP, lanes) for x in (c_re, c_im)]
    vec = pl.BlockSpec((1, 1, lanes), lambda i: (i, 0, 0))
    mat = pl.BlockSpec((1, S5_GROUP, lanes), lambda i: (i, 0, 0))
    big = pl.BlockSpec((1, CW, CW), lambda i: (i, 0, 0))
    return pl.pallas_call(
        _s5_prep_kernel,
        grid=(g,),
        in_specs=[vec, vec, vec, mat, mat, mat, mat],
        out_specs=[big, big, big, vec, vec],
        out_shape=[jax.ShapeDtypeStruct((g, CW, CW), BF16)] * 3 + [jax.ShapeDtypeStruct((g, 1, lanes), F32)] * 2,
        compiler_params=_params(("parallel",)),
        name="s5_prep",
    )(lamr, lami, logdt, bt[0], bt[1], ct[0], ct[1])


def _s5_kernel(u_ref, m_ref, wz_ref, wyt_ref, ar_ref, ai_ref, s0_ref, y_ref, fin_ref,
               x_scr, yx_scr, z_scr, sp_scr, *, nb, nbp, nc, seq):
    s = S5_STATE
    if nbp != nb:
        x_scr[...] = jnp.zeros(x_scr.shape, F32)

    def gather(c, carry):
        row = pl.multiple_of(c * nbp, 8)
        for t in range(CHUNK):
            x_scr[t, pl.ds(row, nb), :] = u_ref[pl.ds(c * CHUNK + t, nb, stride=seq), :]
        return carry

    lax.fori_loop(0, nc, gather, 0)
    fwd = lax.broadcasted_iota(jnp.int32, (nbp, 2 * s), 1) < s

    for gl in range(GROUPS_PER_BLOCK):
        lanes = slice(gl * S5_GROUP, (gl + 1) * S5_GROUP)
        u = jnp.concatenate([x_scr[t, :, lanes] for t in range(CHUNK)], axis=1).astype(BF16)
        z_scr[...] = _dot(u, wz_ref[gl])
        ar = jnp.broadcast_to(ar_ref[gl], (nbp, 2 * s))
        ai = jnp.broadcast_to(ai_ref[gl], (nbp, 2 * s))

        def scan(k, carry):
            sr, si = carry
            up = pl.ds(pl.multiple_of(k * nbp, 8), nbp)
            dn = pl.ds(pl.multiple_of((nc - 1 - k) * nbp, 8), nbp)
            sp_scr[up, 0:s] = sr[:, 0:s]
            sp_scr[dn, s:2 * s] = sr[:, s:2 * s]
            sp_scr[up, 2 * s:3 * s] = si[:, 0:s]
            sp_scr[dn, 3 * s:4 * s] = si[:, s:2 * s]
            zr = jnp.where(fwd, z_scr[up, 0:2 * s], z_scr[dn, 0:2 * s])
            zi = jnp.where(fwd, z_scr[up, 2 * s:4 * s], z_scr[dn, 2 * s:4 * s])
            return ar * sr - ai * si + zr, ar * si + ai * sr + zi

        sr, si = lax.fori_loop(0, nc, scan, (s0_ref[gl, :, 0:2 * s], s0_ref[gl, :, 2 * s:4 * s]))
        fin_ref[gl, :, 0:2 * s] = sr
        fin_ref[gl, :, 2 * s:4 * s] = si
        y = _dot(u, m_ref[gl]) + _dot_nt(sp_scr[...].astype(BF16), wyt_ref[gl])
        for t in range(CHUNK):
            yx_scr[t, :, lanes] = y[:, t * S5_GROUP:(t + 1) * S5_GROUP]

    def scatter(c, carry):
        row = pl.multiple_of(c * nbp, 8)
        for t in range(CHUNK):
            y_ref[pl.ds(c * CHUNK + t, nb, stride=seq), :] = yx_scr[t, pl.ds(row, nb), :]
        return carry

    lax.fori_loop(0, nc, scatter, 0)


def _s5_mixer(u, s0, ops, nb, seq):
    m, wz, wyt, ar, ai = ops
    g, nc = S5_GROUPS, seq // CHUNK
    nbp = -(-nb // 8) * 8
    r = nc * nbp
    gpb = GROUPS_PER_BLOCK
    s0l = jnp.zeros((g, nbp, CW), F32)
    if s0 is not None:
        s0l = s0l.at[:, :nb].set(s0.transpose(3, 0, 2, 1, 4).reshape(g, nb, CW))
    col = pl.BlockSpec((nb * seq, gpb * S5_GROUP), lambda j: (0, j))
    wspec = pl.BlockSpec((gpb, CW, CW), lambda j: (j, 0, 0))
    cspec = pl.BlockSpec((gpb, 1, 2 * S5_STATE), lambda j: (j, 0, 0))
    sspec = pl.BlockSpec((gpb, nbp, CW), lambda j: (j, 0, 0))
    y, fin = pl.pallas_call(
        functools.partial(_s5_kernel, nb=nb, nbp=nbp, nc=nc, seq=seq),
        grid=(g // gpb,),
        in_specs=[col, wspec, wspec, wspec, cspec, cspec, sspec],
        out_specs=[col, sspec],
        out_shape=[jax.ShapeDtypeStruct((nb * seq, S5_WIDTH), F32), jax.ShapeDtypeStruct((g, nbp, CW), F32)],
        scratch_shapes=[pltpu.VMEM((CHUNK, r, gpb * S5_GROUP), F32), pltpu.VMEM((CHUNK, r, gpb * S5_GROUP), F32),
                        pltpu.VMEM((r, CW), F32), pltpu.VMEM((r, CW), F32)],
        compiler_params=_params(("parallel",)),
        name="s5_scan",
    )(u, m, wz, wyt, ar, ai, s0l)
    state = fin[:, :nb].reshape(g, nb, 2, 2, S5_STATE).transpose(1, 3, 2, 0, 4)
    return y, state


def _even_out_kernel(a_ref, y_ref, u_ref, x_ref, d_ref, gw_ref, gb_ref, wo_ref, gate_ref, o_ref):
    y = jax.nn.gelu(y_ref[...] + d_ref[...] * u_ref[...])
    y = y * jax.nn.sigmoid(_dot(y.astype(BF16), gw_ref[...]) + gb_ref[...])
    o = _dot(a_ref[...].astype(BF16), wo_ref[0:NA_WIDTH, :]) + _dot(y.astype(BF16), wo_ref[NA_WIDTH:, :])
    o_ref[...] = x_ref[...] + gate_ref[0] * o


def _even_out(a, y, u, x, d, glu_w, glu_b, w_out, mod, layer, row_fn, tm):
    t = x.shape[0]
    half = pl.BlockSpec((tm, NA_WIDTH), lambda i: (i, 0))
    full = pl.BlockSpec((tm, D_MODEL), lambda i: (i, 0))
    return pl.pallas_call(
        _even_out_kernel,
        grid=(t // tm,),
        in_specs=[half, half, half, full, _full_spec((1, S5_WIDTH)), _resident_spec(glu_w.shape),
                  _full_spec((1, S5_WIDTH)), _resident_spec(w_out.shape), _mod_spec(layer, 2, row_fn)],
        out_specs=full,
        out_shape=jax.ShapeDtypeStruct((t, D_MODEL), F32),
        compiler_params=_params(("parallel",)),
        name="even_out",
    )(a, y, u, x, d, glu_w, glu_b, w_out, mod)


def _dft_consts(seq):
    def cs(n):
        k = np.arange(n)
        ang = 2.0 * np.pi * ((k[:, None] * k[None, :]) % n) / n
        return np.cos(ang), np.sin(ang)

    cc, sc = cs(FNET_GROUP_WIDTH)
    cl, sl = cs(seq)
    chan = jnp.asarray(np.concatenate([cc, sc], axis=1), dtype=F32)
    pos = jnp.asarray(np.concatenate([cl, -sl], axis=1), dtype=F32)
    return chan.astype(BF16), pos.astype(BF16)


def _fourier_kernel(x_ref, g_ref, sh_ref, sc_ref, win_ref, chan_ref, pos_ref, wout_ref, gate_ref, o_ref, *, seq):
    x = x_ref[...]
    h = _norm_mod(x, g_ref[...], sh_ref[0], sc_ref[0]).astype(BF16)
    z = _dot(h, win_ref[...]).astype(BF16)
    w = FNET_GROUP_WIDTH
    pos = pos_ref[...]
    parts = []
    for c in range(FNET_GROUPS):
        ab = _dot(z[:, c * w:(c + 1) * w], chan_ref[...]).astype(BF16)
        stacked = jnp.concatenate([ab[:, :w], ab[:, w:]], axis=0)
        parts.append(_dot(pos, stacked))
    f = jnp.concatenate(parts, axis=1) * (1.0 / math.sqrt(seq * w))
    o_ref[...] = x + gate_ref[0] * _dot(f.astype(BF16), wout_ref[...])


def _fourier_mixer(x, g, mod, layer, row_fn, w_in, w_out, seq):
    t = x.shape[0]
    chan, pos = _dft_consts(seq)
    blk = pl.BlockSpec((seq, D_MODEL), lambda b: (b, 0))
    return pl.pallas_call(
        functools.partial(_fourier_kernel, seq=seq),
        grid=(t // seq,),
        in_specs=[blk, _full_spec((1, D_MODEL)), _mod_spec(layer, 0, row_fn), _mod_spec(layer, 1, row_fn),
                  _resident_spec(w_in.shape), _resident_spec(chan.shape), _resident_spec(pos.shape),
                  _resident_spec(w_out.shape), _mod_spec(layer, 2, row_fn)],
        out_specs=blk,
        out_shape=jax.ShapeDtypeStruct((t, D_MODEL), F32),
        compiler_params=_params(("parallel",)),
        name="fourier_mixer",
    )(x, g, mod, mod, w_in, chan, pos, w_out, mod)


def _ffn_kernel(x_ref, g_ref, sh_ref, sc_ref, gate_ref, wg_ref, wu_ref, wd_ref, fg_ref, o_ref, *, tf, final):
    x = x_ref[...]
    h = _norm_mod(x, g_ref[...], sh_ref[0], sc_ref[0]).astype(BF16)
    acc = jnp.zeros(x.shape, F32)
    for j in range(D_FF // tf):
        sl = slice(j * tf, (j + 1) * tf)
        gg = _dot(h, wg_ref[:, sl])
        uu = _dot(h, wu_ref[:, sl])
        act = (gg * jax.nn.sigmoid(gg) * uu).astype(BF16)
        acc = acc + _dot(act, wd_ref[sl, :])
    y = x + gate_ref[0] * acc
    if final:
        ms = jnp.mean(y * y, axis=-1, keepdims=True)
        y = y * lax.rsqrt(ms + EPS) * fg_ref[...]
    o_ref[...] = y


def _ffn(x, g, mod, layer, row_fn, wg, wu, wd, final_g, final, tm):
    t = x.shape[0]
    blk = pl.BlockSpec((tm, D_MODEL), lambda i: (i, 0))
    return pl.pallas_call(
        functools.partial(_ffn_kernel, tf=256, final=final),
        grid=(t // tm,),
        in_specs=[blk, _full_spec((1, D_MODEL)),
                  _mod_spec(layer, 3, row_fn), _mod_spec(layer, 4, row_fn), _mod_spec(layer, 5, row_fn),
                  _resident_spec(wg.shape), _resident_spec(wu.shape), _resident_spec(wd.shape),
                  _full_spec((1, D_MODEL))],
        out_specs=blk,
        out_shape=jax.ShapeDtypeStruct((t, D_MODEL), F32),
        compiler_params=_params(("parallel",)),
        name="ffn",
    )(x, g, mod, mod, mod, wg, wu, wd, final_g)


def kernel(x_prompt, x_sample, cache_na_k, cache_na_v, state_s5, c, c_ctx, w_mod, b_mod, norm_mix_g, norm_ffn_g, w_in_even, na_rpb, s5_lam_re, s5_lam_im, s5_log_step, s5_b_re, s5_b_im, s5_c_re, s5_c_im, s5_d, s5_glu_w, s5_glu_b, w_out_even, w_in_odd, w_out_odd, ffn_w_gate, ffn_w_up, ffn_w_down, final_norm_g):
    bp, lp, _ = x_prompt.shape
    bs, ls, _ = x_sample.shape
    past = cache_na_k.shape[2]
    depth = w_mod.shape[0]
    assert bs + 1 <= N_COND and depth == 2

    cond = jnp.zeros((COND_ROWS, D_MODEL), F32).at[0].set(c_ctx).at[1:1 + bs].set(c)
    mod = _modulation(cond.T, w_mod, b_mod).reshape(depth * COND_ROWS * 6, 1, D_MODEL)

    tm = 512
    streams = [
        dict(x=x_prompt.reshape(bp * lp, D_MODEL), nb=bp, seq=lp, tok_row=lambda i: 0, seq_row=lambda b: 0),
        dict(x=x_sample.reshape(bs * ls, D_MODEL), nb=bs, seq=ls,
             tok_row=lambda i: 1 + (i * tm) // ls, seq_row=lambda b: 1 + b),
    ]

    e = 0
    w_in = w_in_even[e].astype(BF16)
    w_out = w_out_even[e].astype(BF16)
    glu_w = s5_glu_w[e].astype(BF16)
    glu_b = s5_glu_b[e].reshape(1, S5_WIDTH)
    g_mix = norm_mix_g[0].reshape(1, D_MODEL)
    s5_ops = _s5_prep(s5_lam_re[e], s5_lam_im[e], s5_log_step[e], s5_b_re[e], s5_b_im[e], s5_c_re[e], s5_c_im[e])
    s5_dvec = s5_d[e].reshape(1, S5_WIDTH)
    bias = _na_bias(na_rpb[e], ls)

    widths = (NA_WIDTH, NA_WIDTH, NA_WIDTH, S5_WIDTH)
    new_k = new_v = new_s = None
    for si, st in enumerate(streams):
        nb, seq = st["nb"], st["seq"]
        q, k, v, u = _inproj(st["x"], g_mix, mod, 0, st["tok_row"], w_in, widths, tm, "inproj_even")
        if si == 0:
            a = _ctx_attention(q, k, v, seq)
            s0 = None
            new_k = k.reshape(nb, 1, seq, NA_HEADS, HEAD_DIM)
            new_v = v.reshape(nb, 1, seq, NA_HEADS, HEAD_DIM)
        else:
            a = _na_attention(q.reshape(nb, seq, NA_WIDTH), k.reshape(nb, seq, NA_WIDTH), v.reshape(nb, seq, NA_WIDTH),
                              cache_na_k[:, e].reshape(nb, past, NA_WIDTH), cache_na_v[:, e].reshape(nb, past, NA_WIDTH),
                              bias).reshape(nb * seq, NA_WIDTH)
            s0 = state_s5[:, e].astype(F32)
        y, state = _s5_mixer(u, s0, s5_ops, nb, seq)
        if si == 0:
            new_s = state[:, None]
        st["x"] = _even_out(a, y, u, st["x"], s5_dvec, glu_w, glu_b, w_out, mod, 0, st["tok_row"], tm)

    for layer in range(depth):
        wg = ffn_w_gate[layer].astype(BF16)
        wu = ffn_w_up[layer].astype(BF16)
        wd = ffn_w_down[layer].astype(BF16)
        g_ffn = norm_ffn_g[layer].reshape(1, D_MODEL)
        if layer == 1:
            w_in1 = w_in_odd[0].astype(BF16)
            w_out1 = w_out_odd[0].astype(BF16)
            g_mix1 = norm_mix_g[1].reshape(1, D_MODEL)
            for st in streams:
                st["x"] = _fourier_mixer(st["x"], g_mix1, mod, 1, st["seq_row"], w_in1, w_out1, st["seq"])
        for st in streams:
            st["x"] = _ffn(st["x"], g_ffn, mod, layer, st["tok_row"], wg, wu, wd,
                           final_norm_g.reshape(1, D_MODEL), layer == depth - 1, tm)

    y_prompt = streams[0]["x"].reshape(bp, lp, D_MODEL)
    y_sample = streams[1]["x"].reshape(bs, ls, D_MODEL)
    return (y_prompt, y_sample, new_k, new_v, new_s)
```

```python
import functools
import math

import numpy as np
import jax
import jax.numpy as jnp
from jax import lax
from jax.experimental import pallas as pl
from jax.experimental.pallas import tpu as pltpu

F32 = jnp.float32
BF16 = jnp.bfloat16

D_MODEL = 1024
NA_HEADS = 8
HEAD_DIM = 64
NA_WIDTH = NA_HEADS * HEAD_DIM
GRID_W = 64
WIN_R = 8
WIN_C = 16
S5_GROUP = 16
S5_GROUPS = 32
S5_STATE = 64
S5_WIDTH = S5_GROUPS * S5_GROUP
FNET_GROUPS = 4
FNET_GROUP_WIDTH = D_MODEL // FNET_GROUPS
D_FF = 2816
EPS = 1e-6

CHUNK = 16
CW = CHUNK * S5_GROUP
GROUPS_PER_BLOCK = 128 // S5_GROUP
N_COND = 3
COND_ROWS = 8
NEG = -1e30
VMEM_LIMIT = 56 * 2 ** 20

_NT = (((1,), (1,)), ((), ()))


def _params(sem, vmem=VMEM_LIMIT):
    return pltpu.CompilerParams(dimension_semantics=sem, vmem_limit_bytes=vmem)


def _dot(a, b):
    return jnp.dot(a, b, preferred_element_type=F32)


def _dot_nt(a, b):
    return lax.dot_general(a, b, _NT, preferred_element_type=F32)


def _norm_mod(x, g, shift, scale):
    ms = jnp.mean(x * x, axis=-1, keepdims=True)
    return (x * lax.rsqrt(ms + EPS) * g) * (1.0 + scale) + shift


def _mod_spec(layer, which, row_fn):
    return pl.BlockSpec((1, 1, D_MODEL),
                        lambda *idx: (layer * COND_ROWS * 6 + row_fn(*idx) * 6 + which, 0, 0))


def _full_spec(shape):
    nd = len(shape)
    return pl.BlockSpec(shape, lambda *idx: (0,) * nd)


def _resident_spec(shape):
    nd = len(shape)
    return pl.BlockSpec(shape, lambda *idx: (0,) * nd, pipeline_mode=pl.Buffered(1))


def _mod_kernel(ct_ref, w_ref, b_ref, o_ref):
    ct = ct_ref[...]
    st = ct * jax.nn.sigmoid(ct)
    w = w_ref[0]
    o_ref[0] = jnp.zeros(o_ref.shape[1:], F32)
    for r in range(N_COND):
        o_ref[0, r:r + 1, :] = jnp.sum(w * st[:, r:r + 1], axis=0, keepdims=True) + b_ref[0]


def _modulation(cond_t, w_mod, b_mod):
    depth, _, n = w_mod.shape
    tn = 1024
    return pl.pallas_call(
        _mod_kernel,
        grid=(depth, n // tn),
        in_specs=[_full_spec((D_MODEL, COND_ROWS)),
                  pl.BlockSpec((1, D_MODEL, tn), lambda l, j: (l, 0, j)),
                  pl.BlockSpec((1, 1, tn), lambda l, j: (l, 0, j))],
        out_specs=pl.BlockSpec((1, COND_ROWS, tn), lambda l, j: (l, 0, j)),
        out_shape=jax.ShapeDtypeStruct((depth, COND_ROWS, n), F32),
        compiler_params=_params(("parallel", "parallel")),
        name="modulation",
    )(cond_t, w_mod, b_mod.reshape(depth, 1, n))


def _inproj_kernel(x_ref, g_ref, sh_ref, sc_ref, w_ref, *o_refs):
    h = _norm_mod(x_ref[...], g_ref[...], sh_ref[0], sc_ref[0]).astype(BF16)
    z = _dot(h, w_ref[...])
    for i, o in enumerate(o_refs):
        o[...] = z[:, i * NA_WIDTH:(i + 1) * NA_WIDTH]


def _inproj(x, g, mod, layer, row_fn, w, tm):
    t = x.shape[0]
    flat = pl.BlockSpec((tm, NA_WIDTH), lambda i: (i, 0))
    out_specs = [flat] * 4
    out_shape = [jax.ShapeDtypeStruct((t, NA_WIDTH), F32)] * 4
    return pl.pallas_call(
        _inproj_kernel,
        grid=(t // tm,),
        in_specs=[pl.BlockSpec((tm, D_MODEL), lambda i: (i, 0)),
                  _full_spec((1, D_MODEL)),
                  _mod_spec(layer, 0, row_fn), _mod_spec(layer, 1, row_fn),
                  _resident_spec(w.shape)],
        out_specs=out_specs,
        out_shape=out_shape,
        compiler_params=_params(("parallel",)),
        name="inproj_even",
    )(x, g, mod, mod, w)


def _ctx_attn_kernel(q_ref, k_ref, v_ref, o_ref):
    scale = HEAD_DIM ** -0.5
    for h in range(NA_HEADS):
        sl = slice(h * HEAD_DIM, (h + 1) * HEAD_DIM)
        q = q_ref[:, sl].astype(BF16)
        k = k_ref[:, sl].astype(BF16)
        v = v_ref[:, sl].astype(BF16)
        s = _dot_nt(q, k) * scale
        p = jnp.exp(s - jnp.max(s, axis=-1, keepdims=True))
        den = jnp.sum(p, axis=-1, keepdims=True)
        o_ref[:, sl] = _dot(p.astype(BF16), v) / den


def _ctx_attention(q, k, v, seq):
    t = q.shape[0]
    spec = pl.BlockSpec((seq, NA_WIDTH), lambda b: (b, 0))
    return pl.pallas_call(
        _ctx_attn_kernel,
        grid=(t // seq,),
        in_specs=[spec, spec, spec],
        out_specs=spec,
        out_shape=jax.ShapeDtypeStruct((t, NA_WIDTH), F32),
        compiler_params=_params(("parallel",)),
        name="ctx_attention",
    )(q, k, v)


def _na_bias_kernel(rpb_ref, o_ref, *, rows):
    h = pl.program_id(0)
    kr_win = min(WIN_R, rows)
    qc = lax.broadcasted_iota(jnp.int32, (GRID_W, 2 * GRID_W), 0)
    lane = lax.broadcasted_iota(jnp.int32, (GRID_W, 2 * GRID_W), 1)
    kc = lane & (GRID_W - 1)
    cs = jnp.clip(qc - WIN_C // 2, 0, GRID_W - WIN_C)
    col_ok = (kc >= cs) & (kc < cs + WIN_C)
    dc = kc - qc + (WIN_C - 1)
    left = lane < GRID_W
    neg = jnp.full((GRID_W, 2 * GRID_W), NEG, F32)
    toeplitz = []
    for dr in range(2 * WIN_R - 1):
        t = neg
        for d in range(2 * WIN_C - 1):
            t = jnp.where(dc == d, rpb_ref[h, dr, d], t)
        toeplitz.append(jnp.where(col_ok, t, neg))
    for r in range(rows):
        rs = min(max(r - kr_win // 2, 0), rows - kr_win)

        def blk(kr):
            return toeplitz[kr - r + WIN_R - 1] if rs <= kr < rs + kr_win else None

        for j in range(rows // 2):
            a, b = blk(2 * j), blk(2 * j + 1)
            if a is None and b is None:
                pair = neg
            else:
                pair = jnp.where(left, neg if a is None else a, neg if b is None else b)
            o_ref[0, r * GRID_W:(r + 1) * GRID_W, j * 2 * GRID_W:(j + 1) * 2 * GRID_W] = pair


def _na_bias(rpb, seq):
    rows = seq // GRID_W
    return pl.pallas_call(
        functools.partial(_na_bias_kernel, rows=rows),
        grid=(NA_HEADS,),
        in_specs=[pl.BlockSpec(memory_space=pltpu.SMEM)],
        out_specs=pl.BlockSpec((1, seq, seq), lambda h: (h, 0, 0)),
        out_shape=jax.ShapeDtypeStruct((NA_HEADS, seq, seq), F32),
        compiler_params=_params(("parallel",)),
        name="na_bias",
    )(rpb)


def _na_attn_kernel(q_ref, k_ref, v_ref, ck_ref, cv_ref, bias_ref, o_ref):
    scale = HEAD_DIM ** -0.5
    for hl in range(2):
        sl = slice(hl * HEAD_DIM, (hl + 1) * HEAD_DIM)
        q = q_ref[0, :, sl].astype(BF16)
        s_loc = _dot_nt(q, k_ref[0, :, sl].astype(BF16)) * scale + bias_ref[hl]
        s_ctx = _dot_nt(q, ck_ref[0, :, sl].astype(BF16)) * scale
        m = jnp.maximum(jnp.max(s_loc, axis=-1, keepdims=True), jnp.max(s_ctx, axis=-1, keepdims=True))
        p_loc = jnp.exp(s_loc - m)
        p_ctx = jnp.exp(s_ctx - m)
        den = jnp.sum(p_loc, axis=-1, keepdims=True) + jnp.sum(p_ctx, axis=-1, keepdims=True)
        o = _dot(p_loc.astype(BF16), v_ref[0, :, sl].astype(BF16)) + _dot(p_ctx.astype(BF16), cv_ref[0, :, sl].astype(BF16))
        o_ref[0, :, sl] = o / den


def _na_attention(q, k, v, ck, cv, bias):
    nb, seq, _ = q.shape
    past = ck.shape[1]
    tq = 256
    lat = pl.BlockSpec((1, seq, 2 * HEAD_DIM), lambda hp, qb, b: (b, 0, hp))
    ctx = pl.BlockSpec((1, past, 2 * HEAD_DIM), lambda hp, qb, b: (b, 0, hp))
    qo = pl.BlockSpec((1, tq, 2 * HEAD_DIM), lambda hp, qb, b: (b, qb, hp))
    return pl.pallas_call(
        _na_attn_kernel,
        grid=(NA_HEADS // 2, seq // tq, nb),
        in_specs=[qo, lat, lat, ctx, ctx,
                  pl.BlockSpec((2, tq, seq), lambda hp, qb, b: (hp, qb, 0))],
        out_specs=qo,
        out_shape=jax.ShapeDtypeStruct((nb, seq, NA_WIDTH), F32),
        compiler_params=_params(("parallel", "parallel", "arbitrary")),
        name="na_attention",
    )(q, k, v, ck, cv, bias)


def _s5_prep_kernel(lamr_ref, lami_ref, logdt_ref, btr_ref, bti_ref, cr_ref, ci_ref,
                    m_ref, wz_ref, wyt_ref, ar_ref, ai_ref):
    lr = jnp.minimum(lamr_ref[0], -1e-4)
    li = lami_ref[0]
    dt = jnp.exp(logdt_ref[0])
    a = lr * dt
    th = li * dt
    mag = jnp.exp(a)
    lbr = mag * jnp.cos(th)
    lbi = mag * jnp.sin(th)
    den = lr * lr + li * li
    nr = lbr - 1.0
    coef_r = (nr * lr + lbi * li) / den
    coef_i = (lbi * lr - nr * li) / den
    btr, bti = btr_ref[0], bti_ref[0]
    bbr = coef_r * btr - coef_i * bti
    bbi = coef_r * bti + coef_i * btr
    cr, ci = cr_ref[0], ci_ref[0]

    step = lax.broadcasted_iota(jnp.int32, (CHUNK, 2 * S5_STATE), 0).astype(F32)
    fwd = lax.broadcasted_iota(jnp.int32, (CHUNK, 2 * S5_STATE), 1) < S5_STATE

    def powers(e):
        pm = jnp.exp(a * e)
        return pm * jnp.cos(th * e), pm * jnp.sin(th * e)

    pzr, pzi = powers(jnp.where(fwd, CHUNK - 1 - step, step))
    pyr, pyi = powers(jnp.where(fwd, step + 1.0, CHUNK - step))
    p16r, p16i = powers(jnp.full((1, 2 * S5_STATE), float(CHUNK), F32))
    ar_ref[0] = p16r
    ai_ref[0] = p16i

    def outer(p, c):
        return (p[:, None, :] * c[None, :, :]).reshape(CW, 2 * S5_STATE)

    wzr = outer(pzr, bbr) - outer(pzi, bbi)
    wzi = outer(pzr, bbi) + outer(pzi, bbr)
    wz_ref[0, :, 0:128] = wzr.astype(wz_ref.dtype)
    wz_ref[0, :, 128:256] = wzi.astype(wz_ref.dtype)
    wyt_ref[0, :, 0:128] = (outer(pyr, cr) - outer(pyi, ci)).astype(wyt_ref.dtype)
    wyt_ref[0, :, 128:256] = (-(outer(pyi, cr) + outer(pyr, ci))).astype(wyt_ref.dtype)

    ones = jnp.ones((CHUNK, 2 * S5_STATE), F32)
    cwr = outer(ones, cr)
    cwi = outer(ones, ci)
    fwd_rows = lax.broadcasted_iota(jnp.int32, (CW, 2 * S5_STATE), 1) < S5_STATE
    zero = jnp.zeros((CW, 2 * S5_STATE), F32)
    hp = lax.Precision.HIGHEST

    def kk(sel):
        re = jnp.where(sel, wzr, zero)
        im = jnp.where(sel, wzi, zero)
        return (lax.dot_general(re, cwr, _NT, precision=hp, preferred_element_type=F32)
                - lax.dot_general(im, cwi, _NT, precision=hp, preferred_element_type=F32))

    kk_f = kk(fwd_rows)
    kk_b = kk(jnp.logical_not(fwd_rows))
    col_step = lax.broadcasted_iota(jnp.int32, (CW, CW), 1) // S5_GROUP
    m = jnp.zeros((CW, CW), F32)
    for t in range(CHUNK):
        up = (CHUNK - 1 - t) * S5_GROUP
        dn = t * S5_GROUP
        piece_f = kk_f if up == 0 else jnp.concatenate([kk_f[up:], jnp.zeros((up, CW), F32)], axis=0)
        piece_b = kk_b if dn == 0 else jnp.concatenate([jnp.zeros((dn, CW), F32), kk_b[:CW - dn]], axis=0)
        m = jnp.where(col_step == t, piece_f + piece_b, m)
    m_ref[0] = m.astype(m_ref.dtype)


def _s5_prep(lam_re, lam_im, log_step, b_re, b_im, c_re, c_im):
    g = S5_GROUPS
    lanes = 2 * S5_STATE

    def dirs_last(x):
        return x.transpose(1, 0, 2).reshape(g, 1, lanes)

    lamr, lami = dirs_last(lam_re), dirs_last(lam_im)
    logdt = dirs_last(jnp.broadcast_to(log_step[:, :, None], (2, g, S5_STATE)))
    bt = [x.transpose(1, 3, 0, 2).reshape(g, S5_GROUP, lanes) for x in (b_re, b_im)]
    ct = [x.transpose(1, 2, 0, 3).reshape(g, S5_GROUP, lanes) for x in (c_re, c_im)]
    vec = pl.BlockSpec((1, 1, lanes), lambda i: (i, 0, 0))
    mat = pl.BlockSpec((1, S5_GROUP, lanes), lambda i: (i, 0, 0))
    big = pl.BlockSpec((1, CW, CW), lambda i: (i, 0, 0))
    return pl.pallas_call(
        _s5_prep_kernel,
        grid=(g,),
        in_specs=[vec, vec, vec, mat, mat, mat, mat],
        out_specs=[big, big, big, vec, vec],
        out_shape=[jax.ShapeDtypeStruct((g, CW, CW), BF16)] * 3 + [jax.ShapeDtypeStruct((g, 1, lanes), F32)] * 2,
        compiler_params=_params(("parallel",)),
        name="s5_prep",
    )(lamr, lami, logdt, bt[0], bt[1], ct[0], ct[1])


def _s5_kernel(u_ref, m_ref, wz_ref, wyt_ref, ar_ref, ai_ref, s0_ref, y_ref, fin_ref,
               x_scr, xt_scr, z_scr, sp_scr, *, nb, nbp, nc, seq):
    s = S5_STATE
    if nbp != nb:
        x_scr[...] = jnp.zeros(x_scr.shape, F32)

    def gather(c, carry):
        row = pl.multiple_of(c * nbp, 8)
        for t in range(CHUNK):
            x_scr[t, pl.ds(row, nb), :] = u_ref[pl.ds(c * CHUNK + t, nb, stride=seq), :]
        return carry

    lax.fori_loop(0, nc, gather, 0)
    for t in range(CHUNK):
        xt_scr[t] = x_scr[t].T
    fwd = lax.broadcasted_iota(jnp.int32, (nbp, 2 * s), 1) < s

    for gl in range(GROUPS_PER_BLOCK):
        rows = slice(gl * S5_GROUP, (gl + 1) * S5_GROUP)
        ut = jnp.concatenate([xt_scr[t, rows, :] for t in range(CHUNK)], axis=0)
        u = ut.T.astype(BF16)
        z_scr[...] = _dot(u, wz_ref[gl])
        ar = jnp.broadcast_to(ar_ref[gl], (nbp, 2 * s))
        ai = jnp.broadcast_to(ai_ref[gl], (nbp, 2 * s))

        def scan(k, carry):
            sr, si = carry
            up = pl.ds(pl.multiple_of(k * nbp, 8), nbp)
            dn = pl.ds(pl.multiple_of((nc - 1 - k) * nbp, 8), nbp)
            sp_scr[up, 0:s] = sr[:, 0:s]
            sp_scr[dn, s:2 * s] = sr[:, s:2 * s]
            sp_scr[up, 2 * s:3 * s] = si[:, 0:s]
            sp_scr[dn, 3 * s:4 * s] = si[:, s:2 * s]
            zr = jnp.where(fwd, z_scr[up, 0:2 * s], z_scr[dn, 0:2 * s])
            zi = jnp.where(fwd, z_scr[up, 2 * s:4 * s], z_scr[dn, 2 * s:4 * s])
            return ar * sr - ai * si + zr, ar * si + ai * sr + zi

        sr, si = lax.fori_loop(0, nc, scan, (s0_ref[gl, :, 0:2 * s], s0_ref[gl, :, 2 * s:4 * s]))
        fin_ref[gl, :, 0:2 * s] = sr
        fin_ref[gl, :, 2 * s:4 * s] = si
        y = _dot(u, m_ref[gl]) + _dot_nt(sp_scr[...].astype(BF16), wyt_ref[gl])
        yt = y.T
        for t in range(CHUNK):
            xt_scr[t, rows, :] = yt[t * S5_GROUP:(t + 1) * S5_GROUP, :]

    for t in range(CHUNK):
        x_scr[t] = xt_scr[t].T

    def scatter(c, carry):
        row = pl.multiple_of(c * nbp, 8)
        for t in range(CHUNK):
            y_ref[pl.ds(c * CHUNK + t, nb, stride=seq), :] = x_scr[t, pl.ds(row, nb), :]
        return carry

    lax.fori_loop(0, nc, scatter, 0)


def _s5_mixer(u, s0, ops, nb, seq):
    m, wz, wyt, ar, ai = ops
    g, nc = S5_GROUPS, seq // CHUNK
    nbp = -(-nb // 8) * 8
    r = nc * nbp
    gpb = GROUPS_PER_BLOCK
    s0l = jnp.zeros((g, nbp, CW), F32)
    if s0 is not None:
        s0l = s0l.at[:, :nb].set(s0.transpose(3, 0, 2, 1, 4).reshape(g, nb, CW))
    col = pl.BlockSpec((nb * seq, gpb * S5_GROUP), lambda j: (0, j))
    wspec = pl.BlockSpec((gpb, CW, CW), lambda j: (j, 0, 0))
    cspec = pl.BlockSpec((gpb, 1, 2 * S5_STATE), lambda j: (j, 0, 0))
    sspec = pl.BlockSpec((gpb, nbp, CW), lambda j: (j, 0, 0))
    y, fin = pl.pallas_call(
        functools.partial(_s5_kernel, nb=nb, nbp=nbp, nc=nc, seq=seq),
        grid=(g // gpb,),
        in_specs=[col, wspec, wspec, wspec, cspec, cspec, sspec],
        out_specs=[col, sspec],
        out_shape=[jax.ShapeDtypeStruct((nb * seq, S5_WIDTH), F32), jax.ShapeDtypeStruct((g, nbp, CW), F32)],
        scratch_shapes=[pltpu.VMEM((CHUNK, r, gpb * S5_GROUP), F32), pltpu.VMEM((CHUNK, gpb * S5_GROUP, r), F32),
                        pltpu.VMEM((r, CW), F32), pltpu.VMEM((r, CW), F32)],
        compiler_params=_params(("parallel",)),
        name="s5_scan",
    )(u, m, wz, wyt, ar, ai, s0l)
    state = fin[:, :nb].reshape(g, nb, 2, 2, S5_STATE).transpose(1, 3, 2, 0, 4)
    return y, state


def _even_out_kernel(a_ref, y_ref, u_ref, x_ref, d_ref, gw_ref, gb_ref, wo_ref, gate_ref, o_ref):
    y = jax.nn.gelu(y_ref[...] + d_ref[...] * u_ref[...])
    y = y * jax.nn.sigmoid(_dot(y.astype(BF16), gw_ref[...]) + gb_ref[...])
    o = _dot(a_ref[...].astype(BF16), wo_ref[0:NA_WIDTH, :]) + _dot(y.astype(BF16), wo_ref[NA_WIDTH:, :])
    o_ref[...] = x_ref[...] + gate_ref[0] * o


def _even_out(a, y, u, x, d, glu_w, glu_b, w_out, mod, layer, row_fn, tm):
    t = x.shape[0]
    half = pl.BlockSpec((tm, NA_WIDTH), lambda i: (i, 0))
    full = pl.BlockSpec((tm, D_MODEL), lambda i: (i, 0))
    return pl.pallas_call(
        _even_out_kernel,
        grid=(t // tm,),
        in_specs=[half, half, half, full, _full_spec((1, S5_WIDTH)), _resident_spec(glu_w.shape),
                  _full_spec((1, S5_WIDTH)), _resident_spec(w_out.shape), _mod_spec(layer, 2, row_fn)],
        out_specs=full,
        out_shape=jax.ShapeDtypeStruct((t, D_MODEL), F32),
        compiler_params=_params(("parallel",)),
        name="even_out",
    )(a, y, u, x, d, glu_w, glu_b, w_out, mod)


def _dft_consts(seq):
    def cs(n):
        k = np.arange(n)
        ang = 2.0 * np.pi * ((k[:, None] * k[None, :]) % n) / n
        return np.cos(ang), np.sin(ang)

    cc, sc = cs(FNET_GROUP_WIDTH)
    cl, sl = cs(seq)
    chan = jnp.asarray(np.concatenate([cc, sc], axis=1), dtype=F32)
    pos = jnp.asarray(np.concatenate([cl, -sl], axis=1), dtype=F32)
    return chan.astype(BF16), pos.astype(BF16)


def _fourier_kernel(x_ref, g_ref, sh_ref, sc_ref, win_ref, chan_ref, pos_ref, wout_ref, gate_ref, o_ref, *, seq):
    x = x_ref[...]
    h = _norm_mod(x, g_ref[...], sh_ref[0], sc_ref[0]).astype(BF16)
    z = _dot(h, win_ref[...]).astype(BF16)
    w = FNET_GROUP_WIDTH
    pos = pos_ref[...]
    parts = []
    for c in range(FNET_GROUPS):
        ab = _dot(z[:, c * w:(c + 1) * w], chan_ref[...]).astype(BF16)
        stacked = jnp.concatenate([ab[:, :w], ab[:, w:]], axis=0)
        parts.append(_dot(pos, stacked))
    f = jnp.concatenate(parts, axis=1) * (1.0 / math.sqrt(seq * w))
    o_ref[...] = x + gate_ref[0] * _dot(f.astype(BF16), wout_ref[...])


def _fourier_mixer(x, g, mod, layer, row_fn, w_in, w_out, seq):
    t = x.shape[0]
    chan, pos = _dft_consts(seq)
    blk = pl.BlockSpec((seq, D_MODEL), lambda b: (b, 0))
    return pl.pallas_call(
        functools.partial(_fourier_kernel, seq=seq),
        grid=(t // seq,),
        in_specs=[blk, _full_spec((1, D_MODEL)), _mod_spec(layer, 0, row_fn), _mod_spec(layer, 1, row_fn),
                  _resident_spec(w_in.shape), _resident_spec(chan.shape), _resident_spec(pos.shape),
                  _resident_spec(w_out.shape), _mod_spec(layer, 2, row_fn)],
        out_specs=blk,
        out_shape=jax.ShapeDtypeStruct((t, D_MODEL), F32),
        compiler_params=_params(("parallel",)),
        name="fourier_mixer",
    )(x, g, mod, mod, w_in, chan, pos, w_out, mod)


def _ffn_kernel(x_ref, g_ref, sh_ref, sc_ref, gate_ref, wg_ref, wu_ref, wd_ref, fg_ref, o_ref, *, tf, final):
    x = x_ref[...]
    h = _norm_mod(x, g_ref[...], sh_ref[0], sc_ref[0]).astype(BF16)
    acc = jnp.zeros(x.shape, F32)
    for j in range(D_FF // tf):
        sl = slice(j * tf, (j + 1) * tf)
        gg = _dot(h, wg_ref[:, sl])
        uu = _dot(h, wu_ref[:, sl])
        act = (gg * jax.nn.sigmoid(gg) * uu).astype(BF16)
        acc = acc + _dot(act, wd_ref[sl, :])
    y = x + gate_ref[0] * acc
    if final:
        ms = jnp.mean(y * y, axis=-1, keepdims=True)
        y = y * lax.rsqrt(ms + EPS) * fg_ref[...]
    o_ref[...] = y


def _ffn(x, g, mod, layer, row_fn, wg, wu, wd, final_g, final, tm):
    t = x.shape[0]
    blk = pl.BlockSpec((tm, D_MODEL), lambda i: (i, 0))
    return pl.pallas_call(
        functools.partial(_ffn_kernel, tf=256, final=final),
        grid=(t // tm,),
        in_specs=[blk, _full_spec((1, D_MODEL)),
                  _mod_spec(layer, 3, row_fn), _mod_spec(layer, 4, row_fn), _mod_spec(layer, 5, row_fn),
                  _resident_spec(wg.shape), _resident_spec(wu.shape), _resident_spec(wd.shape),
                  _full_spec((1, D_MODEL))],
        out_specs=blk,
        out_shape=jax.ShapeDtypeStruct((t, D_MODEL), F32),
        compiler_params=_params(("parallel",)),
        name="ffn",
    )(x, g, mod, mod, mod, wg, wu, wd, final_g)


def kernel(x_prompt, x_sample, cache_na_k, cache_na_v, state_s5, c, c_ctx, w_mod, b_mod, norm_mix_g, norm_ffn_g, w_in_even, na_rpb, s5_lam_re, s5_lam_im, s5_log_step, s5_b_re, s5_b_im, s5_c_re, s5_c_im, s5_d, s5_glu_w, s5_glu_b, w_out_even, w_in_odd, w_out_odd, ffn_w_gate, ffn_w_up, ffn_w_down, final_norm_g):
    bp, lp, _ = x_prompt.shape
    bs, ls, _ = x_sample.shape
    past = cache_na_k.shape[2]
    depth = w_mod.shape[0]
    assert bs + 1 <= N_COND and depth == 2

    cond = jnp.zeros((COND_ROWS, D_MODEL), F32).at[0].set(c_ctx).at[1:1 + bs].set(c)
    mod = _modulation(cond.T, w_mod, b_mod).reshape(depth * COND_ROWS * 6, 1, D_MODEL)

    tm = 512
    streams = [
        dict(x=x_prompt.reshape(bp * lp, D_MODEL), nb=bp, seq=lp, tok_row=lambda i: 0, seq_row=lambda b: 0),
        dict(x=x_sample.reshape(bs * ls, D_MODEL), nb=bs, seq=ls,
             tok_row=lambda i: 1 + (i * tm) // ls, seq_row=lambda b: 1 + b),
    ]

    e = 0
    w_in = w_in_even[e].astype(BF16)
    w_out = w_out_even[e].astype(BF16)
    glu_w = s5_glu_w[e].astype(BF16)
    glu_b = s5_glu_b[e].reshape(1, S5_WIDTH)
    g_mix = norm_mix_g[0].reshape(1, D_MODEL)
    s5_ops = _s5_prep(s5_lam_re[e], s5_lam_im[e], s5_log_step[e], s5_b_re[e], s5_b_im[e], s5_c_re[e], s5_c_im[e])
    s5_dvec = s5_d[e].reshape(1, S5_WIDTH)
    bias = _na_bias(na_rpb[e], ls)

    new_k = new_v = new_s = None
    for si, st in enumerate(streams):
        nb, seq = st["nb"], st["seq"]
        q, k, v, u = _inproj(st["x"], g_mix, mod, 0, st["tok_row"], w_in, tm)
        if si == 0:
            a = _ctx_attention(q, k, v, seq)
            s0 = None
            new_k = k.reshape(nb, 1, seq, NA_HEADS, HEAD_DIM)
            new_v = v.reshape(nb, 1, seq, NA_HEADS, HEAD_DIM)
        else:
            a = _na_attention(q.reshape(nb, seq, NA_WIDTH), k.reshape(nb, seq, NA_WIDTH), v.reshape(nb, seq, NA_WIDTH),
                              cache_na_k[:, e].reshape(nb, past, NA_WIDTH), cache_na_v[:, e].reshape(nb, past, NA_WIDTH),
                              bias).reshape(nb * seq, NA_WIDTH)
            s0 = state_s5[:, e].astype(F32)
        y, state = _s5_mixer(u, s0, s5_ops, nb, seq)
        if si == 0:
            new_s = state[:, None]
        st["x"] = _even_out(a, y, u, st["x"], s5_dvec, glu_w, glu_b, w_out, mod, 0, st["tok_row"], tm)

    for layer in range(depth):
        wg = ffn_w_gate[layer].astype(BF16)
        wu = ffn_w_up[layer].astype(BF16)
        wd = ffn_w_down[layer].astype(BF16)
        g_ffn = norm_ffn_g[layer].reshape(1, D_MODEL)
        if layer == 1:
            w_in1 = w_in_odd[0].astype(BF16)
            w_out1 = w_out_odd[0].astype(BF16)
            g_mix1 = norm_mix_g[1].reshape(1, D_MODEL)
            for st in streams:
                st["x"] = _fourier_mixer(st["x"], g_mix1, mod, 1, st["seq_row"], w_in1, w_out1, st["seq"])
        for st in streams:
            st["x"] = _ffn(st["x"], g_ffn, mod, layer, st["tok_row"], wg, wu, wd,
                           final_norm_g.reshape(1, D_MODEL), layer == depth - 1, tm)

    y_prompt = streams[0]["x"].reshape(bp, lp, D_MODEL)
    y_sample = streams[1]["x"].reshape(bs, ls, D_MODEL)
    return (y_prompt, y_sample, new_k, new_v, new_s)
```

```python
import functools
import math

import numpy as np
import jax
import jax.numpy as jnp
from jax import lax
from jax.experimental import pallas as pl
from jax.experimental.pallas import tpu as pltpu

F32 = jnp.float32
BF16 = jnp.bfloat16

D_MODEL = 1024
NA_HEADS = 8
HEAD_DIM = 64
NA_WIDTH = NA_HEADS * HEAD_DIM
GRID_W = 64
WIN_R = 8
WIN_C = 16
S5_GROUP = 16
S5_GROUPS = 32
S5_STATE = 64
S5_WIDTH = S5_GROUPS * S5_GROUP
FNET_GROUPS = 4
FNET_GROUP_WIDTH = D_MODEL // FNET_GROUPS
D_FF = 2816
EPS = 1e-6

CHUNK = 16
CW = CHUNK * S5_GROUP
GROUPS_PER_BLOCK = 128 // S5_GROUP
N_COND = 3
COND_ROWS = 8
NEG = -1e30
VMEM_LIMIT = 56 * 2 ** 20

_NT = (((1,), (1,)), ((), ()))


def _params(sem, vmem=VMEM_LIMIT):
    return pltpu.CompilerParams(dimension_semantics=sem, vmem_limit_bytes=vmem)


def _dot(a, b):
    return jnp.dot(a, b, preferred_element_type=F32)


def _dot_nt(a, b):
    return lax.dot_general(a, b, _NT, preferred_element_type=F32)


def _norm_mod(x, g, shift, scale):
    ms = jnp.mean(x * x, axis=-1, keepdims=True)
    return (x * lax.rsqrt(ms + EPS) * g) * (1.0 + scale) + shift


def _mod_spec(layer, which, row_fn):
    return pl.BlockSpec((1, 1, D_MODEL),
                        lambda *idx: (layer * COND_ROWS * 6 + row_fn(*idx) * 6 + which, 0, 0))


def _full_spec(shape):
    nd = len(shape)
    return pl.BlockSpec(shape, lambda *idx: (0,) * nd)


def _resident_spec(shape):
    nd = len(shape)
    return pl.BlockSpec(shape, lambda *idx: (0,) * nd, pipeline_mode=pl.Buffered(1))


def _mod_kernel(ct_ref, w_ref, b_ref, o_ref):
    ct = ct_ref[...]
    st = ct * jax.nn.sigmoid(ct)
    w = w_ref[0]
    o_ref[0] = jnp.zeros(o_ref.shape[1:], F32)
    for r in range(N_COND):
        o_ref[0, r:r + 1, :] = jnp.sum(w * st[:, r:r + 1], axis=0, keepdims=True) + b_ref[0]


def _modulation(cond_t, w_mod, b_mod):
    depth, _, n = w_mod.shape
    tn = 1024
    return pl.pallas_call(
        _mod_kernel,
        grid=(depth, n // tn),
        in_specs=[_full_spec((D_MODEL, COND_ROWS)),
                  pl.BlockSpec((1, D_MODEL, tn), lambda l, j: (l, 0, j)),
                  pl.BlockSpec((1, 1, tn), lambda l, j: (l, 0, j))],
        out_specs=pl.BlockSpec((1, COND_ROWS, tn), lambda l, j: (l, 0, j)),
        out_shape=jax.ShapeDtypeStruct((depth, COND_ROWS, n), F32),
        compiler_params=_params(("parallel", "parallel")),
        name="modulation",
    )(cond_t, w_mod, b_mod.reshape(depth, 1, n))


def _inproj_kernel(x_ref, g_ref, sh_ref, sc_ref, w_ref, *o_refs):
    h = _norm_mod(x_ref[...], g_ref[...], sh_ref[0], sc_ref[0]).astype(BF16)
    z = _dot(h, w_ref[...])
    for i, o in enumerate(o_refs):
        o[...] = z[:, i * NA_WIDTH:(i + 1) * NA_WIDTH]


def _inproj(x, g, mod, layer, row_fn, w, tm):
    t = x.shape[0]
    flat = pl.BlockSpec((tm, NA_WIDTH), lambda i: (i, 0))
    out_specs = [flat] * 4
    out_shape = [jax.ShapeDtypeStruct((t, NA_WIDTH), F32)] * 4
    return pl.pallas_call(
        _inproj_kernel,
        grid=(t // tm,),
        in_specs=[pl.BlockSpec((tm, D_MODEL), lambda i: (i, 0)),
                  _full_spec((1, D_MODEL)),
                  _mod_spec(layer, 0, row_fn), _mod_spec(layer, 1, row_fn),
                  _resident_spec(w.shape)],
        out_specs=out_specs,
        out_shape=out_shape,
        compiler_params=_params(("parallel",)),
        name="inproj_even",
    )(x, g, mod, mod, w)


def _ctx_attn_kernel(q_ref, k_ref, v_ref, o_ref):
    scale = HEAD_DIM ** -0.5
    for h in range(NA_HEADS):
        sl = slice(h * HEAD_DIM, (h + 1) * HEAD_DIM)
        q = q_ref[:, sl].astype(BF16)
        k = k_ref[:, sl].astype(BF16)
        v = v_ref[:, sl].astype(BF16)
        s = _dot_nt(q, k) * scale
        p = jnp.exp(s - jnp.max(s, axis=-1, keepdims=True))
        den = jnp.sum(p, axis=-1, keepdims=True)
        o_ref[:, sl] = _dot(p.astype(BF16), v) / den


def _ctx_attention(q, k, v, seq):
    t = q.shape[0]
    spec = pl.BlockSpec((seq, NA_WIDTH), lambda b: (b, 0))
    return pl.pallas_call(
        _ctx_attn_kernel,
        grid=(t // seq,),
        in_specs=[spec, spec, spec],
        out_specs=spec,
        out_shape=jax.ShapeDtypeStruct((t, NA_WIDTH), F32),
        compiler_params=_params(("parallel",)),
        name="ctx_attention",
    )(q, k, v)


def _na_bias_kernel(rpb_ref, o_ref, *, rows):
    h = pl.program_id(0)
    kr_win = min(WIN_R, rows)
    qc = lax.broadcasted_iota(jnp.int32, (GRID_W, 2 * GRID_W), 0)
    lane = lax.broadcasted_iota(jnp.int32, (GRID_W, 2 * GRID_W), 1)
    kc = lane & (GRID_W - 1)
    cs = jnp.clip(qc - WIN_C // 2, 0, GRID_W - WIN_C)
    col_ok = (kc >= cs) & (kc < cs + WIN_C)
    dc = kc - qc + (WIN_C - 1)
    left = lane < GRID_W
    neg = jnp.full((GRID_W, 2 * GRID_W), NEG, F32)
    toeplitz = []
    for dr in range(2 * WIN_R - 1):
        t = neg
        for d in range(2 * WIN_C - 1):
            t = jnp.where(dc == d, rpb_ref[h, dr, d], t)
        toeplitz.append(jnp.where(col_ok, t, neg))
    for r in range(rows):
        rs = min(max(r - kr_win // 2, 0), rows - kr_win)

        def blk(kr):
            return toeplitz[kr - r + WIN_R - 1] if rs <= kr < rs + kr_win else None

        for j in range(rows // 2):
            a, b = blk(2 * j), blk(2 * j + 1)
            if a is None and b is None:
                pair = neg
            else:
                pair = jnp.where(left, neg if a is None else a, neg if b is None else b)
            o_ref[0, r * GRID_W:(r + 1) * GRID_W, j * 2 * GRID_W:(j + 1) * 2 * GRID_W] = pair


def _na_bias(rpb, seq):
    rows = seq // GRID_W
    return pl.pallas_call(
        functools.partial(_na_bias_kernel, rows=rows),
        grid=(NA_HEADS,),
        in_specs=[pl.BlockSpec(memory_space=pltpu.SMEM)],
        out_specs=pl.BlockSpec((1, seq, seq), lambda h: (h, 0, 0)),
        out_shape=jax.ShapeDtypeStruct((NA_HEADS, seq, seq), F32),
        compiler_params=_params(("parallel",)),
        name="na_bias",
    )(rpb)


def _na_key_range(qb, tq, rows):
    kr_win = min(WIN_R, rows)
    r0, r1 = qb * tq // GRID_W, ((qb + 1) * tq - 1) // GRID_W
    rs0 = min(max(r0 - kr_win // 2, 0), rows - kr_win)
    rs1 = min(max(r1 - kr_win // 2, 0), rows - kr_win)
    lo, hi = rs0 * GRID_W, (rs1 + kr_win) * GRID_W
    return lo // 128 * 128, -(-hi // 128) * 128


def _na_attn_kernel(q_ref, k_ref, v_ref, ck_ref, cv_ref, bias_ref, o_ref, *, tq):
    scale = HEAD_DIM ** -0.5
    seq = q_ref.shape[1]
    for hl in range(2):
        sl = slice(hl * HEAD_DIM, (hl + 1) * HEAD_DIM)
        ck = ck_ref[0, :, sl].astype(BF16)
        cv = cv_ref[0, :, sl].astype(BF16)
        for qb in range(seq // tq):
            lo, hi = _na_key_range(qb, tq, seq // GRID_W)
            qs = slice(qb * tq, (qb + 1) * tq)
            q = q_ref[0, qs, sl].astype(BF16)
            s_loc = _dot_nt(q, k_ref[0, lo:hi, sl].astype(BF16)) * scale + bias_ref[hl, qs, lo:hi]
            s_ctx = _dot_nt(q, ck) * scale
            m = jnp.maximum(jnp.max(s_loc, axis=-1, keepdims=True), jnp.max(s_ctx, axis=-1, keepdims=True))
            p_loc = jnp.exp(s_loc - m)
            p_ctx = jnp.exp(s_ctx - m)
            den = jnp.sum(p_loc, axis=-1, keepdims=True) + jnp.sum(p_ctx, axis=-1, keepdims=True)
            o = _dot(p_loc.astype(BF16), v_ref[0, lo:hi, sl].astype(BF16)) + _dot(p_ctx.astype(BF16), cv)
            o_ref[0, qs, sl] = o / den


def _na_attention(q, k, v, ck, cv, bias):
    nb, seq, _ = q.shape
    past = ck.shape[1]
    lat = pl.BlockSpec((1, seq, 2 * HEAD_DIM), lambda hp, b: (b, 0, hp))
    ctx = pl.BlockSpec((1, past, 2 * HEAD_DIM), lambda hp, b: (b, 0, hp))
    return pl.pallas_call(
        functools.partial(_na_attn_kernel, tq=256),
        grid=(NA_HEADS // 2, nb),
        in_specs=[lat, lat, lat, ctx, ctx, pl.BlockSpec((2, seq, seq), lambda hp, b: (hp, 0, 0))],
        out_specs=lat,
        out_shape=jax.ShapeDtypeStruct((nb, seq, NA_WIDTH), F32),
        compiler_params=_params(("parallel", "arbitrary")),
        name="na_attention",
    )(q, k, v, ck, cv, bias)


def _s5_prep_kernel(lamr_ref, lami_ref, logdt_ref, btr_ref, bti_ref, cr_ref, ci_ref,
                    m_ref, wz_ref, wyt_ref, ar_ref, ai_ref):
    lr = jnp.minimum(lamr_ref[0], -1e-4)
    li = lami_ref[0]
    dt = jnp.exp(logdt_ref[0])
    a = lr * dt
    th = li * dt
    mag = jnp.exp(a)
    lbr = mag * jnp.cos(th)
    lbi = mag * jnp.sin(th)
    den = lr * lr + li * li
    nr = lbr - 1.0
    coef_r = (nr * lr + lbi * li) / den
    coef_i = (lbi * lr - nr * li) / den
    btr, bti = btr_ref[0], bti_ref[0]
    bbr = coef_r * btr - coef_i * bti
    bbi = coef_r * bti + coef_i * btr
    cr, ci = cr_ref[0], ci_ref[0]

    step = lax.broadcasted_iota(jnp.int32, (CHUNK, 2 * S5_STATE), 0).astype(F32)
    fwd = lax.broadcasted_iota(jnp.int32, (CHUNK, 2 * S5_STATE), 1) < S5_STATE

    def powers(e):
        pm = jnp.exp(a * e)
        return pm * jnp.cos(th * e), pm * jnp.sin(th * e)

    pzr, pzi = powers(jnp.where(fwd, CHUNK - 1 - step, step))
    pyr, pyi = powers(jnp.where(fwd, step + 1.0, CHUNK - step))
    p16r, p16i = powers(jnp.full((1, 2 * S5_STATE), float(CHUNK), F32))
    ar_ref[0] = p16r
    ai_ref[0] = p16i

    def outer(p, c):
        return (p[:, None, :] * c[None, :, :]).reshape(CW, 2 * S5_STATE)

    wzr = outer(pzr, bbr) - outer(pzi, bbi)
    wzi = outer(pzr, bbi) + outer(pzi, bbr)
    wz_ref[0, :, 0:128] = wzr.astype(wz_ref.dtype)
    wz_ref[0, :, 128:256] = wzi.astype(wz_ref.dtype)
    wyt_ref[0, :, 0:128] = (outer(pyr, cr) - outer(pyi, ci)).astype(wyt_ref.dtype)
    wyt_ref[0, :, 128:256] = (-(outer(pyi, cr) + outer(pyr, ci))).astype(wyt_ref.dtype)

    ones = jnp.ones((CHUNK, 2 * S5_STATE), F32)
    cwr = outer(ones, cr)
    cwi = outer(ones, ci)
    fwd_rows = lax.broadcasted_iota(jnp.int32, (CW, 2 * S5_STATE), 1) < S5_STATE
    zero = jnp.zeros((CW, 2 * S5_STATE), F32)
    hp = lax.Precision.HIGHEST

    cw = jnp.concatenate([cwr, -cwi], axis=1)

    def kk(sel):
        lhs = jnp.concatenate([jnp.where(sel, wzr, zero), jnp.where(sel, wzi, zero)], axis=1)
        return lax.dot_general(lhs, cw, _NT, precision=hp, preferred_element_type=F32)

    kk_f = kk(fwd_rows)
    kk_b = kk(jnp.logical_not(fwd_rows))
    col_step = lax.broadcasted_iota(jnp.int32, (CW, CW), 1) // S5_GROUP
    m = jnp.zeros((CW, CW), F32)
    for t in range(CHUNK):
        up = (CHUNK - 1 - t) * S5_GROUP
        dn = t * S5_GROUP
        piece_f = kk_f if up == 0 else jnp.concatenate([kk_f[up:], jnp.zeros((up, CW), F32)], axis=0)
        piece_b = kk_b if dn == 0 else jnp.concatenate([jnp.zeros((dn, CW), F32), kk_b[:CW - dn]], axis=0)
        m = jnp.where(col_step == t, piece_f + piece_b, m)
    m_ref[0] = m.astype(m_ref.dtype)


def _s5_prep(lam_re, lam_im, log_step, b_re, b_im, c_re, c_im):
    g = S5_GROUPS
    lanes = 2 * S5_STATE

    def dirs_last(x):
        return x.transpose(1, 0, 2).reshape(g, 1, lanes)

    lamr, lami = dirs_last(lam_re), dirs_last(lam_im)
    logdt = dirs_last(jnp.broadcast_to(log_step[:, :, None], (2, g, S5_STATE)))
    bt = [x.transpose(1, 3, 0, 2).reshape(g, S5_GROUP, lanes) for x in (b_re, b_im)]
    ct = [x.transpose(1, 2, 0, 3).reshape(g, S5_GROUP, lanes) for x in (c_re, c_im)]
    vec = pl.BlockSpec((1, 1, lanes), lambda i: (i, 0, 0))
    mat = pl.BlockSpec((1, S5_GROUP, lanes), lambda i: (i, 0, 0))
    big = pl.BlockSpec((1, CW, CW), lambda i: (i, 0, 0))
    return pl.pallas_call(
        _s5_prep_kernel,
        grid=(g,),
        in_specs=[vec, vec, vec, mat, mat, mat, mat],
        out_specs=[big, big, big, vec, vec],
        out_shape=[jax.ShapeDtypeStruct((g, CW, CW), BF16)] * 3 + [jax.ShapeDtypeStruct((g, 1, lanes), F32)] * 2,
        compiler_params=_params(("parallel",)),
        name="s5_prep",
    )(lamr, lami, logdt, bt[0], bt[1], ct[0], ct[1])


def _s5_kernel(u_ref, m_ref, wz_ref, wyt_ref, ar_ref, ai_ref, s0_ref, perm_ref, permt_ref, y_ref, fin_ref,
               xt_scr, z_scr, sp_scr, *, nbp, nc):
    s = S5_STATE
    r = xt_scr.shape[2]
    for t in range(CHUNK):
        xt_scr[t] = u_ref[pl.ds(t, r, stride=CHUNK), :].T
    fwd = lax.broadcasted_iota(jnp.int32, (nbp, 2 * s), 1) < s

    for gl in range(GROUPS_PER_BLOCK):
        rows = slice(gl * S5_GROUP, (gl + 1) * S5_GROUP)
        ut = jnp.concatenate([xt_scr[t, rows, :] for t in range(CHUNK)], axis=0)
        u = ut.T.astype(BF16)
        z_scr[...] = _dot(_dot(perm_ref[...], u).astype(BF16), wz_ref[gl])
        ar = jnp.broadcast_to(ar_ref[gl], (nbp, 2 * s))
        ai = jnp.broadcast_to(ai_ref[gl], (nbp, 2 * s))

        sr, si = s0_ref[gl, :, 0:2 * s], s0_ref[gl, :, 2 * s:4 * s]
        for k in range(nc):
            up = slice(k * nbp, (k + 1) * nbp)
            dn = slice((nc - 1 - k) * nbp, (nc - k) * nbp)
            sp_scr[up, 0:s] = sr[:, 0:s]
            sp_scr[dn, s:2 * s] = sr[:, s:2 * s]
            sp_scr[up, 2 * s:3 * s] = si[:, 0:s]
            sp_scr[dn, 3 * s:4 * s] = si[:, s:2 * s]
            zr = jnp.where(fwd, z_scr[up, 0:2 * s], z_scr[dn, 0:2 * s])
            zi = jnp.where(fwd, z_scr[up, 2 * s:4 * s], z_scr[dn, 2 * s:4 * s])
            sr, si = ar * sr - ai * si + zr, ar * si + ai * sr + zi
        fin_ref[gl, :, 0:2 * s] = sr
        fin_ref[gl, :, 2 * s:4 * s] = si
        sp = _dot(permt_ref[...], sp_scr[...].astype(BF16)).astype(BF16)
        yt = (_dot(u, m_ref[gl]) + _dot_nt(sp, wyt_ref[gl])).T
        for t in range(CHUNK):
            xt_scr[t, rows, :] = yt[t * S5_GROUP:(t + 1) * S5_GROUP, :]

    for t in range(CHUNK):
        y_ref[pl.ds(t, r, stride=CHUNK), :] = xt_scr[t].T


def _s5_mixer(u, s0, ops, nb, seq):
    m, wz, wyt, ar, ai = ops
    g, nc = S5_GROUPS, seq // CHUNK
    nbp = -(-nb // 8) * 8
    r, rp = nb * nc, nc * nbp
    gpb = GROUPS_PER_BLOCK
    s0l = jnp.zeros((g, nbp, CW), F32)
    if s0 is not None:
        s0l = s0l.at[:, :nb].set(s0.transpose(3, 0, 2, 1, 4).reshape(g, nb, CW))
    perm = np.zeros((nc, nbp, nb, nc), np.float32)
    for b in range(nb):
        perm[np.arange(nc), b, b, np.arange(nc)] = 1.0
    perm = jnp.asarray(perm.reshape(rp, r)).astype(BF16)
    col = pl.BlockSpec((nb * seq, gpb * S5_GROUP), lambda j: (0, j))
    wspec = pl.BlockSpec((gpb, CW, CW), lambda j: (j, 0, 0))
    cspec = pl.BlockSpec((gpb, 1, 2 * S5_STATE), lambda j: (j, 0, 0))
    sspec = pl.BlockSpec((gpb, nbp, CW), lambda j: (j, 0, 0))
    y, fin = pl.pallas_call(
        functools.partial(_s5_kernel, nbp=nbp, nc=nc),
        grid=(g // gpb,),
        in_specs=[col, wspec, wspec, wspec, cspec, cspec, sspec, _full_spec((rp, r)), _full_spec((r, rp))],
        out_specs=[col, sspec],
        out_shape=[jax.ShapeDtypeStruct((nb * seq, S5_WIDTH), F32), jax.ShapeDtypeStruct((g, nbp, CW), F32)],
        scratch_shapes=[pltpu.VMEM((CHUNK, gpb * S5_GROUP, r), F32),
                        pltpu.VMEM((rp, CW), F32), pltpu.VMEM((rp, CW), F32)],
        compiler_params=_params(("parallel",)),
        name="s5_scan",
    )(u, m, wz, wyt, ar, ai, s0l, perm, perm.T)
    state = fin[:, :nb].reshape(g, nb, 2, 2, S5_STATE).transpose(1, 3, 2, 0, 4)
    return y, state


def _even_out_kernel(a_ref, y_ref, u_ref, x_ref, d_ref, gw_ref, gb_ref, wo_ref, gate_ref, o_ref):
    y = jax.nn.gelu(y_ref[...] + d_ref[...] * u_ref[...])
    y = y * jax.nn.sigmoid(_dot(y.astype(BF16), gw_ref[...]) + gb_ref[...])
    o = _dot(a_ref[...].astype(BF16), wo_ref[0:NA_WIDTH, :]) + _dot(y.astype(BF16), wo_ref[NA_WIDTH:, :])
    o_ref[...] = x_ref[...] + gate_ref[0] * o


def _even_out(a, y, u, x, d, glu_w, glu_b, w_out, mod, layer, row_fn, tm):
    t = x.shape[0]
    half = pl.BlockSpec((tm, NA_WIDTH), lambda i: (i, 0))
    full = pl.BlockSpec((tm, D_MODEL), lambda i: (i, 0))
    return pl.pallas_call(
        _even_out_kernel,
        grid=(t // tm,),
        in_specs=[half, half, half, full, _full_spec((1, S5_WIDTH)), _resident_spec(glu_w.shape),
                  _full_spec((1, S5_WIDTH)), _resident_spec(w_out.shape), _mod_spec(layer, 2, row_fn)],
        out_specs=full,
        out_shape=jax.ShapeDtypeStruct((t, D_MODEL), F32),
        compiler_params=_params(("parallel",)),
        name="even_out",
    )(a, y, u, x, d, glu_w, glu_b, w_out, mod)


def _dft_consts(seq):
    def cs(n):
        k = np.arange(n)
        ang = 2.0 * np.pi * ((k[:, None] * k[None, :]) % n) / n
        return np.cos(ang), np.sin(ang)

    cc, sc = cs(FNET_GROUP_WIDTH)
    cl, sl = cs(seq)
    chan = jnp.asarray(np.concatenate([cc, sc], axis=1), dtype=F32)
    pos = jnp.asarray(np.concatenate([cl, -sl], axis=1), dtype=F32)
    return chan.astype(BF16), pos.astype(BF16)


def _fourier_kernel(x_ref, g_ref, sh_ref, sc_ref, win_ref, chan_ref, pos_ref, wout_ref, gate_ref, o_ref, *, seq):
    x = x_ref[...]
    h = _norm_mod(x, g_ref[...], sh_ref[0], sc_ref[0]).astype(BF16)
    z = _dot(h, win_ref[...]).astype(BF16)
    w = FNET_GROUP_WIDTH
    pos = pos_ref[...]
    parts = []
    for c in range(FNET_GROUPS):
        ab = _dot(z[:, c * w:(c + 1) * w], chan_ref[...]).astype(BF16)
        stacked = jnp.concatenate([ab[:, :w], ab[:, w:]], axis=0)
        parts.append(_dot(pos, stacked))
    f = jnp.concatenate(parts, axis=1) * (1.0 / math.sqrt(seq * w))
    o_ref[...] = x + gate_ref[0] * _dot(f.astype(BF16), wout_ref[...])


def _fourier_mixer(x, g, mod, layer, row_fn, w_in, w_out, seq):
    t = x.shape[0]
    chan, pos = _dft_consts(seq)
    blk = pl.BlockSpec((seq, D_MODEL), lambda b: (b, 0))
    return pl.pallas_call(
        functools.partial(_fourier_kernel, seq=seq),
        grid=(t // seq,),
        in_specs=[blk, _full_spec((1, D_MODEL)), _mod_spec(layer, 0, row_fn), _mod_spec(layer, 1, row_fn),
                  _resident_spec(w_in.shape), _resident_spec(chan.shape), _resident_spec(pos.shape),
                  _resident_spec(w_out.shape), _mod_spec(layer, 2, row_fn)],
        out_specs=blk,
        out_shape=jax.ShapeDtypeStruct((t, D_MODEL), F32),
        compiler_params=_params(("parallel",)),
        name="fourier_mixer",
    )(x, g, mod, mod, w_in, chan, pos, w_out, mod)


def _ffn_kernel(x_ref, g_ref, sh_ref, sc_ref, gate_ref, wg_ref, wu_ref, wd_ref, fg_ref, o_ref, *, tf, final):
    x = x_ref[...]
    h = _norm_mod(x, g_ref[...], sh_ref[0], sc_ref[0]).astype(BF16)
    acc = jnp.zeros(x.shape, F32)
    for j in range(D_FF // tf):
        sl = slice(j * tf, (j + 1) * tf)
        gg = _dot(h, wg_ref[:, sl])
        uu = _dot(h, wu_ref[:, sl])
        act = (gg * jax.nn.sigmoid(gg) * uu).astype(BF16)
        acc = acc + _dot(act, wd_ref[sl, :])
    y = x + gate_ref[0] * acc
    if final:
        ms = jnp.mean(y * y, axis=-1, keepdims=True)
        y = y * lax.rsqrt(ms + EPS) * fg_ref[...]
    o_ref[...] = y


def _ffn(x, g, mod, layer, row_fn, wg, wu, wd, final_g, final, tm):
    t = x.shape[0]
    blk = pl.BlockSpec((tm, D_MODEL), lambda i: (i, 0))
    return pl.pallas_call(
        functools.partial(_ffn_kernel, tf=256, final=final),
        grid=(t // tm,),
        in_specs=[blk, _full_spec((1, D_MODEL)),
                  _mod_spec(layer, 3, row_fn), _mod_spec(layer, 4, row_fn), _mod_spec(layer, 5, row_fn),
                  _resident_spec(wg.shape), _resident_spec(wu.shape), _resident_spec(wd.shape),
                  _full_spec((1, D_MODEL))],
        out_specs=blk,
        out_shape=jax.ShapeDtypeStruct((t, D_MODEL), F32),
        compiler_params=_params(("parallel",)),
        name="ffn",
    )(x, g, mod, mod, mod, wg, wu, wd, final_g)


def kernel(x_prompt, x_sample, cache_na_k, cache_na_v, state_s5, c, c_ctx, w_mod, b_mod, norm_mix_g, norm_ffn_g, w_in_even, na_rpb, s5_lam_re, s5_lam_im, s5_log_step, s5_b_re, s5_b_im, s5_c_re, s5_c_im, s5_d, s5_glu_w, s5_glu_b, w_out_even, w_in_odd, w_out_odd, ffn_w_gate, ffn_w_up, ffn_w_down, final_norm_g):
    bp, lp, _ = x_prompt.shape
    bs, ls, _ = x_sample.shape
    past = cache_na_k.shape[2]
    depth = w_mod.shape[0]
    assert bs + 1 <= N_COND and depth == 2

    cond = jnp.zeros((COND_ROWS, D_MODEL), F32).at[0].set(c_ctx).at[1:1 + bs].set(c)
    mod = _modulation(cond.T, w_mod, b_mod).reshape(depth * COND_ROWS * 6, 1, D_MODEL)

    tm = 512
    streams = [
        dict(x=x_prompt.reshape(bp * lp, D_MODEL), nb=bp, seq=lp, tok_row=lambda i: 0, seq_row=lambda b: 0),
        dict(x=x_sample.reshape(bs * ls, D_MODEL), nb=bs, seq=ls,
             tok_row=lambda i: 1 + (i * tm) // ls, seq_row=lambda b: 1 + b),
    ]

    e = 0
    w_in = w_in_even[e].astype(BF16)
    w_out = w_out_even[e].astype(BF16)
    glu_w = s5_glu_w[e].astype(BF16)
    glu_b = s5_glu_b[e].reshape(1, S5_WIDTH)
    g_mix = norm_mix_g[0].reshape(1, D_MODEL)
    s5_ops = _s5_prep(s5_lam_re[e], s5_lam_im[e], s5_log_step[e], s5_b_re[e], s5_b_im[e], s5_c_re[e], s5_c_im[e])
    s5_dvec = s5_d[e].reshape(1, S5_WIDTH)
    bias = _na_bias(na_rpb[e], ls)

    new_k = new_v = new_s = None
    for si, st in enumerate(streams):
        nb, seq = st["nb"], st["seq"]
        q, k, v, u = _inproj(st["x"], g_mix, mod, 0, st["tok_row"], w_in, tm)
        if si == 0:
            a = _ctx_attention(q, k, v, seq)
            s0 = None
            new_k = k.reshape(nb, 1, seq, NA_HEADS, HEAD_DIM)
            new_v = v.reshape(nb, 1, seq, NA_HEADS, HEAD_DIM)
        else:
            a = _na_attention(q.reshape(nb, seq, NA_WIDTH), k.reshape(nb, seq, NA_WIDTH), v.reshape(nb, seq, NA_WIDTH),
                              cache_na_k[:, e].reshape(nb, past, NA_WIDTH), cache_na_v[:, e].reshape(nb, past, NA_WIDTH),
                              bias).reshape(nb * seq, NA_WIDTH)
            s0 = state_s5[:, e].astype(F32)
        y, state = _s5_mixer(u, s0, s5_ops, nb, seq)
        if si == 0:
            new_s = state[:, None]
        st["x"] = _even_out(a, y, u, st["x"], s5_dvec, glu_w, glu_b, w_out, mod, 0, st["tok_row"], tm)

    for layer in range(depth):
        wg = ffn_w_gate[layer].astype(BF16)
        wu = ffn_w_up[layer].astype(BF16)
        wd = ffn_w_down[layer].astype(BF16)
        g_ffn = norm_ffn_g[layer].reshape(1, D_MODEL)
        if layer == 1:
            w_in1 = w_in_odd[0].astype(BF16)
            w_out1 = w_out_odd[0].astype(BF16)
            g_mix1 = norm_mix_g[1].reshape(1, D_MODEL)
            for st in streams:
                st["x"] = _fourier_mixer(st["x"], g_mix1, mod, 1, st["seq_row"], w_in1, w_out1, st["seq"])
        for st in streams:
            st["x"] = _ffn(st["x"], g_ffn, mod, layer, st["tok_row"], wg, wu, wd,
                           final_norm_g.reshape(1, D_MODEL), layer == depth - 1, tm)

    y_prompt = streams[0]["x"].reshape(bp, lp, D_MODEL)
    y_sample = streams[1]["x"].reshape(bs, ls, D_MODEL)
    return (y_prompt, y_sample, new_k, new_v, new_s)
```

```python
import functools
import math

import numpy as np
import jax
import jax.numpy as jnp
from jax import lax
from jax.experimental import pallas as pl
from jax.experimental.pallas import tpu as pltpu

F32 = jnp.float32
BF16 = jnp.bfloat16

D_MODEL = 1024
NA_HEADS = 8
HEAD_DIM = 64
NA_WIDTH = NA_HEADS * HEAD_DIM
GRID_W = 64
WIN_R = 8
WIN_C = 16
S5_GROUP = 16
S5_GROUPS = 32
S5_STATE = 64
S5_WIDTH = S5_GROUPS * S5_GROUP
FNET_GROUPS = 4
FNET_GROUP_WIDTH = D_MODEL // FNET_GROUPS
D_FF = 2816
EPS = 1e-6

CHUNK = 16
CW = CHUNK * S5_GROUP
GROUPS_PER_BLOCK = 128 // S5_GROUP
N_COND = 3
COND_ROWS = 8
NEG = -1e30
VMEM_LIMIT = 56 * 2 ** 20

_NT = (((1,), (1,)), ((), ()))


def _params(sem, vmem=VMEM_LIMIT):
    return pltpu.CompilerParams(dimension_semantics=sem, vmem_limit_bytes=vmem)


def _dot(a, b):
    return jnp.dot(a, b, preferred_element_type=F32)


def _dot_nt(a, b):
    return lax.dot_general(a, b, _NT, preferred_element_type=F32)


def _norm_mod(x, g, shift, scale):
    ms = jnp.mean(x * x, axis=-1, keepdims=True)
    return (x * lax.rsqrt(ms + EPS) * g) * (1.0 + scale) + shift


def _mod_spec(layer, which, row_fn):
    return pl.BlockSpec((1, 1, D_MODEL),
                        lambda *idx: (layer * COND_ROWS * 6 + row_fn(*idx) * 6 + which, 0, 0))


def _full_spec(shape):
    nd = len(shape)
    return pl.BlockSpec(shape, lambda *idx: (0,) * nd)


def _resident_spec(shape):
    nd = len(shape)
    return pl.BlockSpec(shape, lambda *idx: (0,) * nd, pipeline_mode=pl.Buffered(1))


def _mod_kernel(ct_ref, w_ref, b_ref, o_ref):
    ct = ct_ref[...]
    st = ct * jax.nn.sigmoid(ct)
    w = w_ref[0]
    o_ref[0] = jnp.zeros(o_ref.shape[1:], F32)
    for r in range(N_COND):
        o_ref[0, r:r + 1, :] = jnp.sum(w * st[:, r:r + 1], axis=0, keepdims=True) + b_ref[0]


def _modulation(cond_t, w_mod, b_mod):
    depth, _, n = w_mod.shape
    tn = 1024
    return pl.pallas_call(
        _mod_kernel,
        grid=(depth, n // tn),
        in_specs=[_full_spec((D_MODEL, COND_ROWS)),
                  pl.BlockSpec((1, D_MODEL, tn), lambda l, j: (l, 0, j)),
                  pl.BlockSpec((1, 1, tn), lambda l, j: (l, 0, j))],
        out_specs=pl.BlockSpec((1, COND_ROWS, tn), lambda l, j: (l, 0, j)),
        out_shape=jax.ShapeDtypeStruct((depth, COND_ROWS, n), F32),
        compiler_params=_params(("parallel", "parallel")),
        name="modulation",
    )(cond_t, w_mod, b_mod.reshape(depth, 1, n))


def _inproj_kernel(x_ref, g_ref, sh_ref, sc_ref, w_ref, *o_refs):
    h = _norm_mod(x_ref[...], g_ref[...], sh_ref[0], sc_ref[0]).astype(BF16)
    z = _dot(h, w_ref[...])
    for i, o in enumerate(o_refs):
        o[...] = z[:, i * NA_WIDTH:(i + 1) * NA_WIDTH]


def _inproj(x, g, mod, layer, row_fn, w, tm):
    t = x.shape[0]
    flat = pl.BlockSpec((tm, NA_WIDTH), lambda i: (i, 0))
    out_specs = [flat] * 4
    out_shape = [jax.ShapeDtypeStruct((t, NA_WIDTH), F32)] * 4
    return pl.pallas_call(
        _inproj_kernel,
        grid=(t // tm,),
        in_specs=[pl.BlockSpec((tm, D_MODEL), lambda i: (i, 0)),
                  _full_spec((1, D_MODEL)),
                  _mod_spec(layer, 0, row_fn), _mod_spec(layer, 1, row_fn),
                  _resident_spec(w.shape)],
        out_specs=out_specs,
        out_shape=out_shape,
        compiler_params=_params(("parallel",)),
        name="inproj_even",
    )(x, g, mod, mod, w)


def _ctx_attn_kernel(q_ref, k_ref, v_ref, o_ref):
    scale = HEAD_DIM ** -0.5
    qt = (q_ref[...] * scale).T.astype(BF16)
    vt = v_ref[...].T.astype(BF16)
    heads = [slice(h * HEAD_DIM, (h + 1) * HEAD_DIM) for h in range(NA_HEADS)]
    sts = [_dot(k_ref[:, sl].astype(BF16), qt[sl, :]) for sl in heads]
    pts = [jnp.exp(st - jnp.max(st, axis=0, keepdims=True)) for st in sts]
    dens = [jnp.sum(pt, axis=0, keepdims=True) for pt in pts]
    outs = [_dot(vt[sl, :], pt.astype(BF16)) / den for sl, pt, den in zip(heads, pts, dens)]
    o_ref[...] = jnp.concatenate(outs, axis=0).T


def _ctx_attention(q, k, v, seq):
    t = q.shape[0]
    spec = pl.BlockSpec((seq, NA_WIDTH), lambda b: (b, 0))
    return pl.pallas_call(
        _ctx_attn_kernel,
        grid=(t // seq,),
        in_specs=[spec, spec, spec],
        out_specs=spec,
        out_shape=jax.ShapeDtypeStruct((t, NA_WIDTH), F32),
        compiler_params=_params(("parallel",)),
        name="ctx_attention",
    )(q, k, v)


def _na_bias_kernel(rpb_ref, o_ref, *, rows):
    h = pl.program_id(0)
    kr_win = min(WIN_R, rows)
    qc = lax.broadcasted_iota(jnp.int32, (GRID_W, 2 * GRID_W), 0)
    lane = lax.broadcasted_iota(jnp.int32, (GRID_W, 2 * GRID_W), 1)
    kc = lane & (GRID_W - 1)
    cs = jnp.clip(qc - WIN_C // 2, 0, GRID_W - WIN_C)
    col_ok = (kc >= cs) & (kc < cs + WIN_C)
    dc = kc - qc + (WIN_C - 1)
    left = lane < GRID_W
    neg = jnp.full((GRID_W, 2 * GRID_W), NEG, F32)
    toeplitz = []
    for dr in range(2 * WIN_R - 1):
        t = neg
        for d in range(2 * WIN_C - 1):
            t = jnp.where(dc == d, rpb_ref[h, dr, d], t)
        toeplitz.append(jnp.where(col_ok, t, neg))
    for r in range(rows):
        rs = min(max(r - kr_win // 2, 0), rows - kr_win)

        def blk(kr):
            return toeplitz[kr - r + WIN_R - 1] if rs <= kr < rs + kr_win else None

        for j in range(rows // 2):
            a, b = blk(2 * j), blk(2 * j + 1)
            if a is None and b is None:
                pair = neg
            else:
                pair = jnp.where(left, neg if a is None else a, neg if b is None else b)
            o_ref[0, r * GRID_W:(r + 1) * GRID_W, j * 2 * GRID_W:(j + 1) * 2 * GRID_W] = pair


def _na_bias(rpb, seq):
    rows = seq // GRID_W
    return pl.pallas_call(
        functools.partial(_na_bias_kernel, rows=rows),
        grid=(NA_HEADS,),
        in_specs=[pl.BlockSpec(memory_space=pltpu.SMEM)],
        out_specs=pl.BlockSpec((1, seq, seq), lambda h: (h, 0, 0)),
        out_shape=jax.ShapeDtypeStruct((NA_HEADS, seq, seq), F32),
        compiler_params=_params(("parallel",)),
        name="na_bias",
    )(rpb)


def _na_key_range(qb, tq, rows):
    kr_win = min(WIN_R, rows)
    r0, r1 = qb * tq // GRID_W, ((qb + 1) * tq - 1) // GRID_W
    rs0 = min(max(r0 - kr_win // 2, 0), rows - kr_win)
    rs1 = min(max(r1 - kr_win // 2, 0), rows - kr_win)
    lo, hi = rs0 * GRID_W, (rs1 + kr_win) * GRID_W
    return lo // 128 * 128, -(-hi // 128) * 128


def _na_attn_kernel(q_ref, k_ref, v_ref, ck_ref, cv_ref, bias_ref, o_ref, *, tq):
    scale = HEAD_DIM ** -0.5
    seq = q_ref.shape[1]
    probs = []
    for hl in range(2):
        sl = slice(hl * HEAD_DIM, (hl + 1) * HEAD_DIM)
        for qb in range(seq // tq):
            lo, hi = _na_key_range(qb, tq, seq // GRID_W)
            probs.append((hl, sl, slice(qb * tq, (qb + 1) * tq), slice(lo, hi)))
    qs_all = [(q_ref[0, qs, sl] * scale).astype(BF16) for _, sl, qs, _ in probs]
    s_loc = [_dot_nt(q, k_ref[0, ks, sl].astype(BF16)) + bias_ref[hl, qs, ks]
             for q, (hl, sl, qs, ks) in zip(qs_all, probs)]
    s_ctx = [_dot_nt(q, ck_ref[0, :, sl].astype(BF16)) for q, (_, sl, _, _) in zip(qs_all, probs)]
    ms = [jnp.maximum(jnp.max(a, axis=-1, keepdims=True), jnp.max(b, axis=-1, keepdims=True))
          for a, b in zip(s_loc, s_ctx)]
    p_loc = [jnp.exp(a - m) for a, m in zip(s_loc, ms)]
    p_ctx = [jnp.exp(b - m) for b, m in zip(s_ctx, ms)]
    dens = [jnp.sum(a, axis=-1, keepdims=True) + jnp.sum(b, axis=-1, keepdims=True) for a, b in zip(p_loc, p_ctx)]
    for a, b, den, (_, sl, qs, ks) in zip(p_loc, p_ctx, dens, probs):
        o = (_dot(a.astype(BF16), v_ref[0, ks, sl].astype(BF16))
             + _dot(b.astype(BF16), cv_ref[0, :, sl].astype(BF16)))
        o_ref[0, qs, sl] = o / den


def _na_attention(q, k, v, ck, cv, bias):
    nb, seq, _ = q.shape
    past = ck.shape[1]
    lat = pl.BlockSpec((1, seq, 2 * HEAD_DIM), lambda hp, b: (b, 0, hp))
    ctx = pl.BlockSpec((1, past, 2 * HEAD_DIM), lambda hp, b: (b, 0, hp))
    return pl.pallas_call(
        functools.partial(_na_attn_kernel, tq=256),
        grid=(NA_HEADS // 2, nb),
        in_specs=[lat, lat, lat, ctx, ctx, pl.BlockSpec((2, seq, seq), lambda hp, b: (hp, 0, 0))],
        out_specs=lat,
        out_shape=jax.ShapeDtypeStruct((nb, seq, NA_WIDTH), F32),
        compiler_params=_params(("parallel", "arbitrary")),
        name="na_attention",
    )(q, k, v, ck, cv, bias)


def _s5_prep_group(i, lamr_ref, lami_ref, logdt_ref, btr_ref, bti_ref, cr_ref, ci_ref,
                   m_ref, wz_ref, wyt_ref, ar_ref, ai_ref):
    lr = jnp.minimum(lamr_ref[i], -1e-4)
    li = lami_ref[i]
    dt = jnp.exp(logdt_ref[i])
    a = lr * dt
    th = li * dt
    mag = jnp.exp(a)
    lbr = mag * jnp.cos(th)
    lbi = mag * jnp.sin(th)
    den = lr * lr + li * li
    nr = lbr - 1.0
    coef_r = (nr * lr + lbi * li) / den
    coef_i = (lbi * lr - nr * li) / den
    btr, bti = btr_ref[i], bti_ref[i]
    bbr = coef_r * btr - coef_i * bti
    bbi = coef_r * bti + coef_i * btr
    cr, ci = cr_ref[i], ci_ref[i]

    step = lax.broadcasted_iota(jnp.int32, (CHUNK, 2 * S5_STATE), 0).astype(F32)
    fwd = lax.broadcasted_iota(jnp.int32, (CHUNK, 2 * S5_STATE), 1) < S5_STATE

    def powers(e):
        pm = jnp.exp(a * e)
        return pm * jnp.cos(th * e), pm * jnp.sin(th * e)

    pzr, pzi = powers(jnp.where(fwd, CHUNK - 1 - step, step))
    pyr, pyi = powers(jnp.where(fwd, step + 1.0, CHUNK - step))
    p16r, p16i = powers(jnp.full((1, 2 * S5_STATE), float(CHUNK), F32))
    ar_ref[i] = p16r
    ai_ref[i] = p16i

    def outer(p, c):
        return (p[:, None, :] * c[None, :, :]).reshape(CW, 2 * S5_STATE)

    wzr = outer(pzr, bbr) - outer(pzi, bbi)
    wzi = outer(pzr, bbi) + outer(pzi, bbr)
    wz_ref[i, :, 0:128] = wzr.astype(wz_ref.dtype)
    wz_ref[i, :, 128:256] = wzi.astype(wz_ref.dtype)
    wyt_ref[i, :, 0:128] = (outer(pyr, cr) - outer(pyi, ci)).astype(wyt_ref.dtype)
    wyt_ref[i, :, 128:256] = (-(outer(pyi, cr) + outer(pyr, ci))).astype(wyt_ref.dtype)

    ones = jnp.ones((CHUNK, 2 * S5_STATE), F32)
    cwr = outer(ones, cr)
    cwi = outer(ones, ci)
    fwd_rows = lax.broadcasted_iota(jnp.int32, (CW, 2 * S5_STATE), 1) < S5_STATE
    zero = jnp.zeros((CW, 2 * S5_STATE), F32)
    hp = lax.Precision.HIGHEST

    cw = jnp.concatenate([cwr, -cwi], axis=1)

    def kk(sel):
        lhs = jnp.concatenate([jnp.where(sel, wzr, zero), jnp.where(sel, wzi, zero)], axis=1)
        return lax.dot_general(lhs, cw, _NT, precision=hp, preferred_element_type=F32)

    kk_f = kk(fwd_rows)
    kk_b = kk(jnp.logical_not(fwd_rows))
    col_step = lax.broadcasted_iota(jnp.int32, (CW, CW), 1) // S5_GROUP
    m = jnp.zeros((CW, CW), F32)
    for t in range(CHUNK):
        up = (CHUNK - 1 - t) * S5_GROUP
        dn = t * S5_GROUP
        piece_f = kk_f if up == 0 else jnp.concatenate([kk_f[up:], jnp.zeros((up, CW), F32)], axis=0)
        piece_b = kk_b if dn == 0 else jnp.concatenate([jnp.zeros((dn, CW), F32), kk_b[:CW - dn]], axis=0)
        m = jnp.where(col_step == t, piece_f + piece_b, m)
    m_ref[i] = m.astype(m_ref.dtype)


def _s5_prep_kernel(*refs):
    for i in range(refs[0].shape[0]):
        _s5_prep_group(i, *refs)


def _s5_prep(lam_re, lam_im, log_step, b_re, b_im, c_re, c_im):
    g = S5_GROUPS
    lanes = 2 * S5_STATE

    def dirs_last(x):
        return x.transpose(1, 0, 2).reshape(g, 1, lanes)

    lamr, lami = dirs_last(lam_re), dirs_last(lam_im)
    logdt = dirs_last(jnp.broadcast_to(log_step[:, :, None], (2, g, S5_STATE)))
    bt = [x.transpose(1, 3, 0, 2).reshape(g, S5_GROUP, lanes) for x in (b_re, b_im)]
    ct = [x.transpose(1, 2, 0, 3).reshape(g, S5_GROUP, lanes) for x in (c_re, c_im)]
    gp = 4
    vec = pl.BlockSpec((gp, 1, lanes), lambda i: (i, 0, 0))
    mat = pl.BlockSpec((gp, S5_GROUP, lanes), lambda i: (i, 0, 0))
    big = pl.BlockSpec((gp, CW, CW), lambda i: (i, 0, 0))
    return pl.pallas_call(
        _s5_prep_kernel,
        grid=(g // gp,),
        in_specs=[vec, vec, vec, mat, mat, mat, mat],
        out_specs=[big, big, big, vec, vec],
        out_shape=[jax.ShapeDtypeStruct((g, CW, CW), BF16)] * 3 + [jax.ShapeDtypeStruct((g, 1, lanes), F32)] * 2,
        compiler_params=_params(("parallel",)),
        name="s5_prep",
    )(lamr, lami, logdt, bt[0], bt[1], ct[0], ct[1])


def _s5_kernel(u_ref, m_ref, wz_ref, wyt_ref, ar_ref, ai_ref, s0_ref, perm_ref, permt_ref, y_ref, fin_ref,
               xt_scr, z_scr, sp_scr, *, nbp, nc):
    s = S5_STATE
    r = xt_scr.shape[2]
    for t in range(CHUNK):
        xt_scr[t] = u_ref[pl.ds(t, r, stride=CHUNK), :].T
    fwd = lax.broadcasted_iota(jnp.int32, (nbp, 2 * s), 1) < s
    groups = range(GROUPS_PER_BLOCK)
    chans = [slice(gl * S5_GROUP, (gl + 1) * S5_GROUP) for gl in groups]

    us = [jnp.concatenate([xt_scr[t, ch, :] for t in range(CHUNK)], axis=0).T.astype(BF16) for ch in chans]
    for gl in groups:
        z_scr[gl] = _dot(_dot(perm_ref[...], us[gl]).astype(BF16), wz_ref[gl])

    ars = [jnp.broadcast_to(ar_ref[gl], (nbp, 2 * s)) for gl in groups]
    ais = [jnp.broadcast_to(ai_ref[gl], (nbp, 2 * s)) for gl in groups]
    srs = [s0_ref[gl, :, 0:2 * s] for gl in groups]
    sis = [s0_ref[gl, :, 2 * s:4 * s] for gl in groups]
    for k in range(nc):
        up = slice(k * nbp, (k + 1) * nbp)
        dn = slice((nc - 1 - k) * nbp, (nc - k) * nbp)
        for gl in groups:
            sr, si = srs[gl], sis[gl]
            sp_scr[gl, up, 0:s] = sr[:, 0:s]
            sp_scr[gl, dn, s:2 * s] = sr[:, s:2 * s]
            sp_scr[gl, up, 2 * s:3 * s] = si[:, 0:s]
            sp_scr[gl, dn, 3 * s:4 * s] = si[:, s:2 * s]
            zr = jnp.where(fwd, z_scr[gl, up, 0:2 * s], z_scr[gl, dn, 0:2 * s])
            zi = jnp.where(fwd, z_scr[gl, up, 2 * s:4 * s], z_scr[gl, dn, 2 * s:4 * s])
            srs[gl] = ars[gl] * sr - ais[gl] * si + zr
            sis[gl] = ars[gl] * si + ais[gl] * sr + zi
    for gl in groups:
        fin_ref[gl, :, 0:2 * s] = srs[gl]
        fin_ref[gl, :, 2 * s:4 * s] = sis[gl]
    sps = [_dot(permt_ref[...], sp_scr[gl].astype(BF16)).astype(BF16) for gl in groups]
    for gl in groups:
        yt = (_dot(us[gl], m_ref[gl]) + _dot_nt(sps[gl], wyt_ref[gl])).T
        for t in range(CHUNK):
            xt_scr[t, chans[gl], :] = yt[t * S5_GROUP:(t + 1) * S5_GROUP, :]

    for t in range(CHUNK):
        y_ref[pl.ds(t, r, stride=CHUNK), :] = xt_scr[t].T


def _s5_mixer(u, s0, ops, nb, seq):
    m, wz, wyt, ar, ai = ops
    g, nc = S5_GROUPS, seq // CHUNK
    nbp = -(-nb // 8) * 8
    r, rp = nb * nc, nc * nbp
    gpb = GROUPS_PER_BLOCK
    s0l = jnp.zeros((g, nbp, CW), F32)
    if s0 is not None:
        s0l = s0l.at[:, :nb].set(s0.transpose(3, 0, 2, 1, 4).reshape(g, nb, CW))
    perm = np.zeros((nc, nbp, nb, nc), np.float32)
    for b in range(nb):
        perm[np.arange(nc), b, b, np.arange(nc)] = 1.0
    perm = jnp.asarray(perm.reshape(rp, r)).astype(BF16)
    col = pl.BlockSpec((nb * seq, gpb * S5_GROUP), lambda j: (0, j))
    wspec = pl.BlockSpec((gpb, CW, CW), lambda j: (j, 0, 0))
    cspec = pl.BlockSpec((gpb, 1, 2 * S5_STATE), lambda j: (j, 0, 0))
    sspec = pl.BlockSpec((gpb, nbp, CW), lambda j: (j, 0, 0))
    y, fin = pl.pallas_call(
        functools.partial(_s5_kernel, nbp=nbp, nc=nc),
        grid=(g // gpb,),
        in_specs=[col, wspec, wspec, wspec, cspec, cspec, sspec, _full_spec((rp, r)), _full_spec((r, rp))],
        out_specs=[col, sspec],
        out_shape=[jax.ShapeDtypeStruct((nb * seq, S5_WIDTH), F32), jax.ShapeDtypeStruct((g, nbp, CW), F32)],
        scratch_shapes=[pltpu.VMEM((CHUNK, gpb * S5_GROUP, r), F32),
                        pltpu.VMEM((gpb, rp, CW), F32), pltpu.VMEM((gpb, rp, CW), F32)],
        compiler_params=_params(("parallel",)),
        name="s5_scan",
    )(u, m, wz, wyt, ar, ai, s0l, perm, perm.T)
    state = fin[:, :nb].reshape(g, nb, 2, 2, S5_STATE).transpose(1, 3, 2, 0, 4)
    return y, state


FFN_ROWS = 512
FFN_CHUNK = 256


def _ffn_rows(x, g, shift, scale, gate, wg_ref, wu_ref, wd_ref):
    h = _norm_mod(x, g, shift, scale).astype(BF16)
    acc = jnp.zeros(x.shape, F32)
    for j in range(D_FF // FFN_CHUNK):
        sl = slice(j * FFN_CHUNK, (j + 1) * FFN_CHUNK)
        gg = _dot(h, wg_ref[:, sl])
        uu = _dot(h, wu_ref[:, sl])
        act = (gg * jax.nn.sigmoid(gg) * uu).astype(BF16)
        acc = acc + _dot(act, wd_ref[sl, :])
    return x + gate * acc


def _ffn_specs(layer, row_fn, wg, wu, wd):
    return [_full_spec((1, D_MODEL)),
            _mod_spec(layer, 3, row_fn), _mod_spec(layer, 4, row_fn), _mod_spec(layer, 5, row_fn),
            _resident_spec(wg.shape), _resident_spec(wu.shape), _resident_spec(wd.shape)]


def _even_tail_kernel(a_ref, y_ref, u_ref, x_ref, d_ref, gw_ref, gb_ref, wo_ref, gate_ref,
                      g2_ref, sh2_ref, sc2_ref, gate2_ref, wg_ref, wu_ref, wd_ref, o_ref):
    y = jax.nn.gelu(y_ref[...] + d_ref[...] * u_ref[...])
    y = y * jax.nn.sigmoid(_dot(y.astype(BF16), gw_ref[...]) + gb_ref[...])
    o = _dot(a_ref[...].astype(BF16), wo_ref[0:NA_WIDTH, :]) + _dot(y.astype(BF16), wo_ref[NA_WIDTH:, :])
    x = x_ref[...] + gate_ref[0] * o
    o_ref[...] = _ffn_rows(x, g2_ref[...], sh2_ref[0], sc2_ref[0], gate2_ref[0], wg_ref, wu_ref, wd_ref)


def _even_tail(a, y, u, x, d, glu_w, glu_b, w_out, g_ffn, wg, wu, wd, mod, layer, row_fn):
    t = x.shape[0]
    tm = FFN_ROWS
    half = pl.BlockSpec((tm, NA_WIDTH), lambda i: (i, 0))
    full = pl.BlockSpec((tm, D_MODEL), lambda i: (i, 0))
    return pl.pallas_call(
        _even_tail_kernel,
        grid=(t // tm,),
        in_specs=[half, half, half, full, _full_spec((1, S5_WIDTH)), _resident_spec(glu_w.shape),
                  _full_spec((1, S5_WIDTH)), _resident_spec(w_out.shape), _mod_spec(layer, 2, row_fn)]
                 + _ffn_specs(layer, row_fn, wg, wu, wd),
        out_specs=full,
        out_shape=jax.ShapeDtypeStruct((t, D_MODEL), F32),
        compiler_params=_params(("parallel",)),
        name="even_tail",
    )(a, y, u, x, d, glu_w, glu_b, w_out, mod, g_ffn, mod, mod, mod, wg, wu, wd)


def _dft_consts(seq):
    def cs(n):
        k = np.arange(n)
        ang = 2.0 * np.pi * ((k[:, None] * k[None, :]) % n) / n
        return np.cos(ang), np.sin(ang)

    cc, sc = cs(FNET_GROUP_WIDTH)
    cl, sl = cs(seq)
    chan = jnp.asarray(np.concatenate([cc, sc], axis=1), dtype=F32)
    pos = jnp.asarray(np.concatenate([cl, -sl], axis=1), dtype=F32)
    return chan.astype(BF16), pos.astype(BF16)


def _fourier_rows(x, g, shift, scale, gate, win_ref, chan_ref, pos_ref, wout_ref):
    seq = x.shape[0]
    h = _norm_mod(x, g, shift, scale).astype(BF16)
    z = _dot(h, win_ref[...]).astype(BF16)
    w = FNET_GROUP_WIDTH
    pos = pos_ref[...]
    parts = []
    for c in range(FNET_GROUPS):
        ab = _dot(z[:, c * w:(c + 1) * w], chan_ref[...]).astype(BF16)
        stacked = jnp.concatenate([ab[:, :w], ab[:, w:]], axis=0)
        parts.append(_dot(pos, stacked))
    f = jnp.concatenate(parts, axis=1) * (1.0 / math.sqrt(seq * w))
    return x + gate * _dot(f.astype(BF16), wout_ref[...])


def _odd_layer_kernel(x_ref, g_ref, sh_ref, sc_ref, win_ref, chan_ref, pos_ref, wout_ref, gate_ref,
                      g2_ref, sh2_ref, sc2_ref, gate2_ref, wg_ref, wu_ref, wd_ref, fg_ref, o_ref, *, seq):
    tm = x_ref.shape[0]
    for r0 in range(0, tm, seq):
        rows = slice(r0, r0 + seq)
        o_ref[rows, :] = _fourier_rows(x_ref[rows, :], g_ref[...], sh_ref[0], sc_ref[0], gate_ref[0],
                                       win_ref, chan_ref, pos_ref, wout_ref)
    for r0 in range(0, tm, FFN_ROWS):
        rows = slice(r0, r0 + FFN_ROWS)
        y = _ffn_rows(o_ref[rows, :], g2_ref[...], sh2_ref[0], sc2_ref[0], gate2_ref[0], wg_ref, wu_ref, wd_ref)
        ms = jnp.mean(y * y, axis=-1, keepdims=True)
        o_ref[rows, :] = y * lax.rsqrt(ms + EPS) * fg_ref[...]


def _odd_layer(x, g_mix, g_ffn, final_g, mod, layer, row_of, w_in, w_out, wg, wu, wd, seq):
    t = x.shape[0]
    tm = max(seq, FFN_ROWS)
    row_fn = row_of(tm)
    chan, pos = _dft_consts(seq)
    blk = pl.BlockSpec((tm, D_MODEL), lambda i: (i, 0))
    return pl.pallas_call(
        functools.partial(_odd_layer_kernel, seq=seq),
        grid=(t // tm,),
        in_specs=[blk, _full_spec((1, D_MODEL)), _mod_spec(layer, 0, row_fn), _mod_spec(layer, 1, row_fn),
                  _resident_spec(w_in.shape), _resident_spec(chan.shape), _resident_spec(pos.shape),
                  _resident_spec(w_out.shape), _mod_spec(layer, 2, row_fn)]
                 + _ffn_specs(layer, row_fn, wg, wu, wd) + [_full_spec((1, D_MODEL))],
        out_specs=blk,
        out_shape=jax.ShapeDtypeStruct((t, D_MODEL), F32),
        compiler_params=_params(("parallel",)),
        name="odd_layer",
    )(x, g_mix, mod, mod, w_in, chan, pos, w_out, mod, g_ffn, mod, mod, mod, wg, wu, wd, final_g)


def kernel(x_prompt, x_sample, cache_na_k, cache_na_v, state_s5, c, c_ctx, w_mod, b_mod, norm_mix_g, norm_ffn_g, w_in_even, na_rpb, s5_lam_re, s5_lam_im, s5_log_step, s5_b_re, s5_b_im, s5_c_re, s5_c_im, s5_d, s5_glu_w, s5_glu_b, w_out_even, w_in_odd, w_out_odd, ffn_w_gate, ffn_w_up, ffn_w_down, final_norm_g):
    bp, lp, _ = x_prompt.shape
    bs, ls, _ = x_sample.shape
    past = cache_na_k.shape[2]
    depth = w_mod.shape[0]
    assert bs + 1 <= N_COND and depth == 2

    cond = jnp.zeros((COND_ROWS, D_MODEL), F32).at[0].set(c_ctx).at[1:1 + bs].set(c)
    mod = _modulation(cond.T, w_mod, b_mod).reshape(depth * COND_ROWS * 6, 1, D_MODEL)

    streams = [
        dict(x=x_prompt.reshape(bp * lp, D_MODEL), nb=bp, seq=lp, row_of=lambda tm: (lambda i: 0)),
        dict(x=x_sample.reshape(bs * ls, D_MODEL), nb=bs, seq=ls, row_of=lambda tm: (lambda i: 1 + (i * tm) // ls)),
    ]
    tm = FFN_ROWS

    e = 0
    w_in = w_in_even[e].astype(BF16)
    w_out = w_out_even[e].astype(BF16)
    glu_w = s5_glu_w[e].astype(BF16)
    glu_b = s5_glu_b[e].reshape(1, S5_WIDTH)
    g_mix = norm_mix_g[0].reshape(1, D_MODEL)
    g_ffn0 = norm_ffn_g[0].reshape(1, D_MODEL)
    wg0, wu0, wd0 = (w[0].astype(BF16) for w in (ffn_w_gate, ffn_w_up, ffn_w_down))
    s5_ops = _s5_prep(s5_lam_re[e], s5_lam_im[e], s5_log_step[e], s5_b_re[e], s5_b_im[e], s5_c_re[e], s5_c_im[e])
    s5_dvec = s5_d[e].reshape(1, S5_WIDTH)
    bias = _na_bias(na_rpb[e], ls)

    new_k = new_v = new_s = None
    for si, st in enumerate(streams):
        nb, seq = st["nb"], st["seq"]
        q, k, v, u = _inproj(st["x"], g_mix, mod, 0, st["row_of"](tm), w_in, tm)
        if si == 0:
            a = _ctx_attention(q, k, v, seq)
            s0 = None
            new_k = k.reshape(nb, 1, seq, NA_HEADS, HEAD_DIM)
            new_v = v.reshape(nb, 1, seq, NA_HEADS, HEAD_DIM)
        else:
            a = _na_attention(q.reshape(nb, seq, NA_WIDTH), k.reshape(nb, seq, NA_WIDTH), v.reshape(nb, seq, NA_WIDTH),
                              cache_na_k[:, e].reshape(nb, past, NA_WIDTH), cache_na_v[:, e].reshape(nb, past, NA_WIDTH),
                              bias).reshape(nb * seq, NA_WIDTH)
            s0 = state_s5[:, e].astype(F32)
        y, state = _s5_mixer(u, s0, s5_ops, nb, seq)
        if si == 0:
            new_s = state[:, None]
        st["x"] = _even_tail(a, y, u, st["x"], s5_dvec, glu_w, glu_b, w_out, g_ffn0, wg0, wu0, wd0,
                             mod, 0, st["row_of"](tm))

    wg1, wu1, wd1 = (w[1].astype(BF16) for w in (ffn_w_gate, ffn_w_up, ffn_w_down))
    for st in streams:
        st["x"] = _odd_layer(st["x"], norm_mix_g[1].reshape(1, D_MODEL), norm_ffn_g[1].reshape(1, D_MODEL),
                             final_norm_g.reshape(1, D_MODEL), mod, 1, st["row_of"],
                             w_in_odd[0].astype(BF16), w_out_odd[0].astype(BF16), wg1, wu1, wd1, st["seq"])

    y_prompt = streams[0]["x"].reshape(bp, lp, D_MODEL)
    y_sample = streams[1]["x"].reshape(bs, ls, D_MODEL)
    return (y_prompt, y_sample, new_k, new_v, new_s)
```

```python
import functools
import math

import numpy as np
import jax
import jax.numpy as jnp
from jax import lax
from jax.experimental import pallas as pl
from jax.experimental.pallas import tpu as pltpu

F32 = jnp.float32
BF16 = jnp.bfloat16

D_MODEL = 1024
NA_HEADS = 8
HEAD_DIM = 64
NA_WIDTH = NA_HEADS * HEAD_DIM
GRID_W = 64
WIN_R = 8
WIN_C = 16
S5_GROUP = 16
S5_GROUPS = 32
S5_STATE = 64
S5_WIDTH = S5_GROUPS * S5_GROUP
FNET_GROUPS = 4
FNET_GROUP_WIDTH = D_MODEL // FNET_GROUPS
D_FF = 2816
EPS = 1e-6

CHUNK = 16
CW = CHUNK * S5_GROUP
GROUPS_PER_BLOCK = 128 // S5_GROUP
N_COND = 3
COND_ROWS = 8
NEG = -1e30
VMEM_LIMIT = 56 * 2 ** 20

_NT = (((1,), (1,)), ((), ()))


def _params(sem, vmem=VMEM_LIMIT):
    return pltpu.CompilerParams(dimension_semantics=sem, vmem_limit_bytes=vmem)


def _dot(a, b):
    return jnp.dot(a, b, preferred_element_type=F32)


def _dot_nt(a, b):
    return lax.dot_general(a, b, _NT, preferred_element_type=F32)


def _norm_mod(x, g, shift, scale):
    ms = jnp.mean(x * x, axis=-1, keepdims=True)
    return (x * lax.rsqrt(ms + EPS) * g) * (1.0 + scale) + shift


def _mod_spec(layer, which, row_fn):
    return pl.BlockSpec((1, 1, D_MODEL),
                        lambda *idx: (layer * COND_ROWS * 6 + row_fn(*idx) * 6 + which, 0, 0))


def _full_spec(shape):
    nd = len(shape)
    return pl.BlockSpec(shape, lambda *idx: (0,) * nd)


def _resident_spec(shape):
    nd = len(shape)
    return pl.BlockSpec(shape, lambda *idx: (0,) * nd, pipeline_mode=pl.Buffered(1))


def _mod_kernel(ct_ref, w_ref, b_ref, o_ref):
    ct = ct_ref[...]
    st = ct * jax.nn.sigmoid(ct)
    w = w_ref[0]
    o_ref[0] = jnp.zeros(o_ref.shape[1:], F32)
    for r in range(N_COND):
        o_ref[0, r:r + 1, :] = jnp.sum(w * st[:, r:r + 1], axis=0, keepdims=True) + b_ref[0]


def _modulation(cond_t, w_mod, b_mod):
    depth, _, n = w_mod.shape
    tn = 1024
    return pl.pallas_call(
        _mod_kernel,
        grid=(depth, n // tn),
        in_specs=[_full_spec((D_MODEL, COND_ROWS)),
                  pl.BlockSpec((1, D_MODEL, tn), lambda l, j: (l, 0, j)),
                  pl.BlockSpec((1, 1, tn), lambda l, j: (l, 0, j))],
        out_specs=pl.BlockSpec((1, COND_ROWS, tn), lambda l, j: (l, 0, j)),
        out_shape=jax.ShapeDtypeStruct((depth, COND_ROWS, n), F32),
        compiler_params=_params(("parallel", "parallel")),
        name="modulation",
    )(cond_t, w_mod, b_mod.reshape(depth, 1, n))


def _ctx_attention_rows(q, k, v, seq):
    scale = HEAD_DIM ** -0.5
    nseq = q.shape[0] // seq
    qt = (q * scale).T.astype(BF16)
    vt = v.T.astype(BF16)
    kb = k.astype(BF16)
    probs = [(slice(i * seq, (i + 1) * seq), slice(h * HEAD_DIM, (h + 1) * HEAD_DIM))
             for i in range(nseq) for h in range(NA_HEADS)]
    sts = [_dot(kb[rows, sl], qt[sl, rows]) for rows, sl in probs]
    pts = [jnp.exp(st - jnp.max(st, axis=0, keepdims=True)) for st in sts]
    dens = [jnp.sum(pt, axis=0, keepdims=True) for pt in pts]
    outs = [_dot(vt[sl, rows], pt.astype(BF16)) / den for (rows, sl), pt, den in zip(probs, pts, dens)]
    per_seq = [jnp.concatenate(outs[i * NA_HEADS:(i + 1) * NA_HEADS], axis=0).T for i in range(nseq)]
    return jnp.concatenate(per_seq, axis=0)


def _inproj_kernel(x_ref, g_ref, sh_ref, sc_ref, w_ref, *o_refs, ctx_seq):
    h = _norm_mod(x_ref[...], g_ref[...], sh_ref[0], sc_ref[0]).astype(BF16)
    z = _dot(h, w_ref[...])
    q, k, v, u = (z[:, i * NA_WIDTH:(i + 1) * NA_WIDTH] for i in range(4))
    first = q if ctx_seq is None else _ctx_attention_rows(q, k, v, ctx_seq)
    for o, val in zip(o_refs, (first, k, v, u)):
        o[...] = val.astype(o.dtype)


def _inproj(x, g, mod, layer, row_fn, w, tm, ctx_seq=None):
    t = x.shape[0]
    flat = pl.BlockSpec((tm, NA_WIDTH), lambda i: (i, 0))
    return pl.pallas_call(
        functools.partial(_inproj_kernel, ctx_seq=ctx_seq),
        grid=(t // tm,),
        in_specs=[pl.BlockSpec((tm, D_MODEL), lambda i: (i, 0)),
                  _full_spec((1, D_MODEL)),
                  _mod_spec(layer, 0, row_fn), _mod_spec(layer, 1, row_fn),
                  _resident_spec(w.shape)],
        out_specs=[flat] * 4,
        out_shape=[jax.ShapeDtypeStruct((t, NA_WIDTH), BF16)] + [jax.ShapeDtypeStruct((t, NA_WIDTH), F32)] * 3,
        compiler_params=_params(("parallel",)),
        name="inproj_even",
    )(x, g, mod, mod, w)


def _na_bias_kernel(rpb_ref, o_ref, *, rows):
    h = pl.program_id(0)
    kr_win = min(WIN_R, rows)
    qc = lax.broadcasted_iota(jnp.int32, (GRID_W, 2 * GRID_W), 0)
    lane = lax.broadcasted_iota(jnp.int32, (GRID_W, 2 * GRID_W), 1)
    kc = lane & (GRID_W - 1)
    cs = jnp.clip(qc - WIN_C // 2, 0, GRID_W - WIN_C)
    col_ok = (kc >= cs) & (kc < cs + WIN_C)
    dc = kc - qc + (WIN_C - 1)
    left = lane < GRID_W
    neg = jnp.full((GRID_W, 2 * GRID_W), NEG, F32)
    toeplitz = []
    for dr in range(2 * WIN_R - 1):
        t = neg
        for d in range(2 * WIN_C - 1):
            t = jnp.where(dc == d, rpb_ref[h, dr, d], t)
        toeplitz.append(jnp.where(col_ok, t, neg))
    for r in range(rows):
        rs = min(max(r - kr_win // 2, 0), rows - kr_win)

        def blk(kr):
            return toeplitz[kr - r + WIN_R - 1] if rs <= kr < rs + kr_win else None

        for j in range(rows // 2):
            a, b = blk(2 * j), blk(2 * j + 1)
            if a is None and b is None:
                pair = neg
            else:
                pair = jnp.where(left, neg if a is None else a, neg if b is None else b)
            o_ref[0, r * GRID_W:(r + 1) * GRID_W, j * 2 * GRID_W:(j + 1) * 2 * GRID_W] = pair


def _na_bias(rpb, seq):
    rows = seq // GRID_W
    return pl.pallas_call(
        functools.partial(_na_bias_kernel, rows=rows),
        grid=(NA_HEADS,),
        in_specs=[pl.BlockSpec(memory_space=pltpu.SMEM)],
        out_specs=pl.BlockSpec((1, seq, seq), lambda h: (h, 0, 0)),
        out_shape=jax.ShapeDtypeStruct((NA_HEADS, seq, seq), F32),
        compiler_params=_params(("parallel",)),
        name="na_bias",
    )(rpb)


def _na_key_range(qb, tq, rows):
    kr_win = min(WIN_R, rows)
    r0, r1 = qb * tq // GRID_W, ((qb + 1) * tq - 1) // GRID_W
    rs0 = min(max(r0 - kr_win // 2, 0), rows - kr_win)
    rs1 = min(max(r1 - kr_win // 2, 0), rows - kr_win)
    lo, hi = rs0 * GRID_W, (rs1 + kr_win) * GRID_W
    return lo // 128 * 128, -(-hi // 128) * 128


def _na_attn_kernel(q_ref, k_ref, v_ref, ck_ref, cv_ref, bias_ref, o_ref, *, tq):
    scale = HEAD_DIM ** -0.5
    seq = q_ref.shape[1]
    probs = []
    for hl in range(2):
        sl = slice(hl * HEAD_DIM, (hl + 1) * HEAD_DIM)
        for qb in range(seq // tq):
            lo, hi = _na_key_range(qb, tq, seq // GRID_W)
            probs.append((hl, sl, slice(qb * tq, (qb + 1) * tq), slice(lo, hi)))
    head0 = pl.program_id(0) * 2
    cks = [ck_ref[0, 0, :, head0 + hl, :].astype(BF16) for hl in range(2)]
    cvs = [cv_ref[0, 0, :, head0 + hl, :].astype(BF16) for hl in range(2)]
    qs_all = [(q_ref[0, qs, sl] * scale).astype(BF16) for _, sl, qs, _ in probs]
    s_loc = [_dot_nt(q, k_ref[0, ks, sl].astype(BF16)) + bias_ref[hl, qs, ks]
             for q, (hl, sl, qs, ks) in zip(qs_all, probs)]
    s_ctx = [_dot_nt(q, cks[hl]) for q, (hl, _, _, _) in zip(qs_all, probs)]
    ms = [jnp.maximum(jnp.max(a, axis=-1, keepdims=True), jnp.max(b, axis=-1, keepdims=True))
          for a, b in zip(s_loc, s_ctx)]
    p_loc = [jnp.exp(a - m) for a, m in zip(s_loc, ms)]
    p_ctx = [jnp.exp(b - m) for b, m in zip(s_ctx, ms)]
    dens = [jnp.sum(a, axis=-1, keepdims=True) + jnp.sum(b, axis=-1, keepdims=True) for a, b in zip(p_loc, p_ctx)]
    for a, b, den, (hl, sl, qs, ks) in zip(p_loc, p_ctx, dens, probs):
        o = _dot(a.astype(BF16), v_ref[0, ks, sl].astype(BF16)) + _dot(b.astype(BF16), cvs[hl])
        o_ref[0, qs, sl] = o / den


def _na_attention(q, k, v, ck, cv, layer, bias):
    nb, seq, _ = q.shape
    lat = pl.BlockSpec((1, seq, 2 * HEAD_DIM), lambda hp, b: (b, 0, hp))
    ctx = pl.BlockSpec((1, 1) + ck.shape[2:], lambda hp, b: (b, layer, 0, 0, 0))
    return pl.pallas_call(
        functools.partial(_na_attn_kernel, tq=256),
        grid=(NA_HEADS // 2, nb),
        in_specs=[lat, lat, lat, ctx, ctx, pl.BlockSpec((2, seq, seq), lambda hp, b: (hp, 0, 0))],
        out_specs=lat,
        out_shape=jax.ShapeDtypeStruct((nb, seq, NA_WIDTH), F32),
        compiler_params=_params(("parallel", "arbitrary")),
        name="na_attention",
    )(q, k, v, ck, cv, bias)


def _s5_prep_group(i, lamr_ref, lami_ref, logdt_ref, btr_ref, bti_ref, cr_ref, ci_ref,
                   m_ref, wz_ref, wyt_ref, ar_ref, ai_ref):
    lr = jnp.minimum(lamr_ref[i], -1e-4)
    li = lami_ref[i]
    dt = jnp.exp(logdt_ref[i])
    a = lr * dt
    th = li * dt
    mag = jnp.exp(a)
    lbr = mag * jnp.cos(th)
    lbi = mag * jnp.sin(th)
    den = lr * lr + li * li
    nr = lbr - 1.0
    coef_r = (nr * lr + lbi * li) / den
    coef_i = (lbi * lr - nr * li) / den
    btr, bti = btr_ref[i], bti_ref[i]
    bbr = coef_r * btr - coef_i * bti
    bbi = coef_r * bti + coef_i * btr
    cr, ci = cr_ref[i], ci_ref[i]

    step = lax.broadcasted_iota(jnp.int32, (CHUNK, 2 * S5_STATE), 0).astype(F32)
    fwd = lax.broadcasted_iota(jnp.int32, (CHUNK, 2 * S5_STATE), 1) < S5_STATE

    def powers(e):
        pm = jnp.exp(a * e)
        return pm * jnp.cos(th * e), pm * jnp.sin(th * e)

    pzr, pzi = powers(jnp.where(fwd, CHUNK - 1 - step, step))
    pyr, pyi = powers(jnp.where(fwd, step + 1.0, CHUNK - step))
    p16r, p16i = powers(jnp.full((1, 2 * S5_STATE), float(CHUNK), F32))
    ar_ref[i] = p16r
    ai_ref[i] = p16i

    def outer(p, c):
        return (p[:, None, :] * c[None, :, :]).reshape(CW, 2 * S5_STATE)

    wzr = outer(pzr, bbr) - outer(pzi, bbi)
    wzi = outer(pzr, bbi) + outer(pzi, bbr)
    wz_ref[i, :, 0:128] = wzr.astype(wz_ref.dtype)
    wz_ref[i, :, 128:256] = wzi.astype(wz_ref.dtype)
    wyt_ref[i, :, 0:128] = (outer(pyr, cr) - outer(pyi, ci)).astype(wyt_ref.dtype)
    wyt_ref[i, :, 128:256] = (-(outer(pyi, cr) + outer(pyr, ci))).astype(wyt_ref.dtype)

    ones = jnp.ones((CHUNK, 2 * S5_STATE), F32)
    cwr = outer(ones, cr)
    cwi = outer(ones, ci)
    fwd_rows = lax.broadcasted_iota(jnp.int32, (CW, 2 * S5_STATE), 1) < S5_STATE
    zero = jnp.zeros((CW, 2 * S5_STATE), F32)

    def split(x):
        hi = x.astype(BF16)
        return hi, (x - hi.astype(F32)).astype(BF16)

    cw_hi, cw_lo = split(jnp.concatenate([cwr, -cwi], axis=1))

    def kk(sel):
        lhs_hi, lhs_lo = split(jnp.concatenate([jnp.where(sel, wzr, zero), jnp.where(sel, wzi, zero)], axis=1))
        return _dot_nt(lhs_hi, cw_hi) + (_dot_nt(lhs_hi, cw_lo) + _dot_nt(lhs_lo, cw_hi))

    kk_f = kk(fwd_rows)
    kk_b = kk(jnp.logical_not(fwd_rows))
    col_step = lax.broadcasted_iota(jnp.int32, (CW, CW), 1) // S5_GROUP
    m = jnp.zeros((CW, CW), F32)
    for t in range(CHUNK):
        up = (CHUNK - 1 - t) * S5_GROUP
        dn = t * S5_GROUP
        piece_f = kk_f if up == 0 else jnp.concatenate([kk_f[up:], jnp.zeros((up, CW), F32)], axis=0)
        piece_b = kk_b if dn == 0 else jnp.concatenate([jnp.zeros((dn, CW), F32), kk_b[:CW - dn]], axis=0)
        m = jnp.where(col_step == t, piece_f + piece_b, m)
    m_ref[i] = m.astype(m_ref.dtype)


def _s5_prep_kernel(*refs):
    for i in range(refs[0].shape[0]):
        _s5_prep_group(i, *refs)


def _s5_prep(lam_re, lam_im, log_step, b_re, b_im, c_re, c_im):
    g = S5_GROUPS
    lanes = 2 * S5_STATE

    def dirs_last(x):
        return x.transpose(1, 0, 2).reshape(g, 1, lanes)

    lamr, lami = dirs_last(lam_re), dirs_last(lam_im)
    logdt = dirs_last(jnp.broadcast_to(log_step[:, :, None], (2, g, S5_STATE)))
    bt = [x.transpose(1, 3, 0, 2).reshape(g, S5_GROUP, lanes) for x in (b_re, b_im)]
    ct = [x.transpose(1, 2, 0, 3).reshape(g, S5_GROUP, lanes) for x in (c_re, c_im)]
    gp = 4
    vec = pl.BlockSpec((gp, 1, lanes), lambda i: (i, 0, 0))
    mat = pl.BlockSpec((gp, S5_GROUP, lanes), lambda i: (i, 0, 0))
    big = pl.BlockSpec((gp, CW, CW), lambda i: (i, 0, 0))
    return pl.pallas_call(
        _s5_prep_kernel,
        grid=(g // gp,),
        in_specs=[vec, vec, vec, mat, mat, mat, mat],
        out_specs=[big, big, big, vec, vec],
        out_shape=[jax.ShapeDtypeStruct((g, CW, CW), BF16)] * 3 + [jax.ShapeDtypeStruct((g, 1, lanes), F32)] * 2,
        compiler_params=_params(("parallel",)),
        name="s5_prep",
    )(lamr, lami, logdt, bt[0], bt[1], ct[0], ct[1])


def _s5_kernel(u_ref, m_ref, wz_ref, wyt_ref, ar_ref, ai_ref, s0_ref, perm_ref, permt_ref, y_ref, fin_ref,
               xt_scr, z_scr, sp_scr, *, nbp, nc):
    s = S5_STATE
    r = xt_scr.shape[2]
    for t in range(CHUNK):
        xt_scr[t] = u_ref[pl.ds(t, r, stride=CHUNK), :].T
    fwd = lax.broadcasted_iota(jnp.int32, (nbp, 2 * s), 1) < s
    groups = range(GROUPS_PER_BLOCK)
    chans = [slice(gl * S5_GROUP, (gl + 1) * S5_GROUP) for gl in groups]

    us = [jnp.concatenate([xt_scr[t, ch, :] for t in range(CHUNK)], axis=0).T.astype(BF16) for ch in chans]
    for gl in groups:
        z_scr[gl] = _dot(_dot(perm_ref[...], us[gl]).astype(BF16), wz_ref[gl])

    ars = [jnp.broadcast_to(ar_ref[gl], (nbp, 2 * s)) for gl in groups]
    ais = [jnp.broadcast_to(ai_ref[gl], (nbp, 2 * s)) for gl in groups]
    srs = [s0_ref[gl, :, 0:2 * s] for gl in groups]
    sis = [s0_ref[gl, :, 2 * s:4 * s] for gl in groups]
    for k in range(nc):
        up = slice(k * nbp, (k + 1) * nbp)
        dn = slice((nc - 1 - k) * nbp, (nc - k) * nbp)
        for gl in groups:
            sr, si = srs[gl], sis[gl]
            sp_scr[gl, up, 0:s] = sr[:, 0:s]
            sp_scr[gl, dn, s:2 * s] = sr[:, s:2 * s]
            sp_scr[gl, up, 2 * s:3 * s] = si[:, 0:s]
            sp_scr[gl, dn, 3 * s:4 * s] = si[:, s:2 * s]
            zr = jnp.where(fwd, z_scr[gl, up, 0:2 * s], z_scr[gl, dn, 0:2 * s])
            zi = jnp.where(fwd, z_scr[gl, up, 2 * s:4 * s], z_scr[gl, dn, 2 * s:4 * s])
            srs[gl] = ars[gl] * sr - ais[gl] * si + zr
            sis[gl] = ars[gl] * si + ais[gl] * sr + zi
    for gl in groups:
        fin_ref[gl, :, 0:2 * s] = srs[gl]
        fin_ref[gl, :, 2 * s:4 * s] = sis[gl]
    sps = [_dot(permt_ref[...], sp_scr[gl].astype(BF16)).astype(BF16) for gl in groups]
    for gl in groups:
        yt = (_dot(us[gl], m_ref[gl]) + _dot_nt(sps[gl], wyt_ref[gl])).T
        for t in range(CHUNK):
            xt_scr[t, chans[gl], :] = yt[t * S5_GROUP:(t + 1) * S5_GROUP, :]

    for t in range(CHUNK):
        y_ref[pl.ds(t, r, stride=CHUNK), :] = xt_scr[t].T


def _s5_mixer(u, s0, ops, nb, seq):
    m, wz, wyt, ar, ai = ops
    g, nc = S5_GROUPS, seq // CHUNK
    nbp = -(-nb // 8) * 8
    r, rp = nb * nc, nc * nbp
    gpb = GROUPS_PER_BLOCK
    s0l = jnp.zeros((g, nbp, CW), F32)
    if s0 is not None:
        s0l = s0l.at[:, :nb].set(s0.transpose(3, 0, 2, 1, 4).reshape(g, nb, CW))
    perm = np.zeros((nc, nbp, nb, nc), np.float32)
    for b in range(nb):
        perm[np.arange(nc), b, b, np.arange(nc)] = 1.0
    perm = jnp.asarray(perm.reshape(rp, r)).astype(BF16)
    col = pl.BlockSpec((nb * seq, gpb * S5_GROUP), lambda j: (0, j))
    wspec = pl.BlockSpec((gpb, CW, CW), lambda j: (j, 0, 0))
    cspec = pl.BlockSpec((gpb, 1, 2 * S5_STATE), lambda j: (j, 0, 0))
    sspec = pl.BlockSpec((gpb, nbp, CW), lambda j: (j, 0, 0))
    y, fin = pl.pallas_call(
        functools.partial(_s5_kernel, nbp=nbp, nc=nc),
        grid=(g // gpb,),
        in_specs=[col, wspec, wspec, wspec, cspec, cspec, sspec, _full_spec((rp, r)), _full_spec((r, rp))],
        out_specs=[col, sspec],
        out_shape=[jax.ShapeDtypeStruct((nb * seq, S5_WIDTH), F32), jax.ShapeDtypeStruct((g, nbp, CW), F32)],
        scratch_shapes=[pltpu.VMEM((CHUNK, gpb * S5_GROUP, r), F32),
                        pltpu.VMEM((gpb, rp, CW), F32), pltpu.VMEM((gpb, rp, CW), F32)],
        compiler_params=_params(("parallel",)),
        name="s5_scan",
    )(u, m, wz, wyt, ar, ai, s0l, perm, perm.T)
    state = fin[:, :nb].reshape(g, nb, 2, 2, S5_STATE).transpose(1, 3, 2, 0, 4)
    return y, state


FFN_ROWS = 512
FFN_CHUNK = 256


def _ffn_rows(x, g, shift, scale, gate, wg_ref, wu_ref, wd_ref):
    h = _norm_mod(x, g, shift, scale).astype(BF16)
    acc = jnp.zeros(x.shape, F32)
    for j in range(D_FF // FFN_CHUNK):
        sl = slice(j * FFN_CHUNK, (j + 1) * FFN_CHUNK)
        gg = _dot(h, wg_ref[:, sl])
        uu = _dot(h, wu_ref[:, sl])
        act = (gg * jax.nn.sigmoid(gg) * uu).astype(BF16)
        acc = acc + _dot(act, wd_ref[sl, :])
    return x + gate * acc


def _ffn_specs(layer, row_fn, wg, wu, wd):
    return [_full_spec((1, D_MODEL)),
            _mod_spec(layer, 3, row_fn), _mod_spec(layer, 4, row_fn), _mod_spec(layer, 5, row_fn),
            _resident_spec(wg.shape), _resident_spec(wu.shape), _resident_spec(wd.shape)]


def _even_tail_kernel(a_ref, y_ref, u_ref, x_ref, d_ref, gw_ref, gb_ref, wo_ref, gate_ref,
                      g2_ref, sh2_ref, sc2_ref, gate2_ref, wg_ref, wu_ref, wd_ref, o_ref):
    y = jax.nn.gelu(y_ref[...] + d_ref[...] * u_ref[...])
    y = y * jax.nn.sigmoid(_dot(y.astype(BF16), gw_ref[...]) + gb_ref[...])
    o = _dot(a_ref[...].astype(BF16), wo_ref[0:NA_WIDTH, :]) + _dot(y.astype(BF16), wo_ref[NA_WIDTH:, :])
    x = x_ref[...] + gate_ref[0] * o
    o_ref[...] = _ffn_rows(x, g2_ref[...], sh2_ref[0], sc2_ref[0], gate2_ref[0], wg_ref, wu_ref, wd_ref)


def _even_tail(a, y, u, x, d, glu_w, glu_b, w_out, g_ffn, wg, wu, wd, mod, layer, row_fn):
    t = x.shape[0]
    tm = FFN_ROWS
    half = pl.BlockSpec((tm, NA_WIDTH), lambda i: (i, 0))
    full = pl.BlockSpec((tm, D_MODEL), lambda i: (i, 0))
    return pl.pallas_call(
        _even_tail_kernel,
        grid=(t // tm,),
        in_specs=[half, half, half, full, _full_spec((1, S5_WIDTH)), _resident_spec(glu_w.shape),
                  _full_spec((1, S5_WIDTH)), _resident_spec(w_out.shape), _mod_spec(layer, 2, row_fn)]
                 + _ffn_specs(layer, row_fn, wg, wu, wd),
        out_specs=full,
        out_shape=jax.ShapeDtypeStruct((t, D_MODEL), F32),
        compiler_params=_params(("parallel",)),
        name="even_tail",
    )(a, y, u, x, d, glu_w, glu_b, w_out, mod, g_ffn, mod, mod, mod, wg, wu, wd)


def _dft_consts(seq):
    def cs(n):
        k = np.arange(n)
        ang = 2.0 * np.pi * ((k[:, None] * k[None, :]) % n) / n
        return np.cos(ang), np.sin(ang)

    cc, sc = cs(FNET_GROUP_WIDTH)
    cl, sl = cs(seq)
    chan = jnp.asarray(np.concatenate([cc, sc], axis=1), dtype=F32)
    pos = jnp.asarray(np.concatenate([cl, -sl], axis=1), dtype=F32)
    return chan.astype(BF16), pos.astype(BF16)


def _fourier_rows(x, g, shift, scale, gate, win_ref, chan_ref, pos_ref, wout_ref):
    seq = x.shape[0]
    h = _norm_mod(x, g, shift, scale).astype(BF16)
    z = _dot(h, win_ref[...]).astype(BF16)
    w = FNET_GROUP_WIDTH
    pos = pos_ref[...]
    parts = []
    for c in range(FNET_GROUPS):
        ab = _dot(z[:, c * w:(c + 1) * w], chan_ref[...]).astype(BF16)
        stacked = jnp.concatenate([ab[:, :w], ab[:, w:]], axis=0)
        parts.append(_dot(pos, stacked))
    f = jnp.concatenate(parts, axis=1) * (1.0 / math.sqrt(seq * w))
    return x + gate * _dot(f.astype(BF16), wout_ref[...])


def _odd_layer_kernel(x_ref, g_ref, sh_ref, sc_ref, win_ref, chan_ref, pos_ref, wout_ref, gate_ref,
                      g2_ref, sh2_ref, sc2_ref, gate2_ref, wg_ref, wu_ref, wd_ref, fg_ref, o_ref, *, seq):
    tm = x_ref.shape[0]
    for r0 in range(0, tm, seq):
        rows = slice(r0, r0 + seq)
        o_ref[rows, :] = _fourier_rows(x_ref[rows, :], g_ref[...], sh_ref[0], sc_ref[0], gate_ref[0],
                                       win_ref, chan_ref, pos_ref, wout_ref)
    for r0 in range(0, tm, FFN_ROWS):
        rows = slice(r0, r0 + FFN_ROWS)
        y = _ffn_rows(o_ref[rows, :], g2_ref[...], sh2_ref[0], sc2_ref[0], gate2_ref[0], wg_ref, wu_ref, wd_ref)
        ms = jnp.mean(y * y, axis=-1, keepdims=True)
        o_ref[rows, :] = y * lax.rsqrt(ms + EPS) * fg_ref[...]


def _odd_layer(x, g_mix, g_ffn, final_g, mod, layer, row_of, w_in, w_out, wg, wu, wd, seq):
    t = x.shape[0]
    tm = max(seq, FFN_ROWS)
    row_fn = row_of(tm)
    chan, pos = _dft_consts(seq)
    blk = pl.BlockSpec((tm, D_MODEL), lambda i: (i, 0))
    return pl.pallas_call(
        functools.partial(_odd_layer_kernel, seq=seq),
        grid=(t // tm,),
        in_specs=[blk, _full_spec((1, D_MODEL)), _mod_spec(layer, 0, row_fn), _mod_spec(layer, 1, row_fn),
                  _resident_spec(w_in.shape), _resident_spec(chan.shape), _resident_spec(pos.shape),
                  _resident_spec(w_out.shape), _mod_spec(layer, 2, row_fn)]
                 + _ffn_specs(layer, row_fn, wg, wu, wd) + [_full_spec((1, D_MODEL))],
        out_specs=blk,
        out_shape=jax.ShapeDtypeStruct((t, D_MODEL), F32),
        compiler_params=_params(("parallel",)),
        name="odd_layer",
    )(x, g_mix, mod, mod, w_in, chan, pos, w_out, mod, g_ffn, mod, mod, mod, wg, wu, wd, final_g)


def kernel(x_prompt, x_sample, cache_na_k, cache_na_v, state_s5, c, c_ctx, w_mod, b_mod, norm_mix_g, norm_ffn_g, w_in_even, na_rpb, s5_lam_re, s5_lam_im, s5_log_step, s5_b_re, s5_b_im, s5_c_re, s5_c_im, s5_d, s5_glu_w, s5_glu_b, w_out_even, w_in_odd, w_out_odd, ffn_w_gate, ffn_w_up, ffn_w_down, final_norm_g):
    bp, lp, _ = x_prompt.shape
    bs, ls, _ = x_sample.shape
    depth = w_mod.shape[0]
    assert bs + 1 <= N_COND and depth == 2

    cond = jnp.zeros((COND_ROWS, D_MODEL), F32).at[0].set(c_ctx).at[1:1 + bs].set(c)
    mod = _modulation(cond.T, w_mod, b_mod).reshape(depth * COND_ROWS * 6, 1, D_MODEL)

    streams = [
        dict(x=x_prompt.reshape(bp * lp, D_MODEL), nb=bp, seq=lp, row_of=lambda tm: (lambda i: 0)),
        dict(x=x_sample.reshape(bs * ls, D_MODEL), nb=bs, seq=ls, row_of=lambda tm: (lambda i: 1 + (i * tm) // ls)),
    ]
    tm = FFN_ROWS

    e = 0
    w_in = w_in_even[e].astype(BF16)
    w_out = w_out_even[e].astype(BF16)
    glu_w = s5_glu_w[e].astype(BF16)
    glu_b = s5_glu_b[e].reshape(1, S5_WIDTH)
    g_mix = norm_mix_g[0].reshape(1, D_MODEL)
    g_ffn0 = norm_ffn_g[0].reshape(1, D_MODEL)
    wg0, wu0, wd0 = (w[0].astype(BF16) for w in (ffn_w_gate, ffn_w_up, ffn_w_down))
    s5_ops = _s5_prep(s5_lam_re[e], s5_lam_im[e], s5_log_step[e], s5_b_re[e], s5_b_im[e], s5_c_re[e], s5_c_im[e])
    s5_dvec = s5_d[e].reshape(1, S5_WIDTH)
    bias = _na_bias(na_rpb[e], ls)

    new_k = new_v = new_s = None
    for si, st in enumerate(streams):
        nb, seq = st["nb"], st["seq"]
        if si == 0:
            a, k, v, u = _inproj(st["x"], g_mix, mod, 0, st["row_of"](tm), w_in, tm, ctx_seq=seq)
            s0 = None
            new_k = k.reshape(nb, 1, seq, NA_HEADS, HEAD_DIM)
            new_v = v.reshape(nb, 1, seq, NA_HEADS, HEAD_DIM)
        else:
            q, k, v, u = _inproj(st["x"], g_mix, mod, 0, st["row_of"](tm), w_in, tm)
            a = _na_attention(q.reshape(nb, seq, NA_WIDTH), k.reshape(nb, seq, NA_WIDTH), v.reshape(nb, seq, NA_WIDTH),
                              cache_na_k, cache_na_v, e, bias).reshape(nb * seq, NA_WIDTH)
            s0 = state_s5[:, e].astype(F32)
        y, state = _s5_mixer(u, s0, s5_ops, nb, seq)
        if si == 0:
            new_s = state[:, None]
        st["x"] = _even_tail(a, y, u, st["x"], s5_dvec, glu_w, glu_b, w_out, g_ffn0, wg0, wu0, wd0,
                             mod, 0, st["row_of"](tm))

    wg1, wu1, wd1 = (w[1].astype(BF16) for w in (ffn_w_gate, ffn_w_up, ffn_w_down))
    for st in streams:
        st["x"] = _odd_layer(st["x"], norm_mix_g[1].reshape(1, D_MODEL), norm_ffn_g[1].reshape(1, D_MODEL),
                             final_norm_g.reshape(1, D_MODEL), mod, 1, st["row_of"],
                             w_in_odd[0].astype(BF16), w_out_odd[0].astype(BF16), wg1, wu1, wd1, st["seq"])

    y_prompt = streams[0]["x"].reshape(bp, lp, D_MODEL)
    y_sample = streams[1]["x"].reshape(bs, ls, D_MODEL)
    return (y_prompt, y_sample, new_k, new_v, new_s)
```

```python
import functools
import math

import numpy as np
import jax
import jax.numpy as jnp
from jax import lax
from jax.experimental import pallas as pl
from jax.experimental.pallas import tpu as pltpu

F32 = jnp.float32
BF16 = jnp.bfloat16

D_MODEL = 1024
NA_HEADS = 8
HEAD_DIM = 64
NA_WIDTH = NA_HEADS * HEAD_DIM
GRID_W = 64
WIN_R = 8
WIN_C = 16
S5_GROUP = 16
S5_GROUPS = 32
S5_STATE = 64
S5_WIDTH = S5_GROUPS * S5_GROUP
FNET_GROUPS = 4
FNET_GROUP_WIDTH = D_MODEL // FNET_GROUPS
D_FF = 2816
EPS = 1e-6

CHUNK = 16
CW = CHUNK * S5_GROUP
GROUPS_PER_BLOCK = 128 // S5_GROUP
N_COND = 3
COND_ROWS = 8
NEG = -1e30
VMEM_LIMIT = 56 * 2 ** 20

_NT = (((1,), (1,)), ((), ()))


def _params(sem, vmem=VMEM_LIMIT):
    return pltpu.CompilerParams(dimension_semantics=sem, vmem_limit_bytes=vmem)


def _dot(a, b):
    return jnp.dot(a, b, preferred_element_type=F32)


def _dot_nt(a, b):
    return lax.dot_general(a, b, _NT, preferred_element_type=F32)


def _norm_mod(x, g, shift, scale):
    ms = jnp.mean(x * x, axis=-1, keepdims=True)
    return (x * lax.rsqrt(ms + EPS) * g) * (1.0 + scale) + shift


def _mod_spec(layer, which, row_fn):
    return pl.BlockSpec((1, 1, D_MODEL),
                        lambda *idx: (layer * COND_ROWS * 6 + row_fn(*idx) * 6 + which, 0, 0))


def _full_spec(shape):
    nd = len(shape)
    return pl.BlockSpec(shape, lambda *idx: (0,) * nd)


def _resident_spec(shape):
    nd = len(shape)
    return pl.BlockSpec(shape, lambda *idx: (0,) * nd, pipeline_mode=pl.Buffered(1))


def _mod_kernel(ct_ref, w_ref, b_ref, o_ref):
    ct = ct_ref[...]
    st = ct * jax.nn.sigmoid(ct)
    w = w_ref[0]
    o_ref[0] = jnp.zeros(o_ref.shape[1:], F32)
    for r in range(N_COND):
        o_ref[0, r:r + 1, :] = jnp.sum(w * st[:, r:r + 1], axis=0, keepdims=True) + b_ref[0]


def _modulation(cond_t, w_mod, b_mod):
    depth, _, n = w_mod.shape
    tn = 1024
    return pl.pallas_call(
        _mod_kernel,
        grid=(depth, n // tn),
        in_specs=[_full_spec((D_MODEL, COND_ROWS)),
                  pl.BlockSpec((1, D_MODEL, tn), lambda l, j: (l, 0, j)),
                  pl.BlockSpec((1, 1, tn), lambda l, j: (l, 0, j))],
        out_specs=pl.BlockSpec((1, COND_ROWS, tn), lambda l, j: (l, 0, j)),
        out_shape=jax.ShapeDtypeStruct((depth, COND_ROWS, n), F32),
        compiler_params=_params(("parallel", "parallel")),
        name="modulation",
    )(cond_t, w_mod, b_mod.reshape(depth, 1, n))


def _ctx_attention_rows(q, k, v, seq):
    scale = HEAD_DIM ** -0.5
    nseq = q.shape[0] // seq
    qt = (q * scale).T.astype(BF16)
    vt = v.T.astype(BF16)
    kb = k.astype(BF16)
    probs = [(slice(i * seq, (i + 1) * seq), slice(h * HEAD_DIM, (h + 1) * HEAD_DIM))
             for i in range(nseq) for h in range(NA_HEADS)]
    sts = [_dot(kb[rows, sl], qt[sl, rows]) for rows, sl in probs]
    pts = [jnp.exp(st - jnp.max(st, axis=0, keepdims=True)) for st in sts]
    dens = [jnp.sum(pt, axis=0, keepdims=True) for pt in pts]
    outs = [_dot(vt[sl, rows], pt.astype(BF16)) / den for (rows, sl), pt, den in zip(probs, pts, dens)]
    per_seq = [jnp.concatenate(outs[i * NA_HEADS:(i + 1) * NA_HEADS], axis=0).T for i in range(nseq)]
    return jnp.concatenate(per_seq, axis=0)


def _inproj_kernel(x_ref, g_ref, sh_ref, sc_ref, w_ref, *o_refs, ctx_seq):
    h = _norm_mod(x_ref[...], g_ref[...], sh_ref[0], sc_ref[0]).astype(BF16)
    z = _dot(h, w_ref[...])
    q, k, v, u = (z[:, i * NA_WIDTH:(i + 1) * NA_WIDTH] for i in range(4))
    first = q if ctx_seq is None else _ctx_attention_rows(q, k, v, ctx_seq)
    for o, val in zip(o_refs, (first, k, v, u)):
        o[...] = val.astype(o.dtype)


def _inproj(x, g, mod, layer, row_fn, w, tm, ctx_seq=None):
    t = x.shape[0]
    flat = pl.BlockSpec((tm, NA_WIDTH), lambda i: (i, 0))
    out_specs = [flat] * 4
    out_shape = [jax.ShapeDtypeStruct((t, NA_WIDTH), BF16)] + [jax.ShapeDtypeStruct((t, NA_WIDTH), F32)] * 3
    return pl.pallas_call(
        functools.partial(_inproj_kernel, ctx_seq=ctx_seq),
        grid=(t // tm,),
        in_specs=[pl.BlockSpec((tm, D_MODEL), lambda i: (i, 0)),
                  _full_spec((1, D_MODEL)),
                  _mod_spec(layer, 0, row_fn), _mod_spec(layer, 1, row_fn),
                  _resident_spec(w.shape)],
        out_specs=out_specs,
        out_shape=out_shape,
        compiler_params=_params(("parallel",)),
        name="inproj_even",
    )(x, g, mod, mod, w)


def _na_bias_kernel(rpb_ref, o_ref, *, rows):
    h = pl.program_id(0)
    kr_win = min(WIN_R, rows)
    qc = lax.broadcasted_iota(jnp.int32, (GRID_W, 2 * GRID_W), 0)
    lane = lax.broadcasted_iota(jnp.int32, (GRID_W, 2 * GRID_W), 1)
    kc = lane & (GRID_W - 1)
    cs = jnp.clip(qc - WIN_C // 2, 0, GRID_W - WIN_C)
    col_ok = (kc >= cs) & (kc < cs + WIN_C)
    dc = kc - qc + (WIN_C - 1)
    left = lane < GRID_W
    neg = jnp.full((GRID_W, 2 * GRID_W), NEG, F32)
    toeplitz = []
    for dr in range(2 * WIN_R - 1):
        t = neg
        for d in range(2 * WIN_C - 1):
            t = jnp.where(dc == d, rpb_ref[h, dr, d], t)
        toeplitz.append(jnp.where(col_ok, t, neg))
    for r in range(rows):
        rs = min(max(r - kr_win // 2, 0), rows - kr_win)

        def blk(kr):
            return toeplitz[kr - r + WIN_R - 1] if rs <= kr < rs + kr_win else None

        for j in range(rows // 2):
            a, b = blk(2 * j), blk(2 * j + 1)
            if a is None and b is None:
                pair = neg
            else:
                pair = jnp.where(left, neg if a is None else a, neg if b is None else b)
            o_ref[0, r * GRID_W:(r + 1) * GRID_W, j * 2 * GRID_W:(j + 1) * 2 * GRID_W] = pair


def _na_bias(rpb, seq):
    rows = seq // GRID_W
    return pl.pallas_call(
        functools.partial(_na_bias_kernel, rows=rows),
        grid=(NA_HEADS,),
        in_specs=[pl.BlockSpec(memory_space=pltpu.SMEM)],
        out_specs=pl.BlockSpec((1, seq, seq), lambda h: (h, 0, 0)),
        out_shape=jax.ShapeDtypeStruct((NA_HEADS, seq, seq), F32),
        compiler_params=_params(("parallel",)),
        name="na_bias",
    )(rpb)


def _na_key_range(qb, tq, rows):
    kr_win = min(WIN_R, rows)
    r0, r1 = qb * tq // GRID_W, ((qb + 1) * tq - 1) // GRID_W
    rs0 = min(max(r0 - kr_win // 2, 0), rows - kr_win)
    rs1 = min(max(r1 - kr_win // 2, 0), rows - kr_win)
    lo, hi = rs0 * GRID_W, (rs1 + kr_win) * GRID_W
    return lo // 128 * 128, -(-hi // 128) * 128


def _na_attn_kernel(q_ref, k_ref, v_ref, ck_ref, cv_ref, bias_ref, o_ref, *, tq):
    scale = HEAD_DIM ** -0.5
    seq = q_ref.shape[1]
    probs = []
    for hl in range(2):
        sl = slice(hl * HEAD_DIM, (hl + 1) * HEAD_DIM)
        for qb in range(seq // tq):
            lo, hi = _na_key_range(qb, tq, seq // GRID_W)
            probs.append((hl, sl, slice(qb * tq, (qb + 1) * tq), slice(lo, hi)))
    head0 = pl.program_id(0) * 2
    cks = [ck_ref[0, 0, :, head0 + hl, :].astype(BF16) for hl in range(2)]
    cvs = [cv_ref[0, 0, :, head0 + hl, :].astype(BF16) for hl in range(2)]
    qs_all = [(q_ref[0, qs, sl] * scale).astype(BF16) for _, sl, qs, _ in probs]
    s_loc = [_dot_nt(q, k_ref[0, ks, sl].astype(BF16)) + bias_ref[hl, qs, ks]
             for q, (hl, sl, qs, ks) in zip(qs_all, probs)]
    s_ctx = [_dot_nt(q, cks[hl]) for q, (hl, _, _, _) in zip(qs_all, probs)]
    ms = [jnp.maximum(jnp.max(a, axis=-1, keepdims=True), jnp.max(b, axis=-1, keepdims=True))
          for a, b in zip(s_loc, s_ctx)]
    p_loc = [jnp.exp(a - m) for a, m in zip(s_loc, ms)]
    p_ctx = [jnp.exp(b - m) for b, m in zip(s_ctx, ms)]
    dens = [jnp.sum(a, axis=-1, keepdims=True) + jnp.sum(b, axis=-1, keepdims=True) for a, b in zip(p_loc, p_ctx)]
    for a, b, den, (hl, sl, qs, ks) in zip(p_loc, p_ctx, dens, probs):
        o = _dot(a.astype(BF16), v_ref[0, ks, sl].astype(BF16)) + _dot(b.astype(BF16), cvs[hl])
        o_ref[0, qs, sl] = o / den


def _na_attention(q, k, v, ck, cv, layer, bias):
    nb, seq, _ = q.shape
    lat = pl.BlockSpec((1, seq, 2 * HEAD_DIM), lambda hp, b: (b, 0, hp))
    ctx = pl.BlockSpec((1, 1) + ck.shape[2:], lambda hp, b: (b, layer, 0, 0, 0))
    return pl.pallas_call(
        functools.partial(_na_attn_kernel, tq=256),
        grid=(NA_HEADS // 2, nb),
        in_specs=[lat, lat, lat, ctx, ctx, pl.BlockSpec((2, seq, seq), lambda hp, b: (hp, 0, 0))],
        out_specs=lat,
        out_shape=jax.ShapeDtypeStruct((nb, seq, NA_WIDTH), F32),
        compiler_params=_params(("parallel", "arbitrary")),
        name="na_attention",
    )(q, k, v, ck, cv, bias)


def _s5_prep_group(i, lamr_ref, lami_ref, logdt_ref, btr_ref, bti_ref, cr_ref, ci_ref,
                   m_ref, wz_ref, wyt_ref, ar_ref, ai_ref):
    lr = jnp.minimum(lamr_ref[i], -1e-4)
    li = lami_ref[i]
    dt = jnp.exp(logdt_ref[i])
    a = lr * dt
    th = li * dt
    mag = jnp.exp(a)
    lbr = mag * jnp.cos(th)
    lbi = mag * jnp.sin(th)
    den = lr * lr + li * li
    nr = lbr - 1.0
    coef_r = (nr * lr + lbi * li) / den
    coef_i = (lbi * lr - nr * li) / den
    btr, bti = btr_ref[i], bti_ref[i]
    bbr = coef_r * btr - coef_i * bti
    bbi = coef_r * bti + coef_i * btr
    cr, ci = cr_ref[i], ci_ref[i]

    step = lax.broadcasted_iota(jnp.int32, (CHUNK, 2 * S5_STATE), 0).astype(F32)
    fwd = lax.broadcasted_iota(jnp.int32, (CHUNK, 2 * S5_STATE), 1) < S5_STATE

    def powers(e):
        pm = jnp.exp(a * e)
        return pm * jnp.cos(th * e), pm * jnp.sin(th * e)

    pzr, pzi = powers(jnp.where(fwd, CHUNK - 1 - step, step))
    pyr, pyi = powers(jnp.where(fwd, step + 1.0, CHUNK - step))
    p16r, p16i = powers(jnp.full((1, 2 * S5_STATE), float(CHUNK), F32))
    ar_ref[i] = p16r
    ai_ref[i] = p16i

    def outer(p, c):
        return (p[:, None, :] * c[None, :, :]).reshape(CW, 2 * S5_STATE)

    wzr = outer(pzr, bbr) - outer(pzi, bbi)
    wzi = outer(pzr, bbi) + outer(pzi, bbr)
    wz_ref[i, :, 0:128] = wzr.astype(wz_ref.dtype)
    wz_ref[i, :, 128:256] = wzi.astype(wz_ref.dtype)
    wyt_ref[i, :, 0:128] = (outer(pyr, cr) - outer(pyi, ci)).astype(wyt_ref.dtype)
    wyt_ref[i, :, 128:256] = (-(outer(pyi, cr) + outer(pyr, ci))).astype(wyt_ref.dtype)

    ones = jnp.ones((CHUNK, 2 * S5_STATE), F32)
    cwr = outer(ones, cr)
    cwi = outer(ones, ci)
    fwd_rows = lax.broadcasted_iota(jnp.int32, (CW, 2 * S5_STATE), 1) < S5_STATE
    zero = jnp.zeros((CW, 2 * S5_STATE), F32)

    def split(x):
        hi = x.astype(BF16)
        return hi, (x - hi.astype(F32)).astype(BF16)

    cw_hi, cw_lo = split(jnp.concatenate([cwr, -cwi], axis=1))

    def kk(sel):
        lhs_hi, lhs_lo = split(jnp.concatenate([jnp.where(sel, wzr, zero), jnp.where(sel, wzi, zero)], axis=1))
        return _dot_nt(lhs_hi, cw_hi) + (_dot_nt(lhs_hi, cw_lo) + _dot_nt(lhs_lo, cw_hi))

    kk_f = kk(fwd_rows)
    kk_b = kk(jnp.logical_not(fwd_rows))
    col_step = lax.broadcasted_iota(jnp.int32, (CW, CW), 1) // S5_GROUP
    m = jnp.zeros((CW, CW), F32)
    for t in range(CHUNK):
        up = (CHUNK - 1 - t) * S5_GROUP
        dn = t * S5_GROUP
        piece_f = kk_f if up == 0 else jnp.concatenate([kk_f[up:], jnp.zeros((up, CW), F32)], axis=0)
        piece_b = kk_b if dn == 0 else jnp.concatenate([jnp.zeros((dn, CW), F32), kk_b[:CW - dn]], axis=0)
        m = jnp.where(col_step == t, piece_f + piece_b, m)
    m_ref[i] = m.astype(m_ref.dtype)


def _s5_prep_kernel(*refs):
    for i in range(refs[0].shape[0]):
        _s5_prep_group(i, *refs)


def _s5_prep(lam_re, lam_im, log_step, b_re, b_im, c_re, c_im):
    g = S5_GROUPS
    lanes = 2 * S5_STATE

    def dirs_last(x):
        return x.transpose(1, 0, 2).reshape(g, 1, lanes)

    lamr, lami = dirs_last(lam_re), dirs_last(lam_im)
    logdt = dirs_last(jnp.broadcast_to(log_step[:, :, None], (2, g, S5_STATE)))
    bt = [x.transpose(1, 3, 0, 2).reshape(g, S5_GROUP, lanes) for x in (b_re, b_im)]
    ct = [x.transpose(1, 2, 0, 3).reshape(g, S5_GROUP, lanes) for x in (c_re, c_im)]
    gp = 4
    vec = pl.BlockSpec((gp, 1, lanes), lambda i: (i, 0, 0))
    mat = pl.BlockSpec((gp, S5_GROUP, lanes), lambda i: (i, 0, 0))
    big = pl.BlockSpec((gp, CW, CW), lambda i: (i, 0, 0))
    return pl.pallas_call(
        _s5_prep_kernel,
        grid=(g // gp,),
        in_specs=[vec, vec, vec, mat, mat, mat, mat],
        out_specs=[big, big, big, vec, vec],
        out_shape=[jax.ShapeDtypeStruct((g, CW, CW), BF16)] * 3 + [jax.ShapeDtypeStruct((g, 1, lanes), F32)] * 2,
        compiler_params=_params(("parallel",)),
        name="s5_prep",
    )(lamr, lami, logdt, bt[0], bt[1], ct[0], ct[1])


def _s5_kernel(u_ref, m_ref, wz_ref, wyt_ref, ar_ref, ai_ref, s0_ref, perm_ref, permt_ref, y_ref, fin_ref,
               xt_scr, z_scr, sp_scr, *, nbp, nc):
    s = S5_STATE
    r = xt_scr.shape[2]
    for t in range(CHUNK):
        xt_scr[t] = u_ref[pl.ds(t, r, stride=CHUNK), :].T
    fwd = lax.broadcasted_iota(jnp.int32, (nbp, 2 * s), 1) < s
    groups = range(GROUPS_PER_BLOCK)
    chans = [slice(gl * S5_GROUP, (gl + 1) * S5_GROUP) for gl in groups]

    us = [jnp.concatenate([xt_scr[t, ch, :] for t in range(CHUNK)], axis=0).T.astype(BF16) for ch in chans]
    for gl in groups:
        z_scr[gl] = _dot(_dot(perm_ref[...], us[gl]).astype(BF16), wz_ref[gl])

    ars = [jnp.broadcast_to(ar_ref[gl], (nbp, 2 * s)) for gl in groups]
    ais = [jnp.broadcast_to(ai_ref[gl], (nbp, 2 * s)) for gl in groups]
    srs = [s0_ref[gl, :, 0:2 * s] for gl in groups]
    sis = [s0_ref[gl, :, 2 * s:4 * s] for gl in groups]
    for k in range(nc):
        up = slice(k * nbp, (k + 1) * nbp)
        dn = slice((nc - 1 - k) * nbp, (nc - k) * nbp)
        for gl in groups:
            sr, si = srs[gl], sis[gl]
            sp_scr[gl, up, 0:s] = sr[:, 0:s]
            sp_scr[gl, dn, s:2 * s] = sr[:, s:2 * s]
            sp_scr[gl, up, 2 * s:3 * s] = si[:, 0:s]
            sp_scr[gl, dn, 3 * s:4 * s] = si[:, s:2 * s]
            zr = jnp.where(fwd, z_scr[gl, up, 0:2 * s], z_scr[gl, dn, 0:2 * s])
            zi = jnp.where(fwd, z_scr[gl, up, 2 * s:4 * s], z_scr[gl, dn, 2 * s:4 * s])
            srs[gl] = ars[gl] * sr - ais[gl] * si + zr
            sis[gl] = ars[gl] * si + ais[gl] * sr + zi
    for gl in groups:
        fin_ref[gl, :, 0:2 * s] = srs[gl]
        fin_ref[gl, :, 2 * s:4 * s] = sis[gl]
    sps = [_dot(permt_ref[...], sp_scr[gl].astype(BF16)).astype(BF16) for gl in groups]
    for gl in groups:
        yt = (_dot(us[gl], m_ref[gl]) + _dot_nt(sps[gl], wyt_ref[gl])).T
        for t in range(CHUNK):
            xt_scr[t, chans[gl], :] = yt[t * S5_GROUP:(t + 1) * S5_GROUP, :]

    for t in range(CHUNK):
        y_ref[pl.ds(t, r, stride=CHUNK), :] = xt_scr[t].T


def _s5_mixer(u, s0, ops, nb, seq):
    m, wz, wyt, ar, ai = ops
    g, nc = S5_GROUPS, seq // CHUNK
    nbp = -(-nb // 8) * 8
    r, rp = nb * nc, nc * nbp
    gpb = GROUPS_PER_BLOCK
    s0l = jnp.zeros((g, nbp, CW), F32)
    if s0 is not None:
        s0l = s0l.at[:, :nb].set(s0.transpose(3, 0, 2, 1, 4).reshape(g, nb, CW))
    perm = np.zeros((nc, nbp, nb, nc), np.float32)
    for b in range(nb):
        perm[np.arange(nc), b, b, np.arange(nc)] = 1.0
    perm = jnp.asarray(perm.reshape(rp, r)).astype(BF16)
    col = pl.BlockSpec((nb * seq, gpb * S5_GROUP), lambda j: (0, j))
    wspec = pl.BlockSpec((gpb, CW, CW), lambda j: (j, 0, 0))
    cspec = pl.BlockSpec((gpb, 1, 2 * S5_STATE), lambda j: (j, 0, 0))
    sspec = pl.BlockSpec((gpb, nbp, CW), lambda j: (j, 0, 0))
    y, fin = pl.pallas_call(
        functools.partial(_s5_kernel, nbp=nbp, nc=nc),
        grid=(g // gpb,),
        in_specs=[col, wspec, wspec, wspec, cspec, cspec, sspec, _full_spec((rp, r)), _full_spec((r, rp))],
        out_specs=[col, sspec],
        out_shape=[jax.ShapeDtypeStruct((nb * seq, S5_WIDTH), F32), jax.ShapeDtypeStruct((g, nbp, CW), F32)],
        scratch_shapes=[pltpu.VMEM((CHUNK, gpb * S5_GROUP, r), F32),
                        pltpu.VMEM((gpb, rp, CW), F32), pltpu.VMEM((gpb, rp, CW), F32)],
        compiler_params=_params(("parallel",)),
        name="s5_scan",
    )(u, m, wz, wyt, ar, ai, s0l, perm, perm.T)
    state = fin[:, :nb].reshape(g, nb, 2, 2, S5_STATE).transpose(1, 3, 2, 0, 4)
    return y, state


FFN_ROWS = 512
FFN_CHUNK = 256


def _ffn_rows(x, g, shift, scale, gate, wg_ref, wu_ref, wd_ref):
    h = _norm_mod(x, g, shift, scale).astype(BF16)
    acc = jnp.zeros(x.shape, F32)
    for j in range(D_FF // FFN_CHUNK):
        sl = slice(j * FFN_CHUNK, (j + 1) * FFN_CHUNK)
        gg = _dot(h, wg_ref[:, sl])
        uu = _dot(h, wu_ref[:, sl])
        act = (gg * jax.nn.sigmoid(gg) * uu).astype(BF16)
        acc = acc + _dot(act, wd_ref[sl, :])
    return x + gate * acc


def _layer_weight_spec(w, layer):
    return pl.BlockSpec((None,) + w.shape[1:], lambda *idx: (layer, 0, 0), pipeline_mode=pl.Buffered(1))


def _ffn_specs(layer, row_fn, wg, wu, wd):
    return [_full_spec((1, D_MODEL)),
            _mod_spec(layer, 3, row_fn), _mod_spec(layer, 4, row_fn), _mod_spec(layer, 5, row_fn),
            _layer_weight_spec(wg, layer), _layer_weight_spec(wu, layer), _layer_weight_spec(wd, layer)]


def _even_tail_kernel(a_ref, y_ref, u_ref, x_ref, d_ref, gw_ref, gb_ref, wo_ref, gate_ref,
                      g2_ref, sh2_ref, sc2_ref, gate2_ref, wg_ref, wu_ref, wd_ref, o_ref):
    y = jax.nn.gelu(y_ref[...] + d_ref[...] * u_ref[...])
    y = y * jax.nn.sigmoid(_dot(y.astype(BF16), gw_ref[...]) + gb_ref[...])
    o = _dot(a_ref[...].astype(BF16), wo_ref[0:NA_WIDTH, :]) + _dot(y.astype(BF16), wo_ref[NA_WIDTH:, :])
    x = x_ref[...] + gate_ref[0] * o
    o_ref[...] = _ffn_rows(x, g2_ref[...], sh2_ref[0], sc2_ref[0], gate2_ref[0], wg_ref, wu_ref, wd_ref)


def _even_tail(a, y, u, x, d, glu_w, glu_b, w_out, g_ffn, wg, wu, wd, mod, layer, row_fn):
    t = x.shape[0]
    tm = FFN_ROWS
    half = pl.BlockSpec((tm, NA_WIDTH), lambda i: (i, 0))
    full = pl.BlockSpec((tm, D_MODEL), lambda i: (i, 0))
    return pl.pallas_call(
        _even_tail_kernel,
        grid=(t // tm,),
        in_specs=[half, half, half, full, _full_spec((1, S5_WIDTH)), _resident_spec(glu_w.shape),
                  _full_spec((1, S5_WIDTH)), _resident_spec(w_out.shape), _mod_spec(layer, 2, row_fn)]
                 + _ffn_specs(layer, row_fn, wg, wu, wd),
        out_specs=full,
        out_shape=jax.ShapeDtypeStruct((t, D_MODEL), F32),
        compiler_params=_params(("parallel",)),
        name="even_tail",
    )(a, y, u, x, d, glu_w, glu_b, w_out, mod, g_ffn, mod, mod, mod, wg, wu, wd)


def _dft_consts(seq):
    def cs(n):
        k = np.arange(n)
        ang = 2.0 * np.pi * ((k[:, None] * k[None, :]) % n) / n
        return np.cos(ang), np.sin(ang)

    cc, sc = cs(FNET_GROUP_WIDTH)
    cl, sl = cs(seq)
    chan = jnp.asarray(np.concatenate([cc, sc], axis=1), dtype=F32)
    pos = jnp.asarray(np.concatenate([cl, -sl], axis=1), dtype=F32)
    return chan.astype(BF16), pos.astype(BF16)


def _fourier_rows(x, g, shift, scale, gate, win_ref, chan_ref, pos_ref, wout_ref):
    seq = x.shape[0]
    h = _norm_mod(x, g, shift, scale).astype(BF16)
    z = _dot(h, win_ref[...]).astype(BF16)
    w = FNET_GROUP_WIDTH
    pos = pos_ref[...]
    parts = []
    for c in range(FNET_GROUPS):
        ab = _dot(z[:, c * w:(c + 1) * w], chan_ref[...]).astype(BF16)
        stacked = jnp.concatenate([ab[:, :w], ab[:, w:]], axis=0)
        parts.append(_dot(pos, stacked))
    f = jnp.concatenate(parts, axis=1) * (1.0 / math.sqrt(seq * w))
    return x + gate * _dot(f.astype(BF16), wout_ref[...])


def _odd_layer_kernel(x_ref, g_ref, sh_ref, sc_ref, win_ref, chan_ref, pos_ref, wout_ref, gate_ref,
                      g2_ref, sh2_ref, sc2_ref, gate2_ref, wg_ref, wu_ref, wd_ref, fg_ref, o_ref, *, seq):
    tm = x_ref.shape[0]
    for r0 in range(0, tm, seq):
        rows = slice(r0, r0 + seq)
        o_ref[rows, :] = _fourier_rows(x_ref[rows, :], g_ref[...], sh_ref[0], sc_ref[0], gate_ref[0],
                                       win_ref, chan_ref, pos_ref, wout_ref)
    for r0 in range(0, tm, FFN_ROWS):
        rows = slice(r0, r0 + FFN_ROWS)
        y = _ffn_rows(o_ref[rows, :], g2_ref[...], sh2_ref[0], sc2_ref[0], gate2_ref[0], wg_ref, wu_ref, wd_ref)
        ms = jnp.mean(y * y, axis=-1, keepdims=True)
        o_ref[rows, :] = y * lax.rsqrt(ms + EPS) * fg_ref[...]


def _odd_layer(x, g_mix, g_ffn, final_g, mod, layer, row_of, w_in, w_out, wg, wu, wd, seq):
    t = x.shape[0]
    tm = max(seq, FFN_ROWS)
    row_fn = row_of(tm)
    chan, pos = _dft_consts(seq)
    blk = pl.BlockSpec((tm, D_MODEL), lambda i: (i, 0))
    return pl.pallas_call(
        functools.partial(_odd_layer_kernel, seq=seq),
        grid=(t // tm,),
        in_specs=[blk, _full_spec((1, D_MODEL)), _mod_spec(layer, 0, row_fn), _mod_spec(layer, 1, row_fn),
                  _resident_spec(w_in.shape), _resident_spec(chan.shape), _resident_spec(pos.shape),
                  _resident_spec(w_out.shape), _mod_spec(layer, 2, row_fn)]
                 + _ffn_specs(layer, row_fn, wg, wu, wd) + [_full_spec((1, D_MODEL))],
        out_specs=blk,
        out_shape=jax.ShapeDtypeStruct((t, D_MODEL), F32),
        compiler_params=_params(("parallel",)),
        name="odd_layer",
    )(x, g_mix, mod, mod, w_in, chan, pos, w_out, mod, g_ffn, mod, mod, mod, wg, wu, wd, final_g)


def kernel(x_prompt, x_sample, cache_na_k, cache_na_v, state_s5, c, c_ctx, w_mod, b_mod, norm_mix_g, norm_ffn_g, w_in_even, na_rpb, s5_lam_re, s5_lam_im, s5_log_step, s5_b_re, s5_b_im, s5_c_re, s5_c_im, s5_d, s5_glu_w, s5_glu_b, w_out_even, w_in_odd, w_out_odd, ffn_w_gate, ffn_w_up, ffn_w_down, final_norm_g):
    bp, lp, _ = x_prompt.shape
    bs, ls, _ = x_sample.shape
    depth = w_mod.shape[0]
    assert bs + 1 <= N_COND and depth == 2

    cond = jnp.zeros((COND_ROWS, D_MODEL), F32).at[0].set(c_ctx).at[1:1 + bs].set(c)
    mod = _modulation(cond.T, w_mod, b_mod).reshape(depth * COND_ROWS * 6, 1, D_MODEL)

    streams = [
        dict(x=x_prompt.reshape(bp * lp, D_MODEL), nb=bp, seq=lp, row_of=lambda tm: (lambda i: 0)),
        dict(x=x_sample.reshape(bs * ls, D_MODEL), nb=bs, seq=ls, row_of=lambda tm: (lambda i: 1 + (i * tm) // ls)),
    ]
    tm = FFN_ROWS

    e = 0
    w_in = w_in_even[e].astype(BF16)
    w_out = w_out_even[e].astype(BF16)
    glu_w = s5_glu_w[e].astype(BF16)
    glu_b = s5_glu_b[e].reshape(1, S5_WIDTH)
    g_mix = norm_mix_g[0].reshape(1, D_MODEL)
    g_ffn0 = norm_ffn_g[0].reshape(1, D_MODEL)
    ffn_w = tuple(w.astype(BF16) for w in (ffn_w_gate, ffn_w_up, ffn_w_down))
    s5_ops = _s5_prep(s5_lam_re[e], s5_lam_im[e], s5_log_step[e], s5_b_re[e], s5_b_im[e], s5_c_re[e], s5_c_im[e])
    s5_dvec = s5_d[e].reshape(1, S5_WIDTH)
    bias = _na_bias(na_rpb[e], ls)

    new_k = new_v = new_s = None
    for si, st in enumerate(streams):
        nb, seq = st["nb"], st["seq"]
        if si == 0:
            a, k, v, u = _inproj(st["x"], g_mix, mod, 0, st["row_of"](tm), w_in, tm, ctx_seq=seq)
            s0 = None
            new_k = k.reshape(nb, 1, seq, NA_HEADS, HEAD_DIM)
            new_v = v.reshape(nb, 1, seq, NA_HEADS, HEAD_DIM)
        else:
            q, k, v, u = _inproj(st["x"], g_mix, mod, 0, st["row_of"](tm), w_in, tm)
            a = _na_attention(q.reshape(nb, seq, NA_WIDTH), k.reshape(nb, seq, NA_WIDTH), v.reshape(nb, seq, NA_WIDTH),
                              cache_na_k, cache_na_v, e, bias).reshape(nb * seq, NA_WIDTH)
            s0 = state_s5[:, e].astype(F32)
        y, state = _s5_mixer(u, s0, s5_ops, nb, seq)
        if si == 0:
            new_s = state[:, None]
        st["x"] = _even_tail(a, y, u, st["x"], s5_dvec, glu_w, glu_b, w_out, g_ffn0, *ffn_w,
                             mod, 0, st["row_of"](tm))

    for st in streams:
        st["x"] = _odd_layer(st["x"], norm_mix_g[1].reshape(1, D_MODEL), norm_ffn_g[1].reshape(1, D_MODEL),
                             final_norm_g.reshape(1, D_MODEL), mod, 1, st["row_of"],
                             w_in_odd[0].astype(BF16), w_out_odd[0].astype(BF16), *ffn_w, st["seq"])

    y_prompt = streams[0]["x"].reshape(bp, lp, D_MODEL)
    y_sample = streams[1]["x"].reshape(bs, ls, D_MODEL)
    return (y_prompt, y_sample, new_k, new_v, new_s)
```

```python
import functools
import math

import numpy as np
import jax
import jax.numpy as jnp
from jax import lax
from jax.experimental import pallas as pl
from jax.experimental.pallas import tpu as pltpu

F32 = jnp.float32
BF16 = jnp.bfloat16

D_MODEL = 1024
NA_HEADS = 8
HEAD_DIM = 64
NA_WIDTH = NA_HEADS * HEAD_DIM
GRID_W = 64
WIN_R = 8
WIN_C = 16
S5_GROUP = 16
S5_GROUPS = 32
S5_STATE = 64
S5_WIDTH = S5_GROUPS * S5_GROUP
FNET_GROUPS = 4
FNET_GROUP_WIDTH = D_MODEL // FNET_GROUPS
D_FF = 2816
EPS = 1e-6

CHUNK = 16
CW = CHUNK * S5_GROUP
GROUPS_PER_BLOCK = 128 // S5_GROUP
N_COND = 3
COND_ROWS = 8
NEG = -1e30
VMEM_LIMIT = 56 * 2 ** 20

_NT = (((1,), (1,)), ((), ()))


def _params(sem, vmem=VMEM_LIMIT):
    return pltpu.CompilerParams(dimension_semantics=sem, vmem_limit_bytes=vmem)


def _dot(a, b):
    return jnp.dot(a, b, preferred_element_type=F32)


def _dot_nt(a, b):
    return lax.dot_general(a, b, _NT, preferred_element_type=F32)


def _norm_mod(x, g, shift, scale):
    ms = jnp.mean(x * x, axis=-1, keepdims=True)
    return (x * lax.rsqrt(ms + EPS) * g) * (1.0 + scale) + shift


def _mod_spec(layer, which, row_fn):
    return pl.BlockSpec((1, 1, D_MODEL),
                        lambda *idx: (layer * COND_ROWS * 6 + row_fn(*idx) * 6 + which, 0, 0))


def _full_spec(shape):
    nd = len(shape)
    return pl.BlockSpec(shape, lambda *idx: (0,) * nd)


def _resident_spec(shape):
    nd = len(shape)
    return pl.BlockSpec(shape, lambda *idx: (0,) * nd, pipeline_mode=pl.Buffered(1))


def _mod_kernel(ct_ref, w_ref, b_ref, o_ref):
    ct = ct_ref[...]
    st = ct * jax.nn.sigmoid(ct)
    w = w_ref[0]
    o_ref[0] = jnp.zeros(o_ref.shape[1:], F32)
    for r in range(N_COND):
        o_ref[0, r:r + 1, :] = jnp.sum(w * st[:, r:r + 1], axis=0, keepdims=True) + b_ref[0]


def _modulation(cond_t, w_mod, b_mod):
    depth, _, n = w_mod.shape
    tn = 1024
    return pl.pallas_call(
        _mod_kernel,
        grid=(depth, n // tn),
        in_specs=[_full_spec((D_MODEL, COND_ROWS)),
                  pl.BlockSpec((1, D_MODEL, tn), lambda l, j: (l, 0, j)),
                  pl.BlockSpec((1, 1, tn), lambda l, j: (l, 0, j))],
        out_specs=pl.BlockSpec((1, COND_ROWS, tn), lambda l, j: (l, 0, j)),
        out_shape=jax.ShapeDtypeStruct((depth, COND_ROWS, n), F32),
        compiler_params=_params(("parallel", "parallel")),
        name="modulation",
    )(cond_t, w_mod, b_mod.reshape(depth, 1, n))


def _ctx_attention_rows(q, k, v, seq):
    scale = HEAD_DIM ** -0.5
    nseq = q.shape[0] // seq
    qt = (q * scale).T.astype(BF16)
    vt = v.T.astype(BF16)
    kb = k.astype(BF16)
    probs = [(slice(i * seq, (i + 1) * seq), slice(h * HEAD_DIM, (h + 1) * HEAD_DIM))
             for i in range(nseq) for h in range(NA_HEADS)]
    sts = [_dot(kb[rows, sl], qt[sl, rows]) for rows, sl in probs]
    pts = [jnp.exp(st - jnp.max(st, axis=0, keepdims=True)) for st in sts]
    dens = [jnp.sum(pt, axis=0, keepdims=True) for pt in pts]
    outs = [_dot(vt[sl, rows], pt.astype(BF16)) / den for (rows, sl), pt, den in zip(probs, pts, dens)]
    per_seq = [jnp.concatenate(outs[i * NA_HEADS:(i + 1) * NA_HEADS], axis=0).T for i in range(nseq)]
    return jnp.concatenate(per_seq, axis=0)


def _inproj_kernel(x_ref, g_ref, sh_ref, sc_ref, w_ref, *refs, ctx_seq):
    h = _norm_mod(x_ref[...], g_ref[...], sh_ref[0], sc_ref[0]).astype(BF16)
    z = _dot(h, w_ref[...])
    q, k, v, u = (z[:, i * NA_WIDTH:(i + 1) * NA_WIDTH] for i in range(4))
    if ctx_seq is None:
        for o, val in zip(refs, (q, k, v, u)):
            o[...] = val.astype(o.dtype)
        return
    a_ref, u_ref, kc_ref, vc_ref, rows_scr = refs
    a_ref[...] = _ctx_attention_rows(q, k, v, ctx_seq).astype(a_ref.dtype)
    u_ref[...] = u
    tm = z.shape[0]
    pair = 2 * HEAD_DIM
    for o, val in ((kc_ref, k), (vc_ref, v)):
        for j in range(NA_HEADS // 2):
            both = val[:, j * pair:(j + 1) * pair]
            rows_scr[pl.ds(2 * j, tm, stride=NA_HEADS), :] = both
            rows_scr[pl.ds(2 * j + 1, tm, stride=NA_HEADS), :] = pltpu.roll(both, HEAD_DIM, axis=1)
        for b in range(o.shape[0]):
            blk = rows_scr[b * ctx_seq * NA_HEADS:(b + 1) * ctx_seq * NA_HEADS, 0:HEAD_DIM]
            o[b, 0] = blk.reshape(ctx_seq, NA_HEADS, HEAD_DIM)


def _inproj(x, g, mod, layer, row_fn, w, tm, ctx_seq=None):
    t = x.shape[0]
    flat = pl.BlockSpec((tm, NA_WIDTH), lambda i: (i, 0))
    scratch = []
    if ctx_seq is None:
        out_specs = [flat] * 4
        out_shape = [jax.ShapeDtypeStruct((t, NA_WIDTH), BF16)] + [jax.ShapeDtypeStruct((t, NA_WIDTH), F32)] * 3
    else:
        per = tm // ctx_seq
        cache = pl.BlockSpec((per, 1, ctx_seq, NA_HEADS, HEAD_DIM), lambda i: (i, 0, 0, 0, 0))
        out_specs = [flat, flat, cache, cache]
        out_shape = ([jax.ShapeDtypeStruct((t, NA_WIDTH), BF16), jax.ShapeDtypeStruct((t, NA_WIDTH), F32)]
                     + [jax.ShapeDtypeStruct((t // ctx_seq, 1, ctx_seq, NA_HEADS, HEAD_DIM), F32)] * 2)
        scratch = [pltpu.VMEM((tm * NA_HEADS, 2 * HEAD_DIM), F32)]
    return pl.pallas_call(
        functools.partial(_inproj_kernel, ctx_seq=ctx_seq),
        grid=(t // tm,),
        in_specs=[pl.BlockSpec((tm, D_MODEL), lambda i: (i, 0)),
                  _full_spec((1, D_MODEL)),
                  _mod_spec(layer, 0, row_fn), _mod_spec(layer, 1, row_fn),
                  _resident_spec(w.shape)],
        out_specs=out_specs,
        out_shape=out_shape,
        scratch_shapes=scratch,
        compiler_params=_params(("parallel",)),
        name="inproj_even",
    )(x, g, mod, mod, w)


def _na_bias_kernel(rpb_ref, o_ref, *, rows):
    h = pl.program_id(0)
    kr_win = min(WIN_R, rows)
    qc = lax.broadcasted_iota(jnp.int32, (GRID_W, 2 * GRID_W), 0)
    lane = lax.broadcasted_iota(jnp.int32, (GRID_W, 2 * GRID_W), 1)
    kc = lane & (GRID_W - 1)
    cs = jnp.clip(qc - WIN_C // 2, 0, GRID_W - WIN_C)
    col_ok = (kc >= cs) & (kc < cs + WIN_C)
    dc = kc - qc + (WIN_C - 1)
    left = lane < GRID_W
    neg = jnp.full((GRID_W, 2 * GRID_W), NEG, F32)
    toeplitz = []
    for dr in range(2 * WIN_R - 1):
        t = neg
        for d in range(2 * WIN_C - 1):
            t = jnp.where(dc == d, rpb_ref[h, dr, d], t)
        toeplitz.append(jnp.where(col_ok, t, neg))
    for r in range(rows):
        rs = min(max(r - kr_win // 2, 0), rows - kr_win)

        def blk(kr):
            return toeplitz[kr - r + WIN_R - 1] if rs <= kr < rs + kr_win else None

        for j in range(rows // 2):
            a, b = blk(2 * j), blk(2 * j + 1)
            if a is None and b is None:
                pair = neg
            else:
                pair = jnp.where(left, neg if a is None else a, neg if b is None else b)
            o_ref[0, r * GRID_W:(r + 1) * GRID_W, j * 2 * GRID_W:(j + 1) * 2 * GRID_W] = pair


def _na_bias(rpb, seq):
    rows = seq // GRID_W
    return pl.pallas_call(
        functools.partial(_na_bias_kernel, rows=rows),
        grid=(NA_HEADS,),
        in_specs=[pl.BlockSpec(memory_space=pltpu.SMEM)],
        out_specs=pl.BlockSpec((1, seq, seq), lambda h: (h, 0, 0)),
        out_shape=jax.ShapeDtypeStruct((NA_HEADS, seq, seq), F32),
        compiler_params=_params(("parallel",)),
        name="na_bias",
    )(rpb)


def _na_key_range(qb, tq, rows):
    kr_win = min(WIN_R, rows)
    r0, r1 = qb * tq // GRID_W, ((qb + 1) * tq - 1) // GRID_W
    rs0 = min(max(r0 - kr_win // 2, 0), rows - kr_win)
    rs1 = min(max(r1 - kr_win // 2, 0), rows - kr_win)
    lo, hi = rs0 * GRID_W, (rs1 + kr_win) * GRID_W
    return lo // 128 * 128, -(-hi // 128) * 128


def _na_attn_kernel(q_ref, k_ref, v_ref, ck_ref, cv_ref, bias_ref, o_ref, *, tq):
    scale = HEAD_DIM ** -0.5
    seq = q_ref.shape[1]
    probs = []
    for hl in range(2):
        sl = slice(hl * HEAD_DIM, (hl + 1) * HEAD_DIM)
        for qb in range(seq // tq):
            lo, hi = _na_key_range(qb, tq, seq // GRID_W)
            probs.append((hl, sl, slice(qb * tq, (qb + 1) * tq), slice(lo, hi)))
    head0 = pl.program_id(0) * 2
    cks = [ck_ref[0, 0, :, head0 + hl, :].astype(BF16) for hl in range(2)]
    cvs = [cv_ref[0, 0, :, head0 + hl, :].astype(BF16) for hl in range(2)]
    qs_all = [(q_ref[0, qs, sl] * scale).astype(BF16) for _, sl, qs, _ in probs]
    s_loc = [_dot_nt(q, k_ref[0, ks, sl].astype(BF16)) + bias_ref[hl, qs, ks]
             for q, (hl, sl, qs, ks) in zip(qs_all, probs)]
    s_ctx = [_dot_nt(q, cks[hl]) for q, (hl, _, _, _) in zip(qs_all, probs)]
    ms = [jnp.maximum(jnp.max(a, axis=-1, keepdims=True), jnp.max(b, axis=-1, keepdims=True))
          for a, b in zip(s_loc, s_ctx)]
    p_loc = [jnp.exp(a - m) for a, m in zip(s_loc, ms)]
    p_ctx = [jnp.exp(b - m) for b, m in zip(s_ctx, ms)]
    dens = [jnp.sum(a, axis=-1, keepdims=True) + jnp.sum(b, axis=-1, keepdims=True) for a, b in zip(p_loc, p_ctx)]
    for a, b, den, (hl, sl, qs, ks) in zip(p_loc, p_ctx, dens, probs):
        o = _dot(a.astype(BF16), v_ref[0, ks, sl].astype(BF16)) + _dot(b.astype(BF16), cvs[hl])
        o_ref[0, qs, sl] = o / den


def _na_attention(q, k, v, ck, cv, layer, bias):
    nb, seq, _ = q.shape
    lat = pl.BlockSpec((1, seq, 2 * HEAD_DIM), lambda hp, b: (b, 0, hp))
    ctx = pl.BlockSpec((1, 1) + ck.shape[2:], lambda hp, b: (b, layer, 0, 0, 0))
    return pl.pallas_call(
        functools.partial(_na_attn_kernel, tq=256),
        grid=(NA_HEADS // 2, nb),
        in_specs=[lat, lat, lat, ctx, ctx, pl.BlockSpec((2, seq, seq), lambda hp, b: (hp, 0, 0))],
        out_specs=lat,
        out_shape=jax.ShapeDtypeStruct((nb, seq, NA_WIDTH), F32),
        compiler_params=_params(("parallel", "arbitrary")),
        name="na_attention",
    )(q, k, v, ck, cv, bias)


def _s5_prep_group(i, lamr_ref, lami_ref, logdt_ref, btr_ref, bti_ref, cr_ref, ci_ref,
                   m_ref, wz_ref, wyt_ref, ar_ref, ai_ref):
    lr = jnp.minimum(lamr_ref[i], -1e-4)
    li = lami_ref[i]
    dt = jnp.exp(logdt_ref[i])
    a = lr * dt
    th = li * dt
    mag = jnp.exp(a)
    lbr = mag * jnp.cos(th)
    lbi = mag * jnp.sin(th)
    den = lr * lr + li * li
    nr = lbr - 1.0
    coef_r = (nr * lr + lbi * li) / den
    coef_i = (lbi * lr - nr * li) / den
    btr, bti = btr_ref[i], bti_ref[i]
    bbr = coef_r * btr - coef_i * bti
    bbi = coef_r * bti + coef_i * btr
    cr, ci = cr_ref[i], ci_ref[i]

    step = lax.broadcasted_iota(jnp.int32, (CHUNK, 2 * S5_STATE), 0).astype(F32)
    fwd = lax.broadcasted_iota(jnp.int32, (CHUNK, 2 * S5_STATE), 1) < S5_STATE

    def powers(e):
        pm = jnp.exp(a * e)
        return pm * jnp.cos(th * e), pm * jnp.sin(th * e)

    pzr, pzi = powers(jnp.where(fwd, CHUNK - 1 - step, step))
    pyr, pyi = powers(jnp.where(fwd, step + 1.0, CHUNK - step))
    p16r, p16i = powers(jnp.full((1, 2 * S5_STATE), float(CHUNK), F32))
    ar_ref[i] = p16r
    ai_ref[i] = p16i

    def outer(p, c):
        return (p[:, None, :] * c[None, :, :]).reshape(CW, 2 * S5_STATE)

    wzr = outer(pzr, bbr) - outer(pzi, bbi)
    wzi = outer(pzr, bbi) + outer(pzi, bbr)
    wz_ref[i, :, 0:128] = wzr.astype(wz_ref.dtype)
    wz_ref[i, :, 128:256] = wzi.astype(wz_ref.dtype)
    wyt_ref[i, :, 0:128] = (outer(pyr, cr) - outer(pyi, ci)).astype(wyt_ref.dtype)
    wyt_ref[i, :, 128:256] = (-(outer(pyi, cr) + outer(pyr, ci))).astype(wyt_ref.dtype)

    ones = jnp.ones((CHUNK, 2 * S5_STATE), F32)
    cwr = outer(ones, cr)
    cwi = outer(ones, ci)
    fwd_rows = lax.broadcasted_iota(jnp.int32, (CW, 2 * S5_STATE), 1) < S5_STATE
    zero = jnp.zeros((CW, 2 * S5_STATE), F32)

    def split(x):
        hi = x.astype(BF16)
        return hi, (x - hi.astype(F32)).astype(BF16)

    cw_hi, cw_lo = split(jnp.concatenate([cwr, -cwi], axis=1))

    def kk(sel):
        lhs_hi, lhs_lo = split(jnp.concatenate([jnp.where(sel, wzr, zero), jnp.where(sel, wzi, zero)], axis=1))
        return _dot_nt(lhs_hi, cw_hi) + (_dot_nt(lhs_hi, cw_lo) + _dot_nt(lhs_lo, cw_hi))

    kk_f = kk(fwd_rows)
    kk_b = kk(jnp.logical_not(fwd_rows))
    col_step = lax.broadcasted_iota(jnp.int32, (CW, CW), 1) // S5_GROUP
    m = jnp.zeros((CW, CW), F32)
    for t in range(CHUNK):
        up = (CHUNK - 1 - t) * S5_GROUP
        dn = t * S5_GROUP
        piece_f = kk_f if up == 0 else jnp.concatenate([kk_f[up:], jnp.zeros((up, CW), F32)], axis=0)
        piece_b = kk_b if dn == 0 else jnp.concatenate([jnp.zeros((dn, CW), F32), kk_b[:CW - dn]], axis=0)
        m = jnp.where(col_step == t, piece_f + piece_b, m)
    m_ref[i] = m.astype(m_ref.dtype)


def _s5_prep_kernel(*refs):
    for i in range(refs[0].shape[0]):
        _s5_prep_group(i, *refs)


def _s5_prep(lam_re, lam_im, log_step, b_re, b_im, c_re, c_im):
    g = S5_GROUPS
    lanes = 2 * S5_STATE

    def dirs_last(x):
        return x.transpose(1, 0, 2).reshape(g, 1, lanes)

    lamr, lami = dirs_last(lam_re), dirs_last(lam_im)
    logdt = dirs_last(jnp.broadcast_to(log_step[:, :, None], (2, g, S5_STATE)))
    bt = [x.transpose(1, 3, 0, 2).reshape(g, S5_GROUP, lanes) for x in (b_re, b_im)]
    ct = [x.transpose(1, 2, 0, 3).reshape(g, S5_GROUP, lanes) for x in (c_re, c_im)]
    gp = 4
    vec = pl.BlockSpec((gp, 1, lanes), lambda i: (i, 0, 0))
    mat = pl.BlockSpec((gp, S5_GROUP, lanes), lambda i: (i, 0, 0))
    big = pl.BlockSpec((gp, CW, CW), lambda i: (i, 0, 0))
    return pl.pallas_call(
        _s5_prep_kernel,
        grid=(g // gp,),
        in_specs=[vec, vec, vec, mat, mat, mat, mat],
        out_specs=[big, big, big, vec, vec],
        out_shape=[jax.ShapeDtypeStruct((g, CW, CW), BF16)] * 3 + [jax.ShapeDtypeStruct((g, 1, lanes), F32)] * 2,
        compiler_params=_params(("parallel",)),
        name="s5_prep",
    )(lamr, lami, logdt, bt[0], bt[1], ct[0], ct[1])


def _s5_kernel(u_ref, m_ref, wz_ref, wyt_ref, ar_ref, ai_ref, s0_ref, perm_ref, permt_ref, y_ref, fin_ref,
               xt_scr, z_scr, sp_scr, *, nbp, nc):
    s = S5_STATE
    r = xt_scr.shape[2]
    for t in range(CHUNK):
        xt_scr[t] = u_ref[pl.ds(t, r, stride=CHUNK), :].T
    fwd = lax.broadcasted_iota(jnp.int32, (nbp, 2 * s), 1) < s
    groups = range(GROUPS_PER_BLOCK)
    chans = [slice(gl * S5_GROUP, (gl + 1) * S5_GROUP) for gl in groups]

    us = [jnp.concatenate([xt_scr[t, ch, :] for t in range(CHUNK)], axis=0).T.astype(BF16) for ch in chans]
    for gl in groups:
        z_scr[gl] = _dot(_dot(perm_ref[...], us[gl]).astype(BF16), wz_ref[gl])

    ars = [jnp.broadcast_to(ar_ref[gl], (nbp, 2 * s)) for gl in groups]
    ais = [jnp.broadcast_to(ai_ref[gl], (nbp, 2 * s)) for gl in groups]
    srs = [s0_ref[gl, :, 0:2 * s] for gl in groups]
    sis = [s0_ref[gl, :, 2 * s:4 * s] for gl in groups]
    for k in range(nc):
        up = slice(k * nbp, (k + 1) * nbp)
        dn = slice((nc - 1 - k) * nbp, (nc - k) * nbp)
        for gl in groups:
            sr, si = srs[gl], sis[gl]
            sp_scr[gl, up, 0:s] = sr[:, 0:s]
            sp_scr[gl, dn, s:2 * s] = sr[:, s:2 * s]
            sp_scr[gl, up, 2 * s:3 * s] = si[:, 0:s]
            sp_scr[gl, dn, 3 * s:4 * s] = si[:, s:2 * s]
            zr = jnp.where(fwd, z_scr[gl, up, 0:2 * s], z_scr[gl, dn, 0:2 * s])
            zi = jnp.where(fwd, z_scr[gl, up, 2 * s:4 * s], z_scr[gl, dn, 2 * s:4 * s])
            srs[gl] = ars[gl] * sr - ais[gl] * si + zr
            sis[gl] = ars[gl] * si + ais[gl] * sr + zi
    for gl in groups:
        fin_ref[gl, :, 0:2 * s] = srs[gl]
        fin_ref[gl, :, 2 * s:4 * s] = sis[gl]
    sps = [_dot(permt_ref[...], sp_scr[gl].astype(BF16)).astype(BF16) for gl in groups]
    for gl in groups:
        yt = (_dot(us[gl], m_ref[gl]) + _dot_nt(sps[gl], wyt_ref[gl])).T
        for t in range(CHUNK):
            xt_scr[t, chans[gl], :] = yt[t * S5_GROUP:(t + 1) * S5_GROUP, :]

    for t in range(CHUNK):
        y_ref[pl.ds(t, r, stride=CHUNK), :] = xt_scr[t].T


def _s5_mixer(u, s0, ops, nb, seq):
    m, wz, wyt, ar, ai = ops
    g, nc = S5_GROUPS, seq // CHUNK
    nbp = -(-nb // 8) * 8
    r, rp = nb * nc, nc * nbp
    gpb = GROUPS_PER_BLOCK
    s0l = jnp.zeros((g, nbp, CW), F32)
    if s0 is not None:
        s0l = s0l.at[:, :nb].set(s0.transpose(3, 0, 2, 1, 4).reshape(g, nb, CW))
    perm = np.zeros((nc, nbp, nb, nc), np.float32)
    for b in range(nb):
        perm[np.arange(nc), b, b, np.arange(nc)] = 1.0
    perm = jnp.asarray(perm.reshape(rp, r)).astype(BF16)
    col = pl.BlockSpec((nb * seq, gpb * S5_GROUP), lambda j: (0, j))
    wspec = pl.BlockSpec((gpb, CW, CW), lambda j: (j, 0, 0))
    cspec = pl.BlockSpec((gpb, 1, 2 * S5_STATE), lambda j: (j, 0, 0))
    sspec = pl.BlockSpec((gpb, nbp, CW), lambda j: (j, 0, 0))
    y, fin = pl.pallas_call(
        functools.partial(_s5_kernel, nbp=nbp, nc=nc),
        grid=(g // gpb,),
        in_specs=[col, wspec, wspec, wspec, cspec, cspec, sspec, _full_spec((rp, r)), _full_spec((r, rp))],
        out_specs=[col, sspec],
        out_shape=[jax.ShapeDtypeStruct((nb * seq, S5_WIDTH), F32), jax.ShapeDtypeStruct((g, nbp, CW), F32)],
        scratch_shapes=[pltpu.VMEM((CHUNK, gpb * S5_GROUP, r), F32),
                        pltpu.VMEM((gpb, rp, CW), F32), pltpu.VMEM((gpb, rp, CW), F32)],
        compiler_params=_params(("parallel",)),
        name="s5_scan",
    )(u, m, wz, wyt, ar, ai, s0l, perm, perm.T)
    state = fin[:, :nb].reshape(g, nb, 2, 2, S5_STATE).transpose(1, 3, 2, 0, 4)
    return y, state


FFN_ROWS = 512
FFN_CHUNK = 256


def _ffn_rows(x, g, shift, scale, gate, wg_ref, wu_ref, wd_ref):
    h = _norm_mod(x, g, shift, scale).astype(BF16)
    acc = jnp.zeros(x.shape, F32)
    for j in range(D_FF // FFN_CHUNK):
        sl = slice(j * FFN_CHUNK, (j + 1) * FFN_CHUNK)
        gg = _dot(h, wg_ref[:, sl])
        uu = _dot(h, wu_ref[:, sl])
        act = (gg * jax.nn.sigmoid(gg) * uu).astype(BF16)
        acc = acc + _dot(act, wd_ref[sl, :])
    return x + gate * acc


def _layer_weight_spec(w, layer):
    return pl.BlockSpec((None,) + w.shape[1:], lambda *idx: (layer, 0, 0), pipeline_mode=pl.Buffered(1))


def _ffn_specs(layer, row_fn, wg, wu, wd):
    return [_full_spec((1, D_MODEL)),
            _mod_spec(layer, 3, row_fn), _mod_spec(layer, 4, row_fn), _mod_spec(layer, 5, row_fn),
            _layer_weight_spec(wg, layer), _layer_weight_spec(wu, layer), _layer_weight_spec(wd, layer)]


def _even_tail_kernel(a_ref, y_ref, u_ref, x_ref, d_ref, gw_ref, gb_ref, wo_ref, gate_ref,
                      g2_ref, sh2_ref, sc2_ref, gate2_ref, wg_ref, wu_ref, wd_ref, o_ref):
    y = jax.nn.gelu(y_ref[...] + d_ref[...] * u_ref[...])
    y = y * jax.nn.sigmoid(_dot(y.astype(BF16), gw_ref[...]) + gb_ref[...])
    o = _dot(a_ref[...].astype(BF16), wo_ref[0:NA_WIDTH, :]) + _dot(y.astype(BF16), wo_ref[NA_WIDTH:, :])
    x = x_ref[...] + gate_ref[0] * o
    o_ref[...] = _ffn_rows(x, g2_ref[...], sh2_ref[0], sc2_ref[0], gate2_ref[0], wg_ref, wu_ref, wd_ref)


def _even_tail(a, y, u, x, d, glu_w, glu_b, w_out, g_ffn, wg, wu, wd, mod, layer, row_fn):
    t = x.shape[0]
    tm = FFN_ROWS
    half = pl.BlockSpec((tm, NA_WIDTH), lambda i: (i, 0))
    full = pl.BlockSpec((tm, D_MODEL), lambda i: (i, 0))
    return pl.pallas_call(
        _even_tail_kernel,
        grid=(t // tm,),
        in_specs=[half, half, half, full, _full_spec((1, S5_WIDTH)), _resident_spec(glu_w.shape),
                  _full_spec((1, S5_WIDTH)), _resident_spec(w_out.shape), _mod_spec(layer, 2, row_fn)]
                 + _ffn_specs(layer, row_fn, wg, wu, wd),
        out_specs=full,
        out_shape=jax.ShapeDtypeStruct((t, D_MODEL), F32),
        compiler_params=_params(("parallel",)),
        name="even_tail",
    )(a, y, u, x, d, glu_w, glu_b, w_out, mod, g_ffn, mod, mod, mod, wg, wu, wd)


def _dft_consts(seq):
    def cs(n):
        k = np.arange(n)
        ang = 2.0 * np.pi * ((k[:, None] * k[None, :]) % n) / n
        return np.cos(ang), np.sin(ang)

    cc, sc = cs(FNET_GROUP_WIDTH)
    cl, sl = cs(seq)
    chan = jnp.asarray(np.concatenate([cc, sc], axis=1), dtype=F32)
    pos = jnp.asarray(np.concatenate([cl, -sl], axis=1), dtype=F32)
    return chan.astype(BF16), pos.astype(BF16)


def _fourier_rows(x, g, shift, scale, gate, win_ref, chan_ref, pos_ref, wout_ref):
    seq = x.shape[0]
    h = _norm_mod(x, g, shift, scale).astype(BF16)
    z = _dot(h, win_ref[...]).astype(BF16)
    w = FNET_GROUP_WIDTH
    pos = pos_ref[...]
    parts = []
    for c in range(FNET_GROUPS):
        ab = _dot(z[:, c * w:(c + 1) * w], chan_ref[...]).astype(BF16)
        stacked = jnp.concatenate([ab[:, :w], ab[:, w:]], axis=0)
        parts.append(_dot(pos, stacked))
    f = jnp.concatenate(parts, axis=1) * (1.0 / math.sqrt(seq * w))
    return x + gate * _dot(f.astype(BF16), wout_ref[...])


def _odd_layer_kernel(x_ref, g_ref, sh_ref, sc_ref, win_ref, chan_ref, pos_ref, wout_ref, gate_ref,
                      g2_ref, sh2_ref, sc2_ref, gate2_ref, wg_ref, wu_ref, wd_ref, fg_ref, o_ref, *, seq):
    tm = x_ref.shape[0]
    for r0 in range(0, tm, seq):
        rows = slice(r0, r0 + seq)
        o_ref[rows, :] = _fourier_rows(x_ref[rows, :], g_ref[...], sh_ref[0], sc_ref[0], gate_ref[0],
                                       win_ref, chan_ref, pos_ref, wout_ref)
    for r0 in range(0, tm, FFN_ROWS):
        rows = slice(r0, r0 + FFN_ROWS)
        y = _ffn_rows(o_ref[rows, :], g2_ref[...], sh2_ref[0], sc2_ref[0], gate2_ref[0], wg_ref, wu_ref, wd_ref)
        ms = jnp.mean(y * y, axis=-1, keepdims=True)
        o_ref[rows, :] = y * lax.rsqrt(ms + EPS) * fg_ref[...]


def _odd_layer(x, g_mix, g_ffn, final_g, mod, layer, row_of, w_in, w_out, wg, wu, wd, seq):
    t = x.shape[0]
    tm = max(seq, FFN_ROWS)
    row_fn = row_of(tm)
    chan, pos = _dft_consts(seq)
    blk = pl.BlockSpec((tm, D_MODEL), lambda i: (i, 0))
    return pl.pallas_call(
        functools.partial(_odd_layer_kernel, seq=seq),
        grid=(t // tm,),
        in_specs=[blk, _full_spec((1, D_MODEL)), _mod_spec(layer, 0, row_fn), _mod_spec(layer, 1, row_fn),
                  _resident_spec(w_in.shape), _resident_spec(chan.shape), _resident_spec(pos.shape),
                  _resident_spec(w_out.shape), _mod_spec(layer, 2, row_fn)]
                 + _ffn_specs(layer, row_fn, wg, wu, wd) + [_full_spec((1, D_MODEL))],
        out_specs=blk,
        out_shape=jax.ShapeDtypeStruct((t, D_MODEL), F32),
        compiler_params=_params(("parallel",)),
        name="odd_layer",
    )(x, g_mix, mod, mod, w_in, chan, pos, w_out, mod, g_ffn, mod, mod, mod, wg, wu, wd, final_g)


def kernel(x_prompt, x_sample, cache_na_k, cache_na_v, state_s5, c, c_ctx, w_mod, b_mod, norm_mix_g, norm_ffn_g, w_in_even, na_rpb, s5_lam_re, s5_lam_im, s5_log_step, s5_b_re, s5_b_im, s5_c_re, s5_c_im, s5_d, s5_glu_w, s5_glu_b, w_out_even, w_in_odd, w_out_odd, ffn_w_gate, ffn_w_up, ffn_w_down, final_norm_g):
    bp, lp, _ = x_prompt.shape
    bs, ls, _ = x_sample.shape
    depth = w_mod.shape[0]
    assert bs + 1 <= N_COND and depth == 2

    cond = jnp.zeros((COND_ROWS, D_MODEL), F32).at[0].set(c_ctx).at[1:1 + bs].set(c)
    mod = _modulation(cond.T, w_mod, b_mod).reshape(depth * COND_ROWS * 6, 1, D_MODEL)

    streams = [
        dict(x=x_prompt.reshape(bp * lp, D_MODEL), nb=bp, seq=lp, row_of=lambda tm: (lambda i: 0)),
        dict(x=x_sample.reshape(bs * ls, D_MODEL), nb=bs, seq=ls, row_of=lambda tm: (lambda i: 1 + (i * tm) // ls)),
    ]
    tm = FFN_ROWS

    e = 0
    w_in = w_in_even[e].astype(BF16)
    w_out = w_out_even[e].astype(BF16)
    glu_w = s5_glu_w[e].astype(BF16)
    glu_b = s5_glu_b[e].reshape(1, S5_WIDTH)
    g_mix = norm_mix_g[0].reshape(1, D_MODEL)
    g_ffn0 = norm_ffn_g[0].reshape(1, D_MODEL)
    ffn_w = tuple(w.astype(BF16) for w in (ffn_w_gate, ffn_w_up, ffn_w_down))
    s5_ops = _s5_prep(s5_lam_re[e], s5_lam_im[e], s5_log_step[e], s5_b_re[e], s5_b_im[e], s5_c_re[e], s5_c_im[e])
    s5_dvec = s5_d[e].reshape(1, S5_WIDTH)
    bias = _na_bias(na_rpb[e], ls)

    new_k = new_v = new_s = None
    for si, st in enumerate(streams):
        nb, seq = st["nb"], st["seq"]
        if si == 0:
            a, u, new_k, new_v = _inproj(st["x"], g_mix, mod, 0, st["row_of"](tm), w_in, tm, ctx_seq=seq)
            s0 = None
        else:
            q, k, v, u = _inproj(st["x"], g_mix, mod, 0, st["row_of"](tm), w_in, tm)
            a = _na_attention(q.reshape(nb, seq, NA_WIDTH), k.reshape(nb, seq, NA_WIDTH), v.reshape(nb, seq, NA_WIDTH),
                              cache_na_k, cache_na_v, e, bias).reshape(nb * seq, NA_WIDTH)
            s0 = state_s5[:, e].astype(F32)
        y, state = _s5_mixer(u, s0, s5_ops, nb, seq)
        if si == 0:
            new_s = state[:, None]
        st["x"] = _even_tail(a, y, u, st["x"], s5_dvec, glu_w, glu_b, w_out, g_ffn0, *ffn_w,
                             mod, 0, st["row_of"](tm))

    for st in streams:
        st["x"] = _odd_layer(st["x"], norm_mix_g[1].reshape(1, D_MODEL), norm_ffn_g[1].reshape(1, D_MODEL),
                             final_norm_g.reshape(1, D_MODEL), mod, 1, st["row_of"],
                             w_in_odd[0].astype(BF16), w_out_odd[0].astype(BF16), *ffn_w, st["seq"])

    y_prompt = streams[0]["x"].reshape(bp, lp, D_MODEL)
    y_sample = streams[1]["x"].reshape(bs, ls, D_MODEL)
    return (y_prompt, y_sample, new_k, new_v, new_s)
```

```python
import functools
import math

import numpy as np
import jax
import jax.numpy as jnp
from jax import lax
from jax.experimental import pallas as pl
from jax.experimental.pallas import tpu as pltpu

F32 = jnp.float32
BF16 = jnp.bfloat16

D_MODEL = 1024
NA_HEADS = 8
HEAD_DIM = 64
NA_WIDTH = NA_HEADS * HEAD_DIM
GRID_W = 64
WIN_R = 8
WIN_C = 16
S5_GROUP = 16
S5_GROUPS = 32
S5_STATE = 64
S5_WIDTH = S5_GROUPS * S5_GROUP
FNET_GROUPS = 4
FNET_GROUP_WIDTH = D_MODEL // FNET_GROUPS
D_FF = 2816
EPS = 1e-6

CHUNK = 16
CW = CHUNK * S5_GROUP
GROUPS_PER_BLOCK = 128 // S5_GROUP
N_COND = 3
COND_ROWS = 8
NEG = -1e30
VMEM_LIMIT = 56 * 2 ** 20

_NT = (((1,), (1,)), ((), ()))


def _params(sem, vmem=VMEM_LIMIT):
    return pltpu.CompilerParams(dimension_semantics=sem, vmem_limit_bytes=vmem)


def _dot(a, b):
    return jnp.dot(a, b, preferred_element_type=F32)


def _dot_nt(a, b):
    return lax.dot_general(a, b, _NT, preferred_element_type=F32)


def _norm_mod(x, g, shift, scale):
    ms = jnp.mean(x * x, axis=-1, keepdims=True)
    return (x * lax.rsqrt(ms + EPS) * g) * (1.0 + scale) + shift


def _mod_spec(layer, which, row_fn):
    return pl.BlockSpec((1, 1, D_MODEL),
                        lambda *idx: (layer * COND_ROWS * 6 + row_fn(*idx) * 6 + which, 0, 0))


def _full_spec(shape):
    nd = len(shape)
    return pl.BlockSpec(shape, lambda *idx: (0,) * nd)


def _resident_spec(shape):
    nd = len(shape)
    return pl.BlockSpec(shape, lambda *idx: (0,) * nd, pipeline_mode=pl.Buffered(1))


def _cast_plan(weights, steps):
    in_specs, out_specs, out_shape, operands = [], [], [], []
    for w in weights:
        w, layer = w if isinstance(w, tuple) else (w, None)
        rows, cols = w.shape[-2:]
        slab = rows // steps
        if layer is None:
            in_specs.append(pl.BlockSpec((slab, cols), lambda i: (i, 0)))
        else:
            in_specs.append(pl.BlockSpec((None, slab, cols), lambda i, layer=layer: (layer, i, 0)))
        out_specs.append(pl.BlockSpec((slab, cols), lambda i: (i, 0)))
        out_shape.append(jax.ShapeDtypeStruct((rows, cols), BF16))
        operands.append(w)
    return in_specs, out_specs, out_shape, operands


def _run_casts(src_refs, dst_refs):
    for s, d in zip(src_refs, dst_refs):
        d[...] = s[...].astype(d.dtype)


def _mod_kernel(ct_ref, w_ref, b_ref, o_ref):
    ct = ct_ref[...]
    st = ct * jax.nn.sigmoid(ct)
    w = w_ref[0]
    o_ref[0] = jnp.zeros(o_ref.shape[1:], F32)
    for r in range(N_COND):
        o_ref[0, r:r + 1, :] = jnp.sum(w * st[:, r:r + 1], axis=0, keepdims=True) + b_ref[0]


def _modulation(cond_t, w_mod, b_mod):
    depth, _, n = w_mod.shape
    tn = 1024
    return pl.pallas_call(
        _mod_kernel,
        grid=(depth, n // tn),
        in_specs=[_full_spec((D_MODEL, COND_ROWS)),
                  pl.BlockSpec((1, D_MODEL, tn), lambda l, j: (l, 0, j)),
                  pl.BlockSpec((1, 1, tn), lambda l, j: (l, 0, j))],
        out_specs=pl.BlockSpec((1, COND_ROWS, tn), lambda l, j: (l, 0, j)),
        out_shape=jax.ShapeDtypeStruct((depth, COND_ROWS, n), F32),
        compiler_params=_params(("parallel", "parallel")),
        name="modulation",
    )(cond_t, w_mod, b_mod.reshape(depth, 1, n))


def _ctx_attention_rows(q, k, v, seq):
    scale = HEAD_DIM ** -0.5
    nseq = q.shape[0] // seq
    qt = (q * scale).T.astype(BF16)
    vt = v.T.astype(BF16)
    kb = k.astype(BF16)
    probs = [(slice(i * seq, (i + 1) * seq), slice(h * HEAD_DIM, (h + 1) * HEAD_DIM))
             for i in range(nseq) for h in range(NA_HEADS)]
    sts = [_dot(kb[rows, sl], qt[sl, rows]) for rows, sl in probs]
    pts = [jnp.exp(st - jnp.max(st, axis=0, keepdims=True)) for st in sts]
    dens = [jnp.sum(pt, axis=0, keepdims=True) for pt in pts]
    outs = [_dot(vt[sl, rows], pt.astype(BF16)) / den for (rows, sl), pt, den in zip(probs, pts, dens)]
    per_seq = [jnp.concatenate(outs[i * NA_HEADS:(i + 1) * NA_HEADS], axis=0).T for i in range(nseq)]
    return jnp.concatenate(per_seq, axis=0)


def _inproj_kernel(x_ref, g_ref, sh_ref, sc_ref, w_ref, *refs, ctx_seq, n_cast):
    _run_casts(refs[:n_cast], refs[n_cast + 4:2 * n_cast + 4])
    refs = refs[n_cast:n_cast + 4] + refs[2 * n_cast + 4:]
    h = _norm_mod(x_ref[...], g_ref[...], sh_ref[0], sc_ref[0]).astype(BF16)
    z = _dot(h, w_ref[...])
    q, k, v, u = (z[:, i * NA_WIDTH:(i + 1) * NA_WIDTH] for i in range(4))
    if ctx_seq is None:
        for o, val in zip(refs, (q, k, v, u)):
            o[...] = val.astype(o.dtype)
        return
    a_ref, u_ref, kc_ref, vc_ref, rows_scr = refs
    a_ref[...] = _ctx_attention_rows(q, k, v, ctx_seq).astype(a_ref.dtype)
    u_ref[...] = u
    tm = z.shape[0]
    pair = 2 * HEAD_DIM
    for o, val in ((kc_ref, k), (vc_ref, v)):
        for j in range(NA_HEADS // 2):
            both = val[:, j * pair:(j + 1) * pair]
            rows_scr[pl.ds(2 * j, tm, stride=NA_HEADS), :] = both
            rows_scr[pl.ds(2 * j + 1, tm, stride=NA_HEADS), :] = pltpu.roll(both, HEAD_DIM, axis=1)
        for b in range(o.shape[0]):
            blk = rows_scr[b * ctx_seq * NA_HEADS:(b + 1) * ctx_seq * NA_HEADS, 0:HEAD_DIM]
            o[b, 0] = blk.reshape(ctx_seq, NA_HEADS, HEAD_DIM)


def _inproj(x, g, mod, layer, row_fn, w, tm, ctx_seq=None, casts=()):
    t = x.shape[0]
    c_in, c_out, c_shape, c_ops = _cast_plan(casts, t // tm)
    flat = pl.BlockSpec((tm, NA_WIDTH), lambda i: (i, 0))
    scratch = []
    if ctx_seq is None:
        out_specs = [flat] * 4
        out_shape = [jax.ShapeDtypeStruct((t, NA_WIDTH), BF16)] + [jax.ShapeDtypeStruct((t, NA_WIDTH), F32)] * 3
    else:
        per = tm // ctx_seq
        cache = pl.BlockSpec((per, 1, ctx_seq, NA_HEADS, HEAD_DIM), lambda i: (i, 0, 0, 0, 0))
        out_specs = [flat, flat, cache, cache]
        out_shape = ([jax.ShapeDtypeStruct((t, NA_WIDTH), BF16), jax.ShapeDtypeStruct((t, NA_WIDTH), F32)]
                     + [jax.ShapeDtypeStruct((t // ctx_seq, 1, ctx_seq, NA_HEADS, HEAD_DIM), F32)] * 2)
        scratch = [pltpu.VMEM((tm * NA_HEADS, 2 * HEAD_DIM), F32)]
    return pl.pallas_call(
        functools.partial(_inproj_kernel, ctx_seq=ctx_seq, n_cast=len(c_ops)),
        grid=(t // tm,),
        in_specs=[pl.BlockSpec((tm, D_MODEL), lambda i: (i, 0)),
                  _full_spec((1, D_MODEL)),
                  _mod_spec(layer, 0, row_fn), _mod_spec(layer, 1, row_fn),
                  _resident_spec(w.shape)] + c_in,
        out_specs=out_specs + c_out,
        out_shape=out_shape + c_shape,
        scratch_shapes=scratch,
        compiler_params=_params(("parallel",)),
        name="inproj_even",
    )(x, g, mod, mod, w, *c_ops)


def _na_bias_kernel(rpb_ref, o_ref, *, rows):
    h = pl.program_id(0)
    kr_win = min(WIN_R, rows)
    qc = lax.broadcasted_iota(jnp.int32, (GRID_W, 2 * GRID_W), 0)
    lane = lax.broadcasted_iota(jnp.int32, (GRID_W, 2 * GRID_W), 1)
    kc = lane & (GRID_W - 1)
    cs = jnp.clip(qc - WIN_C // 2, 0, GRID_W - WIN_C)
    col_ok = (kc >= cs) & (kc < cs + WIN_C)
    dc = kc - qc + (WIN_C - 1)
    left = lane < GRID_W
    neg = jnp.full((GRID_W, 2 * GRID_W), NEG, F32)
    toeplitz = []
    for dr in range(2 * WIN_R - 1):
        t = neg
        for d in range(2 * WIN_C - 1):
            t = jnp.where(dc == d, rpb_ref[h, dr, d], t)
        toeplitz.append(jnp.where(col_ok, t, neg))
    for r in range(rows):
        rs = min(max(r - kr_win // 2, 0), rows - kr_win)

        def blk(kr):
            return toeplitz[kr - r + WIN_R - 1] if rs <= kr < rs + kr_win else None

        for j in range(rows // 2):
            a, b = blk(2 * j), blk(2 * j + 1)
            if a is None and b is None:
                pair = neg
            else:
                pair = jnp.where(left, neg if a is None else a, neg if b is None else b)
            o_ref[0, r * GRID_W:(r + 1) * GRID_W, j * 2 * GRID_W:(j + 1) * 2 * GRID_W] = pair


def _na_bias(rpb, seq):
    rows = seq // GRID_W
    return pl.pallas_call(
        functools.partial(_na_bias_kernel, rows=rows),
        grid=(NA_HEADS,),
        in_specs=[pl.BlockSpec(memory_space=pltpu.SMEM)],
        out_specs=pl.BlockSpec((1, seq, seq), lambda h: (h, 0, 0)),
        out_shape=jax.ShapeDtypeStruct((NA_HEADS, seq, seq), F32),
        compiler_params=_params(("parallel",)),
        name="na_bias",
    )(rpb)


def _na_key_range(qb, tq, rows):
    kr_win = min(WIN_R, rows)
    r0, r1 = qb * tq // GRID_W, ((qb + 1) * tq - 1) // GRID_W
    rs0 = min(max(r0 - kr_win // 2, 0), rows - kr_win)
    rs1 = min(max(r1 - kr_win // 2, 0), rows - kr_win)
    lo, hi = rs0 * GRID_W, (rs1 + kr_win) * GRID_W
    return lo // 128 * 128, -(-hi // 128) * 128


def _na_attn_kernel(q_ref, k_ref, v_ref, ck_ref, cv_ref, bias_ref, o_ref, *, tq):
    scale = HEAD_DIM ** -0.5
    seq = q_ref.shape[1]
    probs = []
    for hl in range(2):
        sl = slice(hl * HEAD_DIM, (hl + 1) * HEAD_DIM)
        for qb in range(seq // tq):
            lo, hi = _na_key_range(qb, tq, seq // GRID_W)
            probs.append((hl, sl, slice(qb * tq, (qb + 1) * tq), slice(lo, hi)))
    head0 = pl.program_id(0) * 2
    cks = [ck_ref[0, 0, :, head0 + hl, :].astype(BF16) for hl in range(2)]
    cvs = [cv_ref[0, 0, :, head0 + hl, :].astype(BF16) for hl in range(2)]
    qs_all = [(q_ref[0, qs, sl] * scale).astype(BF16) for _, sl, qs, _ in probs]
    s_loc = [_dot_nt(q, k_ref[0, ks, sl].astype(BF16)) + bias_ref[hl, qs, ks]
             for q, (hl, sl, qs, ks) in zip(qs_all, probs)]
    s_ctx = [_dot_nt(q, cks[hl]) for q, (hl, _, _, _) in zip(qs_all, probs)]
    ms = [jnp.maximum(jnp.max(a, axis=-1, keepdims=True), jnp.max(b, axis=-1, keepdims=True))
          for a, b in zip(s_loc, s_ctx)]
    p_loc = [jnp.exp(a - m) for a, m in zip(s_loc, ms)]
    p_ctx = [jnp.exp(b - m) for b, m in zip(s_ctx, ms)]
    dens = [jnp.sum(a, axis=-1, keepdims=True) + jnp.sum(b, axis=-1, keepdims=True) for a, b in zip(p_loc, p_ctx)]
    for a, b, den, (hl, sl, qs, ks) in zip(p_loc, p_ctx, dens, probs):
        o = _dot(a.astype(BF16), v_ref[0, ks, sl].astype(BF16)) + _dot(b.astype(BF16), cvs[hl])
        o_ref[0, qs, sl] = o / den


def _na_attention(q, k, v, ck, cv, layer, bias):
    nb, seq, _ = q.shape
    lat = pl.BlockSpec((1, seq, 2 * HEAD_DIM), lambda hp, b: (b, 0, hp))
    ctx = pl.BlockSpec((1, 1) + ck.shape[2:], lambda hp, b: (b, layer, 0, 0, 0))
    return pl.pallas_call(
        functools.partial(_na_attn_kernel, tq=256),
        grid=(NA_HEADS // 2, nb),
        in_specs=[lat, lat, lat, ctx, ctx, pl.BlockSpec((2, seq, seq), lambda hp, b: (hp, 0, 0))],
        out_specs=lat,
        out_shape=jax.ShapeDtypeStruct((nb, seq, NA_WIDTH), F32),
        compiler_params=_params(("parallel", "arbitrary")),
        name="na_attention",
    )(q, k, v, ck, cv, bias)


def _s5_prep_group(i, lamr_ref, lami_ref, logdt_ref, btr_ref, bti_ref, cr_ref, ci_ref,
                   m_ref, wz_ref, wyt_ref, ar_ref, ai_ref):
    lr = jnp.minimum(lamr_ref[i], -1e-4)
    li = lami_ref[i]
    dt = jnp.exp(logdt_ref[i])
    a = lr * dt
    th = li * dt
    mag = jnp.exp(a)
    lbr = mag * jnp.cos(th)
    lbi = mag * jnp.sin(th)
    den = lr * lr + li * li
    nr = lbr - 1.0
    coef_r = (nr * lr + lbi * li) / den
    coef_i = (lbi * lr - nr * li) / den
    btr, bti = btr_ref[i], bti_ref[i]
    bbr = coef_r * btr - coef_i * bti
    bbi = coef_r * bti + coef_i * btr
    cr, ci = cr_ref[i], ci_ref[i]

    step = lax.broadcasted_iota(jnp.int32, (CHUNK, 2 * S5_STATE), 0).astype(F32)
    fwd = lax.broadcasted_iota(jnp.int32, (CHUNK, 2 * S5_STATE), 1) < S5_STATE

    def powers(e):
        pm = jnp.exp(a * e)
        return pm * jnp.cos(th * e), pm * jnp.sin(th * e)

    pzr, pzi = powers(jnp.where(fwd, CHUNK - 1 - step, step))
    pyr, pyi = powers(jnp.where(fwd, step + 1.0, CHUNK - step))
    p16r, p16i = powers(jnp.full((1, 2 * S5_STATE), float(CHUNK), F32))
    ar_ref[i] = p16r
    ai_ref[i] = p16i

    def outer(p, c):
        return (p[:, None, :] * c[None, :, :]).reshape(CW, 2 * S5_STATE)

    wzr = outer(pzr, bbr) - outer(pzi, bbi)
    wzi = outer(pzr, bbi) + outer(pzi, bbr)
    wz_ref[i, :, 0:128] = wzr.astype(wz_ref.dtype)
    wz_ref[i, :, 128:256] = wzi.astype(wz_ref.dtype)
    wyt_ref[i, :, 0:128] = (outer(pyr, cr) - outer(pyi, ci)).astype(wyt_ref.dtype)
    wyt_ref[i, :, 128:256] = (-(outer(pyi, cr) + outer(pyr, ci))).astype(wyt_ref.dtype)

    ones = jnp.ones((CHUNK, 2 * S5_STATE), F32)
    cwr = outer(ones, cr)
    cwi = outer(ones, ci)
    fwd_rows = lax.broadcasted_iota(jnp.int32, (CW, 2 * S5_STATE), 1) < S5_STATE
    zero = jnp.zeros((CW, 2 * S5_STATE), F32)

    def split(x):
        hi = x.astype(BF16)
        return hi, (x - hi.astype(F32)).astype(BF16)

    cw_hi, cw_lo = split(jnp.concatenate([cwr, -cwi], axis=1))

    def kk(sel):
        lhs_hi, lhs_lo = split(jnp.concatenate([jnp.where(sel, wzr, zero), jnp.where(sel, wzi, zero)], axis=1))
        return _dot_nt(lhs_hi, cw_hi) + (_dot_nt(lhs_hi, cw_lo) + _dot_nt(lhs_lo, cw_hi))

    kk_f = kk(fwd_rows)
    kk_b = kk(jnp.logical_not(fwd_rows))
    col_step = lax.broadcasted_iota(jnp.int32, (CW, CW), 1) // S5_GROUP
    m = jnp.zeros((CW, CW), F32)
    for t in range(CHUNK):
        up = (CHUNK - 1 - t) * S5_GROUP
        dn = t * S5_GROUP
        piece_f = kk_f if up == 0 else jnp.concatenate([kk_f[up:], jnp.zeros((up, CW), F32)], axis=0)
        piece_b = kk_b if dn == 0 else jnp.concatenate([jnp.zeros((dn, CW), F32), kk_b[:CW - dn]], axis=0)
        m = jnp.where(col_step == t, piece_f + piece_b, m)
    m_ref[i] = m.astype(m_ref.dtype)


def _s5_prep_kernel(*refs):
    for i in range(refs[0].shape[0]):
        _s5_prep_group(i, *refs)


def _s5_prep(lam_re, lam_im, log_step, b_re, b_im, c_re, c_im):
    g = S5_GROUPS
    lanes = 2 * S5_STATE

    def dirs_last(x):
        return x.transpose(1, 0, 2).reshape(g, 1, lanes)

    lamr, lami = dirs_last(lam_re), dirs_last(lam_im)
    logdt = dirs_last(jnp.broadcast_to(log_step[:, :, None], (2, g, S5_STATE)))
    bt = [x.transpose(1, 3, 0, 2).reshape(g, S5_GROUP, lanes) for x in (b_re, b_im)]
    ct = [x.transpose(1, 2, 0, 3).reshape(g, S5_GROUP, lanes) for x in (c_re, c_im)]
    gp = 4
    vec = pl.BlockSpec((gp, 1, lanes), lambda i: (i, 0, 0))
    mat = pl.BlockSpec((gp, S5_GROUP, lanes), lambda i: (i, 0, 0))
    big = pl.BlockSpec((gp, CW, CW), lambda i: (i, 0, 0))
    return pl.pallas_call(
        _s5_prep_kernel,
        grid=(g // gp,),
        in_specs=[vec, vec, vec, mat, mat, mat, mat],
        out_specs=[big, big, big, vec, vec],
        out_shape=[jax.ShapeDtypeStruct((g, CW, CW), BF16)] * 3 + [jax.ShapeDtypeStruct((g, 1, lanes), F32)] * 2,
        compiler_params=_params(("parallel",)),
        name="s5_prep",
    )(lamr, lami, logdt, bt[0], bt[1], ct[0], ct[1])


def _s5_kernel(u_ref, m_ref, wz_ref, wyt_ref, ar_ref, ai_ref, s0_ref, perm_ref, permt_ref, y_ref, fin_ref,
               xt_scr, z_scr, sp_scr, *, nbp, nc):
    s = S5_STATE
    r = xt_scr.shape[2]
    for t in range(CHUNK):
        xt_scr[t] = u_ref[pl.ds(t, r, stride=CHUNK), :].T
    fwd = lax.broadcasted_iota(jnp.int32, (nbp, 2 * s), 1) < s
    groups = range(GROUPS_PER_BLOCK)
    chans = [slice(gl * S5_GROUP, (gl + 1) * S5_GROUP) for gl in groups]

    us = [jnp.concatenate([xt_scr[t, ch, :] for t in range(CHUNK)], axis=0).T.astype(BF16) for ch in chans]
    for gl in groups:
        z_scr[gl] = _dot(_dot(perm_ref[...], us[gl]).astype(BF16), wz_ref[gl])

    ars = [jnp.broadcast_to(ar_ref[gl], (nbp, 2 * s)) for gl in groups]
    ais = [jnp.broadcast_to(ai_ref[gl], (nbp, 2 * s)) for gl in groups]
    srs = [s0_ref[gl, :, 0:2 * s] for gl in groups]
    sis = [s0_ref[gl, :, 2 * s:4 * s] for gl in groups]
    for k in range(nc):
        up = slice(k * nbp, (k + 1) * nbp)
        dn = slice((nc - 1 - k) * nbp, (nc - k) * nbp)
        for gl in groups:
            sr, si = srs[gl], sis[gl]
            sp_scr[gl, up, 0:s] = sr[:, 0:s]
            sp_scr[gl, dn, s:2 * s] = sr[:, s:2 * s]
            sp_scr[gl, up, 2 * s:3 * s] = si[:, 0:s]
            sp_scr[gl, dn, 3 * s:4 * s] = si[:, s:2 * s]
            zr = jnp.where(fwd, z_scr[gl, up, 0:2 * s], z_scr[gl, dn, 0:2 * s])
            zi = jnp.where(fwd, z_scr[gl, up, 2 * s:4 * s], z_scr[gl, dn, 2 * s:4 * s])
            srs[gl] = ars[gl] * sr - ais[gl] * si + zr
            sis[gl] = ars[gl] * si + ais[gl] * sr + zi
    for gl in groups:
        fin_ref[gl, :, 0:2 * s] = srs[gl]
        fin_ref[gl, :, 2 * s:4 * s] = sis[gl]
    sps = [_dot(permt_ref[...], sp_scr[gl].astype(BF16)).astype(BF16) for gl in groups]
    for gl in groups:
        yt = (_dot(us[gl], m_ref[gl]) + _dot_nt(sps[gl], wyt_ref[gl])).T
        for t in range(CHUNK):
            xt_scr[t, chans[gl], :] = yt[t * S5_GROUP:(t + 1) * S5_GROUP, :]

    for t in range(CHUNK):
        y_ref[pl.ds(t, r, stride=CHUNK), :] = xt_scr[t].T


def _s5_mixer(u, s0, ops, nb, seq):
    m, wz, wyt, ar, ai = ops
    g, nc = S5_GROUPS, seq // CHUNK
    nbp = -(-nb // 8) * 8
    r, rp = nb * nc, nc * nbp
    gpb = GROUPS_PER_BLOCK
    s0l = jnp.zeros((g, nbp, CW), F32)
    if s0 is not None:
        s0l = s0l.at[:, :nb].set(s0.transpose(3, 0, 2, 1, 4).reshape(g, nb, CW))
    perm = np.zeros((nc, nbp, nb, nc), np.float32)
    for b in range(nb):
        perm[np.arange(nc), b, b, np.arange(nc)] = 1.0
    perm = jnp.asarray(perm.reshape(rp, r)).astype(BF16)
    col = pl.BlockSpec((nb * seq, gpb * S5_GROUP), lambda j: (0, j))
    wspec = pl.BlockSpec((gpb, CW, CW), lambda j: (j, 0, 0))
    cspec = pl.BlockSpec((gpb, 1, 2 * S5_STATE), lambda j: (j, 0, 0))
    sspec = pl.BlockSpec((gpb, nbp, CW), lambda j: (j, 0, 0))
    y, fin = pl.pallas_call(
        functools.partial(_s5_kernel, nbp=nbp, nc=nc),
        grid=(g // gpb,),
        in_specs=[col, wspec, wspec, wspec, cspec, cspec, sspec, _full_spec((rp, r)), _full_spec((r, rp))],
        out_specs=[col, sspec],
        out_shape=[jax.ShapeDtypeStruct((nb * seq, S5_WIDTH), F32), jax.ShapeDtypeStruct((g, nbp, CW), F32)],
        scratch_shapes=[pltpu.VMEM((CHUNK, gpb * S5_GROUP, r), F32),
                        pltpu.VMEM((gpb, rp, CW), F32), pltpu.VMEM((gpb, rp, CW), F32)],
        compiler_params=_params(("parallel",)),
        name="s5_scan",
    )(u, m, wz, wyt, ar, ai, s0l, perm, perm.T)
    state = fin[:, :nb].reshape(g, nb, 2, 2, S5_STATE).transpose(1, 3, 2, 0, 4)
    return y, state


FFN_ROWS = 512
FFN_CHUNK = 256


def _ffn_rows(x, g, shift, scale, gate, wg_ref, wu_ref, wd_ref):
    h = _norm_mod(x, g, shift, scale).astype(BF16)
    acc = jnp.zeros(x.shape, F32)
    for j in range(D_FF // FFN_CHUNK):
        sl = slice(j * FFN_CHUNK, (j + 1) * FFN_CHUNK)
        gg = _dot(h, wg_ref[:, sl])
        uu = _dot(h, wu_ref[:, sl])
        act = (gg * jax.nn.sigmoid(gg) * uu).astype(BF16)
        acc = acc + _dot(act, wd_ref[sl, :])
    return x + gate * acc


def _ffn_specs(layer, row_fn, wg, wu, wd):
    return [_full_spec((1, D_MODEL)),
            _mod_spec(layer, 3, row_fn), _mod_spec(layer, 4, row_fn), _mod_spec(layer, 5, row_fn),
            _resident_spec(wg.shape), _resident_spec(wu.shape), _resident_spec(wd.shape)]


def _even_tail_kernel(a_ref, y_ref, u_ref, x_ref, d_ref, gw_ref, gb_ref, wo_ref, gate_ref,
                      g2_ref, sh2_ref, sc2_ref, gate2_ref, wg_ref, wu_ref, wd_ref, *refs):
    n_cast = len(refs) // 2
    o_ref = refs[n_cast]
    _run_casts(refs[:n_cast], refs[n_cast + 1:])
    y = jax.nn.gelu(y_ref[...] + d_ref[...] * u_ref[...])
    y = y * jax.nn.sigmoid(_dot(y.astype(BF16), gw_ref[...]) + gb_ref[...])
    o = _dot(a_ref[...].astype(BF16), wo_ref[0:NA_WIDTH, :]) + _dot(y.astype(BF16), wo_ref[NA_WIDTH:, :])
    x = x_ref[...] + gate_ref[0] * o
    o_ref[...] = _ffn_rows(x, g2_ref[...], sh2_ref[0], sc2_ref[0], gate2_ref[0], wg_ref, wu_ref, wd_ref)


def _even_tail(a, y, u, x, d, glu_w, glu_b, w_out, g_ffn, wg, wu, wd, mod, layer, row_fn, casts=()):
    t = x.shape[0]
    tm = FFN_ROWS
    c_in, c_out, c_shape, c_ops = _cast_plan(casts, t // tm)
    half = pl.BlockSpec((tm, NA_WIDTH), lambda i: (i, 0))
    full = pl.BlockSpec((tm, D_MODEL), lambda i: (i, 0))
    return pl.pallas_call(
        _even_tail_kernel,
        grid=(t // tm,),
        in_specs=[half, half, half, full, _full_spec((1, S5_WIDTH)), _resident_spec(glu_w.shape),
                  _full_spec((1, S5_WIDTH)), _resident_spec(w_out.shape), _mod_spec(layer, 2, row_fn)]
                 + _ffn_specs(layer, row_fn, wg, wu, wd) + c_in,
        out_specs=[full] + c_out,
        out_shape=[jax.ShapeDtypeStruct((t, D_MODEL), F32)] + c_shape,
        compiler_params=_params(("parallel",)),
        name="even_tail",
    )(a, y, u, x, d, glu_w, glu_b, w_out, mod, g_ffn, mod, mod, mod, wg, wu, wd, *c_ops)


def _dft_consts(seq):
    def cs(n):
        k = np.arange(n)
        ang = 2.0 * np.pi * ((k[:, None] * k[None, :]) % n) / n
        return np.cos(ang), np.sin(ang)

    cc, sc = cs(FNET_GROUP_WIDTH)
    cl, sl = cs(seq)
    chan = jnp.asarray(np.concatenate([cc, sc], axis=1), dtype=F32)
    pos = jnp.asarray(np.concatenate([cl, -sl], axis=1), dtype=F32)
    return chan.astype(BF16), pos.astype(BF16)


def _fourier_rows(x, g, shift, scale, gate, win_ref, chan_ref, pos_ref, wout_ref):
    seq = x.shape[0]
    h = _norm_mod(x, g, shift, scale).astype(BF16)
    z = _dot(h, win_ref[...]).astype(BF16)
    w = FNET_GROUP_WIDTH
    pos = pos_ref[...]
    parts = []
    for c in range(FNET_GROUPS):
        ab = _dot(z[:, c * w:(c + 1) * w], chan_ref[...]).astype(BF16)
        stacked = jnp.concatenate([ab[:, :w], ab[:, w:]], axis=0)
        parts.append(_dot(pos, stacked))
    f = jnp.concatenate(parts, axis=1) * (1.0 / math.sqrt(seq * w))
    return x + gate * _dot(f.astype(BF16), wout_ref[...])


def _odd_layer_kernel(x_ref, g_ref, sh_ref, sc_ref, win_ref, chan_ref, pos_ref, wout_ref, gate_ref,
                      g2_ref, sh2_ref, sc2_ref, gate2_ref, wg_ref, wu_ref, wd_ref, fg_ref, o_ref, *, seq):
    tm = x_ref.shape[0]
    for r0 in range(0, tm, seq):
        rows = slice(r0, r0 + seq)
        o_ref[rows, :] = _fourier_rows(x_ref[rows, :], g_ref[...], sh_ref[0], sc_ref[0], gate_ref[0],
                                       win_ref, chan_ref, pos_ref, wout_ref)
    for r0 in range(0, tm, FFN_ROWS):
        rows = slice(r0, r0 + FFN_ROWS)
        y = _ffn_rows(o_ref[rows, :], g2_ref[...], sh2_ref[0], sc2_ref[0], gate2_ref[0], wg_ref, wu_ref, wd_ref)
        ms = jnp.mean(y * y, axis=-1, keepdims=True)
        o_ref[rows, :] = y * lax.rsqrt(ms + EPS) * fg_ref[...]


def _odd_layer(x, g_mix, g_ffn, final_g, mod, layer, row_of, w_in, w_out, wg, wu, wd, seq):
    t = x.shape[0]
    tm = max(seq, FFN_ROWS)
    row_fn = row_of(tm)
    chan, pos = _dft_consts(seq)
    blk = pl.BlockSpec((tm, D_MODEL), lambda i: (i, 0))
    return pl.pallas_call(
        functools.partial(_odd_layer_kernel, seq=seq),
        grid=(t // tm,),
        in_specs=[blk, _full_spec((1, D_MODEL)), _mod_spec(layer, 0, row_fn), _mod_spec(layer, 1, row_fn),
                  _resident_spec(w_in.shape), _resident_spec(chan.shape), _resident_spec(pos.shape),
                  _resident_spec(w_out.shape), _mod_spec(layer, 2, row_fn)]
                 + _ffn_specs(layer, row_fn, wg, wu, wd) + [_full_spec((1, D_MODEL))],
        out_specs=blk,
        out_shape=jax.ShapeDtypeStruct((t, D_MODEL), F32),
        compiler_params=_params(("parallel",)),
        name="odd_layer",
    )(x, g_mix, mod, mod, w_in, chan, pos, w_out, mod, g_ffn, mod, mod, mod, wg, wu, wd, final_g)


def kernel(x_prompt, x_sample, cache_na_k, cache_na_v, state_s5, c, c_ctx, w_mod, b_mod, norm_mix_g, norm_ffn_g, w_in_even, na_rpb, s5_lam_re, s5_lam_im, s5_log_step, s5_b_re, s5_b_im, s5_c_re, s5_c_im, s5_d, s5_glu_w, s5_glu_b, w_out_even, w_in_odd, w_out_odd, ffn_w_gate, ffn_w_up, ffn_w_down, final_norm_g):
    bp, lp, _ = x_prompt.shape
    bs, ls, _ = x_sample.shape
    depth = w_mod.shape[0]
    assert bs + 1 <= N_COND and depth == 2

    cond = jnp.zeros((COND_ROWS, D_MODEL), F32).at[0].set(c_ctx).at[1:1 + bs].set(c)
    mod = _modulation(cond.T, w_mod, b_mod).reshape(depth * COND_ROWS * 6, 1, D_MODEL)

    streams = [
        dict(x=x_prompt.reshape(bp * lp, D_MODEL), nb=bp, seq=lp, row_of=lambda tm: (lambda i: 0)),
        dict(x=x_sample.reshape(bs * ls, D_MODEL), nb=bs, seq=ls, row_of=lambda tm: (lambda i: 1 + (i * tm) // ls)),
    ]
    tm = FFN_ROWS

    e = 0
    w_in = w_in_even[e].astype(BF16)
    glu_b = s5_glu_b[e].reshape(1, S5_WIDTH)
    g_mix = norm_mix_g[0].reshape(1, D_MODEL)
    g_ffn0 = norm_ffn_g[0].reshape(1, D_MODEL)
    s5_ops = _s5_prep(s5_lam_re[e], s5_lam_im[e], s5_log_step[e], s5_b_re[e], s5_b_im[e], s5_c_re[e], s5_c_im[e])
    s5_dvec = s5_d[e].reshape(1, S5_WIDTH)
    bias = _na_bias(na_rpb[e], ls)
    layer0_w = [(ffn_w_gate, 0), (ffn_w_up, 0), (ffn_w_down, 0), (w_out_even, e), (s5_glu_w, e)]
    layer1_w = [(ffn_w_gate, 1), (ffn_w_up, 1), (ffn_w_down, 1), (w_in_odd, 0), (w_out_odd, 0)]

    new_k = new_v = new_s = None
    for si, st in enumerate(streams):
        nb, seq = st["nb"], st["seq"]
        if si == 0:
            a, u, new_k, new_v, *layer0_w = _inproj(st["x"], g_mix, mod, 0, st["row_of"](tm), w_in, tm,
                                                    ctx_seq=seq, casts=layer0_w)
            wg0, wu0, wd0, w_out, glu_w = layer0_w
            s0 = None
        else:
            q, k, v, u = _inproj(st["x"], g_mix, mod, 0, st["row_of"](tm), w_in, tm)
            a = _na_attention(q.reshape(nb, seq, NA_WIDTH), k.reshape(nb, seq, NA_WIDTH), v.reshape(nb, seq, NA_WIDTH),
                              cache_na_k, cache_na_v, e, bias).reshape(nb * seq, NA_WIDTH)
            s0 = state_s5[:, e].astype(F32)
        y, state = _s5_mixer(u, s0, s5_ops, nb, seq)
        if si == 0:
            new_s = state[:, None]
        tail = _even_tail(a, y, u, st["x"], s5_dvec, glu_w, glu_b, w_out, g_ffn0, wg0, wu0, wd0,
                          mod, 0, st["row_of"](tm), casts=layer1_w if si == 0 else ())
        st["x"] = tail[0]
        if si == 0:
            wg1, wu1, wd1, w_in1, w_out1 = tail[1:]

    for st in streams:
        st["x"] = _odd_layer(st["x"], norm_mix_g[1].reshape(1, D_MODEL), norm_ffn_g[1].reshape(1, D_MODEL),
                             final_norm_g.reshape(1, D_MODEL), mod, 1, st["row_of"],
                             w_in1, w_out1, wg1, wu1, wd1, st["seq"])

    y_prompt = streams[0]["x"].reshape(bp, lp, D_MODEL)
    y_sample = streams[1]["x"].reshape(bs, ls, D_MODEL)
    return (y_prompt, y_sample, new_k, new_v, new_s)
```

```python
import functools
import math

import numpy as np
import jax
import jax.numpy as jnp
from jax import lax
from jax.experimental import pallas as pl
from jax.experimental.pallas import tpu as pltpu

F32 = jnp.float32
BF16 = jnp.bfloat16

D_MODEL = 1024
NA_HEADS = 8
HEAD_DIM = 64
NA_WIDTH = NA_HEADS * HEAD_DIM
GRID_W = 64
WIN_R = 8
WIN_C = 16
S5_GROUP = 16
S5_GROUPS = 32
S5_STATE = 64
S5_WIDTH = S5_GROUPS * S5_GROUP
FNET_GROUPS = 4
FNET_GROUP_WIDTH = D_MODEL // FNET_GROUPS
D_FF = 2816
EPS = 1e-6

CHUNK = 16
CW = CHUNK * S5_GROUP
GROUPS_PER_BLOCK = 128 // S5_GROUP
N_COND = 3
COND_ROWS = 8
NEG = -1e30
VMEM_LIMIT = 56 * 2 ** 20

_NT = (((1,), (1,)), ((), ()))


def _params(sem, vmem=VMEM_LIMIT):
    return pltpu.CompilerParams(dimension_semantics=sem, vmem_limit_bytes=vmem)


def _dot(a, b):
    return jnp.dot(a, b, preferred_element_type=F32)


def _dot_nt(a, b):
    return lax.dot_general(a, b, _NT, preferred_element_type=F32)


def _norm_mod(x, g, shift, scale):
    ms = jnp.mean(x * x, axis=-1, keepdims=True)
    return (x * lax.rsqrt(ms + EPS) * g) * (1.0 + scale) + shift


def _mod_spec(layer, which, row_fn):
    return pl.BlockSpec((1, 1, D_MODEL),
                        lambda *idx: (layer * COND_ROWS * 6 + row_fn(*idx) * 6 + which, 0, 0))


def _full_spec(shape):
    nd = len(shape)
    return pl.BlockSpec(shape, lambda *idx: (0,) * nd)


def _resident_spec(shape):
    nd = len(shape)
    return pl.BlockSpec(shape, lambda *idx: (0,) * nd, pipeline_mode=pl.Buffered(1))


def _cast_plan(weights, steps):
    in_specs, out_specs, out_shape, operands = [], [], [], []
    for w in weights:
        w, layer = w if isinstance(w, tuple) else (w, None)
        rows, cols = w.shape[-2:]
        slab = rows // steps
        if layer is None:
            in_specs.append(pl.BlockSpec((slab, cols), lambda i: (i, 0)))
        else:
            in_specs.append(pl.BlockSpec((None, slab, cols), lambda i, layer=layer: (layer, i, 0)))
        out_specs.append(pl.BlockSpec((slab, cols), lambda i: (i, 0)))
        out_shape.append(jax.ShapeDtypeStruct((rows, cols), BF16))
        operands.append(w)
    return in_specs, out_specs, out_shape, operands


def _run_casts(src_refs, dst_refs):
    for s, d in zip(src_refs, dst_refs):
        d[...] = s[...].astype(d.dtype)


MOD_COLS = 2048


def _mod_kernel(ct_ref, w_ref, b_ref, o_ref):
    ct = ct_ref[...]
    st = ct * jax.nn.sigmoid(ct)
    w = w_ref[0]
    o_ref[0] = jnp.zeros(o_ref.shape[1:], F32)
    for r in range(N_COND):
        o_ref[0, r:r + 1, :] = jnp.sum(w * st[:, r:r + 1], axis=0, keepdims=True) + b_ref[0]


def _modulation(cond_t, w_mod, b_mod):
    depth, _, n = w_mod.shape
    tn = MOD_COLS
    return pl.pallas_call(
        _mod_kernel,
        grid=(depth, n // tn),
        in_specs=[_full_spec((D_MODEL, COND_ROWS)),
                  pl.BlockSpec((1, D_MODEL, tn), lambda l, j: (l, 0, j)),
                  pl.BlockSpec((1, 1, tn), lambda l, j: (l, 0, j))],
        out_specs=pl.BlockSpec((1, COND_ROWS, tn), lambda l, j: (l, 0, j)),
        out_shape=jax.ShapeDtypeStruct((depth, COND_ROWS, n), F32),
        compiler_params=_params(("parallel", "parallel")),
        name="modulation",
    )(cond_t, w_mod, b_mod.reshape(depth, 1, n))


def _ctx_attention_rows(q, k, v, seq):
    scale = HEAD_DIM ** -0.5
    nseq = q.shape[0] // seq
    qt = (q * scale).T.astype(BF16)
    vt = v.T.astype(BF16)
    kb = k.astype(BF16)
    probs = [(slice(i * seq, (i + 1) * seq), slice(h * HEAD_DIM, (h + 1) * HEAD_DIM))
             for i in range(nseq) for h in range(NA_HEADS)]
    sts = [_dot(kb[rows, sl], qt[sl, rows]) for rows, sl in probs]
    pts = [jnp.exp(st - jnp.max(st, axis=0, keepdims=True)) for st in sts]
    dens = [jnp.sum(pt, axis=0, keepdims=True) for pt in pts]
    outs = [_dot(vt[sl, rows], pt.astype(BF16)) / den for (rows, sl), pt, den in zip(probs, pts, dens)]
    per_seq = [jnp.concatenate(outs[i * NA_HEADS:(i + 1) * NA_HEADS], axis=0).T for i in range(nseq)]
    return jnp.concatenate(per_seq, axis=0)


def _inproj_kernel(x_ref, g_ref, sh_ref, sc_ref, w_ref, *refs, ctx_seq, n_cast):
    _run_casts(refs[:n_cast], refs[n_cast + 4:2 * n_cast + 4])
    refs = refs[n_cast:n_cast + 4] + refs[2 * n_cast + 4:]
    h = _norm_mod(x_ref[...], g_ref[...], sh_ref[0], sc_ref[0]).astype(BF16)
    z = _dot(h, w_ref[...])
    q, k, v, u = (z[:, i * NA_WIDTH:(i + 1) * NA_WIDTH] for i in range(4))
    if ctx_seq is None:
        for o, val in zip(refs, (q, k, v, u)):
            o[...] = val.astype(o.dtype)
        return
    a_ref, u_ref, kc_ref, vc_ref, rows_scr = refs
    a_ref[...] = _ctx_attention_rows(q, k, v, ctx_seq).astype(a_ref.dtype)
    u_ref[...] = u
    tm = z.shape[0]
    pair = 2 * HEAD_DIM
    for o, val in ((kc_ref, k), (vc_ref, v)):
        for j in range(NA_HEADS // 2):
            both = val[:, j * pair:(j + 1) * pair]
            rows_scr[pl.ds(2 * j, tm, stride=NA_HEADS), :] = both
            rows_scr[pl.ds(2 * j + 1, tm, stride=NA_HEADS), :] = pltpu.roll(both, HEAD_DIM, axis=1)
        for b in range(o.shape[0]):
            blk = rows_scr[b * ctx_seq * NA_HEADS:(b + 1) * ctx_seq * NA_HEADS, 0:HEAD_DIM]
            o[b, 0] = blk.reshape(ctx_seq, NA_HEADS, HEAD_DIM)


def _inproj(x, g, mod, layer, row_fn, w, tm, ctx_seq=None, casts=()):
    t = x.shape[0]
    c_in, c_out, c_shape, c_ops = _cast_plan(casts, t // tm)
    flat = pl.BlockSpec((tm, NA_WIDTH), lambda i: (i, 0))
    scratch = []
    if ctx_seq is None:
        out_specs = [flat] * 4
        out_shape = [jax.ShapeDtypeStruct((t, NA_WIDTH), BF16)] + [jax.ShapeDtypeStruct((t, NA_WIDTH), F32)] * 3
    else:
        per = tm // ctx_seq
        cache = pl.BlockSpec((per, 1, ctx_seq, NA_HEADS, HEAD_DIM), lambda i: (i, 0, 0, 0, 0))
        out_specs = [flat, flat, cache, cache]
        out_shape = ([jax.ShapeDtypeStruct((t, NA_WIDTH), BF16), jax.ShapeDtypeStruct((t, NA_WIDTH), F32)]
                     + [jax.ShapeDtypeStruct((t // ctx_seq, 1, ctx_seq, NA_HEADS, HEAD_DIM), F32)] * 2)
        scratch = [pltpu.VMEM((tm * NA_HEADS, 2 * HEAD_DIM), F32)]
    return pl.pallas_call(
        functools.partial(_inproj_kernel, ctx_seq=ctx_seq, n_cast=len(c_ops)),
        grid=(t // tm,),
        in_specs=[pl.BlockSpec((tm, D_MODEL), lambda i: (i, 0)),
                  _full_spec((1, D_MODEL)),
                  _mod_spec(layer, 0, row_fn), _mod_spec(layer, 1, row_fn),
                  _resident_spec(w.shape)] + c_in,
        out_specs=out_specs + c_out,
        out_shape=out_shape + c_shape,
        scratch_shapes=scratch,
        compiler_params=_params(("parallel",)),
        name="inproj_even",
    )(x, g, mod, mod, w, *c_ops)


def _na_bias_kernel(rpb_ref, o_ref, *, rows):
    h = pl.program_id(0)
    kr_win = min(WIN_R, rows)
    qc = lax.broadcasted_iota(jnp.int32, (GRID_W, 2 * GRID_W), 0)
    lane = lax.broadcasted_iota(jnp.int32, (GRID_W, 2 * GRID_W), 1)
    kc = lane & (GRID_W - 1)
    cs = jnp.clip(qc - WIN_C // 2, 0, GRID_W - WIN_C)
    col_ok = (kc >= cs) & (kc < cs + WIN_C)
    dc = kc - qc + (WIN_C - 1)
    left = lane < GRID_W
    neg = jnp.full((GRID_W, 2 * GRID_W), NEG, F32)
    toeplitz = []
    for dr in range(2 * WIN_R - 1):
        t = neg
        for d in range(2 * WIN_C - 1):
            t = jnp.where(dc == d, rpb_ref[h, dr, d], t)
        toeplitz.append(jnp.where(col_ok, t, neg))
    for r in range(rows):
        rs = min(max(r - kr_win // 2, 0), rows - kr_win)

        def blk(kr):
            return toeplitz[kr - r + WIN_R - 1] if rs <= kr < rs + kr_win else None

        for j in range(rows // 2):
            a, b = blk(2 * j), blk(2 * j + 1)
            if a is None and b is None:
                pair = neg
            else:
                pair = jnp.where(left, neg if a is None else a, neg if b is None else b)
            o_ref[0, r * GRID_W:(r + 1) * GRID_W, j * 2 * GRID_W:(j + 1) * 2 * GRID_W] = pair


def _na_bias(rpb, seq):
    rows = seq // GRID_W
    return pl.pallas_call(
        functools.partial(_na_bias_kernel, rows=rows),
        grid=(NA_HEADS,),
        in_specs=[pl.BlockSpec(memory_space=pltpu.SMEM)],
        out_specs=pl.BlockSpec((1, seq, seq), lambda h: (h, 0, 0)),
        out_shape=jax.ShapeDtypeStruct((NA_HEADS, seq, seq), F32),
        compiler_params=_params(("parallel",)),
        name="na_bias",
    )(rpb)


def _na_key_range(qb, tq, rows):
    kr_win = min(WIN_R, rows)
    r0, r1 = qb * tq // GRID_W, ((qb + 1) * tq - 1) // GRID_W
    rs0 = min(max(r0 - kr_win // 2, 0), rows - kr_win)
    rs1 = min(max(r1 - kr_win // 2, 0), rows - kr_win)
    lo, hi = rs0 * GRID_W, (rs1 + kr_win) * GRID_W
    return lo // 128 * 128, -(-hi // 128) * 128


def _na_attn_kernel(q_ref, k_ref, v_ref, ck_ref, cv_ref, bias_ref, o_ref, *, tq):
    scale = HEAD_DIM ** -0.5
    seq = q_ref.shape[1]
    probs = []
    for hl in range(2):
        sl = slice(hl * HEAD_DIM, (hl + 1) * HEAD_DIM)
        for qb in range(seq // tq):
            lo, hi = _na_key_range(qb, tq, seq // GRID_W)
            probs.append((hl, sl, slice(qb * tq, (qb + 1) * tq), slice(lo, hi)))
    head0 = pl.program_id(0) * 2
    cks = [ck_ref[0, 0, :, head0 + hl, :].astype(BF16) for hl in range(2)]
    cvs = [cv_ref[0, 0, :, head0 + hl, :].astype(BF16) for hl in range(2)]
    qs_all = [(q_ref[0, qs, sl] * scale).astype(BF16) for _, sl, qs, _ in probs]
    s_loc = [_dot_nt(q, k_ref[0, ks, sl].astype(BF16)) + bias_ref[hl, qs, ks]
             for q, (hl, sl, qs, ks) in zip(qs_all, probs)]
    s_ctx = [_dot_nt(q, cks[hl]) for q, (hl, _, _, _) in zip(qs_all, probs)]
    ms = [jnp.maximum(jnp.max(a, axis=-1, keepdims=True), jnp.max(b, axis=-1, keepdims=True))
          for a, b in zip(s_loc, s_ctx)]
    p_loc = [jnp.exp(a - m) for a, m in zip(s_loc, ms)]
    p_ctx = [jnp.exp(b - m) for b, m in zip(s_ctx, ms)]
    dens = [jnp.sum(a, axis=-1, keepdims=True) + jnp.sum(b, axis=-1, keepdims=True) for a, b in zip(p_loc, p_ctx)]
    for a, b, den, (hl, sl, qs, ks) in zip(p_loc, p_ctx, dens, probs):
        o = _dot(a.astype(BF16), v_ref[0, ks, sl].astype(BF16)) + _dot(b.astype(BF16), cvs[hl])
        o_ref[0, qs, sl] = o / den


def _na_attention(q, k, v, ck, cv, layer, bias):
    nb, seq, _ = q.shape
    lat = pl.BlockSpec((1, seq, 2 * HEAD_DIM), lambda hp, b: (b, 0, hp))
    ctx = pl.BlockSpec((1, 1) + ck.shape[2:], lambda hp, b: (b, layer, 0, 0, 0))
    return pl.pallas_call(
        functools.partial(_na_attn_kernel, tq=256),
        grid=(NA_HEADS // 2, nb),
        in_specs=[lat, lat, lat, ctx, ctx, pl.BlockSpec((2, seq, seq), lambda hp, b: (hp, 0, 0))],
        out_specs=lat,
        out_shape=jax.ShapeDtypeStruct((nb, seq, NA_WIDTH), F32),
        compiler_params=_params(("parallel", "arbitrary")),
        name="na_attention",
    )(q, k, v, ck, cv, bias)


def _s5_prep_group(i, lamr_ref, lami_ref, logdt_ref, btr_ref, bti_ref, cr_ref, ci_ref,
                   m_ref, wz_ref, wyt_ref, ar_ref, ai_ref):
    lr = jnp.minimum(lamr_ref[i], -1e-4)
    li = lami_ref[i]
    dt = jnp.exp(logdt_ref[i])
    a = lr * dt
    th = li * dt
    mag = jnp.exp(a)
    lbr = mag * jnp.cos(th)
    lbi = mag * jnp.sin(th)
    den = lr * lr + li * li
    nr = lbr - 1.0
    coef_r = (nr * lr + lbi * li) / den
    coef_i = (lbi * lr - nr * li) / den
    btr, bti = btr_ref[i], bti_ref[i]
    bbr = coef_r * btr - coef_i * bti
    bbi = coef_r * bti + coef_i * btr
    cr, ci = cr_ref[i], ci_ref[i]

    step = lax.broadcasted_iota(jnp.int32, (CHUNK, 2 * S5_STATE), 0).astype(F32)
    fwd = lax.broadcasted_iota(jnp.int32, (CHUNK, 2 * S5_STATE), 1) < S5_STATE

    def powers(e):
        pm = jnp.exp(a * e)
        return pm * jnp.cos(th * e), pm * jnp.sin(th * e)

    pzr, pzi = powers(jnp.where(fwd, CHUNK - 1 - step, step))
    pyr, pyi = powers(jnp.where(fwd, step + 1.0, CHUNK - step))
    p16r, p16i = powers(jnp.full((1, 2 * S5_STATE), float(CHUNK), F32))
    ar_ref[i] = p16r
    ai_ref[i] = p16i

    def outer(p, c):
        return (p[:, None, :] * c[None, :, :]).reshape(CW, 2 * S5_STATE)

    wzr = outer(pzr, bbr) - outer(pzi, bbi)
    wzi = outer(pzr, bbi) + outer(pzi, bbr)
    wz_ref[i, :, 0:128] = wzr.astype(wz_ref.dtype)
    wz_ref[i, :, 128:256] = wzi.astype(wz_ref.dtype)
    wyt_ref[i, :, 0:128] = (outer(pyr, cr) - outer(pyi, ci)).astype(wyt_ref.dtype)
    wyt_ref[i, :, 128:256] = (-(outer(pyi, cr) + outer(pyr, ci))).astype(wyt_ref.dtype)

    ones = jnp.ones((CHUNK, 2 * S5_STATE), F32)
    cwr = outer(ones, cr)
    cwi = outer(ones, ci)
    fwd_rows = lax.broadcasted_iota(jnp.int32, (CW, 2 * S5_STATE), 1) < S5_STATE
    zero = jnp.zeros((CW, 2 * S5_STATE), F32)

    def split(x):
        hi = x.astype(BF16)
        return hi, (x - hi.astype(F32)).astype(BF16)

    cw_hi, cw_lo = split(jnp.concatenate([cwr, -cwi], axis=1))

    def kk(sel):
        lhs_hi, lhs_lo = split(jnp.concatenate([jnp.where(sel, wzr, zero), jnp.where(sel, wzi, zero)], axis=1))
        return _dot_nt(lhs_hi, cw_hi) + (_dot_nt(lhs_hi, cw_lo) + _dot_nt(lhs_lo, cw_hi))

    kk_f = kk(fwd_rows)
    kk_b = kk(jnp.logical_not(fwd_rows))
    lanes = 2 * S5_STATE
    steps_per_tile = lanes // S5_GROUP
    col_step = lax.broadcasted_iota(jnp.int32, (CW, lanes), 1) // S5_GROUP
    for tile in range(CW // lanes):
        cols = slice(tile * lanes, (tile + 1) * lanes)
        kf, kb = kk_f[:, cols], kk_b[:, cols]
        m = jnp.zeros((CW, lanes), F32)
        for tt in range(steps_per_tile):
            t = tile * steps_per_tile + tt
            up = (CHUNK - 1 - t) * S5_GROUP
            dn = t * S5_GROUP
            piece_f = kf if up == 0 else jnp.concatenate([kf[up:], jnp.zeros((up, lanes), F32)], axis=0)
            piece_b = kb if dn == 0 else jnp.concatenate([jnp.zeros((dn, lanes), F32), kb[:CW - dn]], axis=0)
            m = jnp.where(col_step == tt, piece_f + piece_b, m)
        m_ref[i, :, cols] = m.astype(m_ref.dtype)


def _s5_prep_kernel(*refs):
    for i in range(refs[0].shape[0]):
        _s5_prep_group(i, *refs)


def _s5_prep(lam_re, lam_im, log_step, b_re, b_im, c_re, c_im):
    g = S5_GROUPS
    lanes = 2 * S5_STATE

    def dirs_last(x):
        return x.transpose(1, 0, 2).reshape(g, 1, lanes)

    lamr, lami = dirs_last(lam_re), dirs_last(lam_im)
    logdt = dirs_last(jnp.broadcast_to(log_step[:, :, None], (2, g, S5_STATE)))
    bt = [x.transpose(1, 3, 0, 2).reshape(g, S5_GROUP, lanes) for x in (b_re, b_im)]
    ct = [x.transpose(1, 2, 0, 3).reshape(g, S5_GROUP, lanes) for x in (c_re, c_im)]
    gp = 4
    vec = pl.BlockSpec((gp, 1, lanes), lambda i: (i, 0, 0))
    mat = pl.BlockSpec((gp, S5_GROUP, lanes), lambda i: (i, 0, 0))
    big = pl.BlockSpec((gp, CW, CW), lambda i: (i, 0, 0))
    return pl.pallas_call(
        _s5_prep_kernel,
        grid=(g // gp,),
        in_specs=[vec, vec, vec, mat, mat, mat, mat],
        out_specs=[big, big, big, vec, vec],
        out_shape=[jax.ShapeDtypeStruct((g, CW, CW), BF16)] * 3 + [jax.ShapeDtypeStruct((g, 1, lanes), F32)] * 2,
        compiler_params=_params(("parallel",)),
        name="s5_prep",
    )(lamr, lami, logdt, bt[0], bt[1], ct[0], ct[1])


def _s5_kernel(u_ref, m_ref, wz_ref, wyt_ref, ar_ref, ai_ref, s0_ref, perm_ref, permt_ref, y_ref, fin_ref,
               xt_scr, z_scr, sp_scr, *, nbp, nc):
    s = S5_STATE
    r = xt_scr.shape[2]
    for t in range(CHUNK):
        xt_scr[t] = u_ref[pl.ds(t, r, stride=CHUNK), :].T
    fwd = lax.broadcasted_iota(jnp.int32, (nbp, 2 * s), 1) < s
    groups = range(GROUPS_PER_BLOCK)
    chans = [slice(gl * S5_GROUP, (gl + 1) * S5_GROUP) for gl in groups]

    us = [jnp.concatenate([xt_scr[t, ch, :] for t in range(CHUNK)], axis=0).T.astype(BF16) for ch in chans]
    for gl in groups:
        z_scr[gl] = _dot(_dot(perm_ref[...], us[gl]).astype(BF16), wz_ref[gl])

    ars = [jnp.broadcast_to(ar_ref[gl], (nbp, 2 * s)) for gl in groups]
    ais = [jnp.broadcast_to(ai_ref[gl], (nbp, 2 * s)) for gl in groups]
    srs = [s0_ref[gl, :, 0:2 * s] for gl in groups]
    sis = [s0_ref[gl, :, 2 * s:4 * s] for gl in groups]
    for k in range(nc):
        up = slice(k * nbp, (k + 1) * nbp)
        dn = slice((nc - 1 - k) * nbp, (nc - k) * nbp)
        for gl in groups:
            sr, si = srs[gl], sis[gl]
            sp_scr[gl, up, 0:s] = sr[:, 0:s]
            sp_scr[gl, dn, s:2 * s] = sr[:, s:2 * s]
            sp_scr[gl, up, 2 * s:3 * s] = si[:, 0:s]
            sp_scr[gl, dn, 3 * s:4 * s] = si[:, s:2 * s]
            zr = jnp.where(fwd, z_scr[gl, up, 0:2 * s], z_scr[gl, dn, 0:2 * s])
            zi = jnp.where(fwd, z_scr[gl, up, 2 * s:4 * s], z_scr[gl, dn, 2 * s:4 * s])
            srs[gl] = ars[gl] * sr - ais[gl] * si + zr
            sis[gl] = ars[gl] * si + ais[gl] * sr + zi
    for gl in groups:
        fin_ref[gl, :, 0:2 * s] = srs[gl]
        fin_ref[gl, :, 2 * s:4 * s] = sis[gl]
    sps = [_dot(permt_ref[...], sp_scr[gl].astype(BF16)).astype(BF16) for gl in groups]
    for gl in groups:
        yt = (_dot(us[gl], m_ref[gl]) + _dot_nt(sps[gl], wyt_ref[gl])).T
        for t in range(CHUNK):
            xt_scr[t, chans[gl], :] = yt[t * S5_GROUP:(t + 1) * S5_GROUP, :]

    for t in range(CHUNK):
        y_ref[pl.ds(t, r, stride=CHUNK), :] = xt_scr[t].T


def _s5_mixer(u, s0, ops, nb, seq):
    m, wz, wyt, ar, ai = ops
    g, nc = S5_GROUPS, seq // CHUNK
    nbp = -(-nb // 8) * 8
    r, rp = nb * nc, nc * nbp
    gpb = GROUPS_PER_BLOCK
    s0l = jnp.zeros((g, nbp, CW), F32)
    if s0 is not None:
        s0l = s0l.at[:, :nb].set(s0.transpose(3, 0, 2, 1, 4).reshape(g, nb, CW))
    perm = np.zeros((nc, nbp, nb, nc), np.float32)
    for b in range(nb):
        perm[np.arange(nc), b, b, np.arange(nc)] = 1.0
    perm = jnp.asarray(perm.reshape(rp, r)).astype(BF16)
    col = pl.BlockSpec((nb * seq, gpb * S5_GROUP), lambda j: (0, j))
    wspec = pl.BlockSpec((gpb, CW, CW), lambda j: (j, 0, 0))
    cspec = pl.BlockSpec((gpb, 1, 2 * S5_STATE), lambda j: (j, 0, 0))
    sspec = pl.BlockSpec((gpb, nbp, CW), lambda j: (j, 0, 0))
    y, fin = pl.pallas_call(
        functools.partial(_s5_kernel, nbp=nbp, nc=nc),
        grid=(g // gpb,),
        in_specs=[col, wspec, wspec, wspec, cspec, cspec, sspec, _full_spec((rp, r)), _full_spec((r, rp))],
        out_specs=[col, sspec],
        out_shape=[jax.ShapeDtypeStruct((nb * seq, S5_WIDTH), F32), jax.ShapeDtypeStruct((g, nbp, CW), F32)],
        scratch_shapes=[pltpu.VMEM((CHUNK, gpb * S5_GROUP, r), F32),
                        pltpu.VMEM((gpb, rp, CW), F32), pltpu.VMEM((gpb, rp, CW), F32)],
        compiler_params=_params(("parallel",)),
        name="s5_scan",
    )(u, m, wz, wyt, ar, ai, s0l, perm, perm.T)
    state = fin[:, :nb].reshape(g, nb, 2, 2, S5_STATE).transpose(1, 3, 2, 0, 4)
    return y, state


FFN_ROWS = 512
FFN_CHUNK = 256


def _ffn_rows(x, g, shift, scale, gate, wg_ref, wu_ref, wd_ref):
    h = _norm_mod(x, g, shift, scale).astype(BF16)
    acc = jnp.zeros(x.shape, F32)
    for j in range(D_FF // FFN_CHUNK):
        sl = slice(j * FFN_CHUNK, (j + 1) * FFN_CHUNK)
        gg = _dot(h, wg_ref[:, sl])
        uu = _dot(h, wu_ref[:, sl])
        act = (gg * jax.nn.sigmoid(gg) * uu).astype(BF16)
        acc = acc + _dot(act, wd_ref[sl, :])
    return x + gate * acc


def _ffn_specs(layer, row_fn, wg, wu, wd):
    return [_full_spec((1, D_MODEL)),
            _mod_spec(layer, 3, row_fn), _mod_spec(layer, 4, row_fn), _mod_spec(layer, 5, row_fn),
            _resident_spec(wg.shape), _resident_spec(wu.shape), _resident_spec(wd.shape)]


def _even_tail_kernel(a_ref, y_ref, u_ref, x_ref, d_ref, gw_ref, gb_ref, wo_ref, gate_ref,
                      g2_ref, sh2_ref, sc2_ref, gate2_ref, wg_ref, wu_ref, wd_ref, *refs):
    n_cast = len(refs) // 2
    o_ref = refs[n_cast]
    _run_casts(refs[:n_cast], refs[n_cast + 1:])
    y = jax.nn.gelu(y_ref[...] + d_ref[...] * u_ref[...])
    y = y * jax.nn.sigmoid(_dot(y.astype(BF16), gw_ref[...]) + gb_ref[...])
    o = _dot(a_ref[...].astype(BF16), wo_ref[0:NA_WIDTH, :]) + _dot(y.astype(BF16), wo_ref[NA_WIDTH:, :])
    x = x_ref[...] + gate_ref[0] * o
    o_ref[...] = _ffn_rows(x, g2_ref[...], sh2_ref[0], sc2_ref[0], gate2_ref[0], wg_ref, wu_ref, wd_ref)


def _even_tail(a, y, u, x, d, glu_w, glu_b, w_out, g_ffn, wg, wu, wd, mod, layer, row_fn, casts=()):
    t = x.shape[0]
    tm = FFN_ROWS
    c_in, c_out, c_shape, c_ops = _cast_plan(casts, t // tm)
    half = pl.BlockSpec((tm, NA_WIDTH), lambda i: (i, 0))
    full = pl.BlockSpec((tm, D_MODEL), lambda i: (i, 0))
    return pl.pallas_call(
        _even_tail_kernel,
        grid=(t // tm,),
        in_specs=[half, half, half, full, _full_spec((1, S5_WIDTH)), _resident_spec(glu_w.shape),
                  _full_spec((1, S5_WIDTH)), _resident_spec(w_out.shape), _mod_spec(layer, 2, row_fn)]
                 + _ffn_specs(layer, row_fn, wg, wu, wd) + c_in,
        out_specs=[full] + c_out,
        out_shape=[jax.ShapeDtypeStruct((t, D_MODEL), F32)] + c_shape,
        compiler_params=_params(("parallel",)),
        name="even_tail",
    )(a, y, u, x, d, glu_w, glu_b, w_out, mod, g_ffn, mod, mod, mod, wg, wu, wd, *c_ops)


def _dft_consts(seq):
    def cs(n):
        k = np.arange(n)
        ang = 2.0 * np.pi * ((k[:, None] * k[None, :]) % n) / n
        return np.cos(ang), np.sin(ang)

    cc, sc = cs(FNET_GROUP_WIDTH)
    cl, sl = cs(seq)
    chan = jnp.asarray(np.concatenate([cc, sc], axis=1), dtype=F32)
    pos = jnp.asarray(np.concatenate([cl, -sl], axis=1), dtype=F32)
    return chan.astype(BF16), pos.astype(BF16)


def _fourier_rows(x, g, shift, scale, gate, win_ref, chan_ref, pos_ref, wout_ref):
    seq = x.shape[0]
    h = _norm_mod(x, g, shift, scale).astype(BF16)
    z = _dot(h, win_ref[...]).astype(BF16)
    w = FNET_GROUP_WIDTH
    pos = pos_ref[...]
    parts = []
    for c in range(FNET_GROUPS):
        ab = _dot(z[:, c * w:(c + 1) * w], chan_ref[...]).astype(BF16)
        stacked = jnp.concatenate([ab[:, :w], ab[:, w:]], axis=0)
        parts.append(_dot(pos, stacked))
    f = jnp.concatenate(parts, axis=1) * (1.0 / math.sqrt(seq * w))
    return x + gate * _dot(f.astype(BF16), wout_ref[...])


def _odd_layer_kernel(x_ref, g_ref, sh_ref, sc_ref, win_ref, chan_ref, pos_ref, wout_ref, gate_ref,
                      g2_ref, sh2_ref, sc2_ref, gate2_ref, wg_ref, wu_ref, wd_ref, fg_ref, o_ref, *, seq):
    tm = x_ref.shape[0]
    for r0 in range(0, tm, seq):
        rows = slice(r0, r0 + seq)
        o_ref[rows, :] = _fourier_rows(x_ref[rows, :], g_ref[...], sh_ref[0], sc_ref[0], gate_ref[0],
                                       win_ref, chan_ref, pos_ref, wout_ref)
    for r0 in range(0, tm, FFN_ROWS):
        rows = slice(r0, r0 + FFN_ROWS)
        y = _ffn_rows(o_ref[rows, :], g2_ref[...], sh2_ref[0], sc2_ref[0], gate2_ref[0], wg_ref, wu_ref, wd_ref)
        ms = jnp.mean(y * y, axis=-1, keepdims=True)
        o_ref[rows, :] = y * lax.rsqrt(ms + EPS) * fg_ref[...]


def _odd_layer(x, g_mix, g_ffn, final_g, mod, layer, row_of, w_in, w_out, wg, wu, wd, seq):
    t = x.shape[0]
    tm = max(seq, FFN_ROWS)
    row_fn = row_of(tm)
    chan, pos = _dft_consts(seq)
    blk = pl.BlockSpec((tm, D_MODEL), lambda i: (i, 0))
    return pl.pallas_call(
        functools.partial(_odd_layer_kernel, seq=seq),
        grid=(t // tm,),
        in_specs=[blk, _full_spec((1, D_MODEL)), _mod_spec(layer, 0, row_fn), _mod_spec(layer, 1, row_fn),
                  _resident_spec(w_in.shape), _resident_spec(chan.shape), _resident_spec(pos.shape),
                  _resident_spec(w_out.shape), _mod_spec(layer, 2, row_fn)]
                 + _ffn_specs(layer, row_fn, wg, wu, wd) + [_full_spec((1, D_MODEL))],
        out_specs=blk,
        out_shape=jax.ShapeDtypeStruct((t, D_MODEL), F32),
        compiler_params=_params(("parallel",)),
        name="odd_layer",
    )(x, g_mix, mod, mod, w_in, chan, pos, w_out, mod, g_ffn, mod, mod, mod, wg, wu, wd, final_g)


def kernel(x_prompt, x_sample, cache_na_k, cache_na_v, state_s5, c, c_ctx, w_mod, b_mod, norm_mix_g, norm_ffn_g, w_in_even, na_rpb, s5_lam_re, s5_lam_im, s5_log_step, s5_b_re, s5_b_im, s5_c_re, s5_c_im, s5_d, s5_glu_w, s5_glu_b, w_out_even, w_in_odd, w_out_odd, ffn_w_gate, ffn_w_up, ffn_w_down, final_norm_g):
    bp, lp, _ = x_prompt.shape
    bs, ls, _ = x_sample.shape
    depth = w_mod.shape[0]
    assert bs + 1 <= N_COND and depth == 2

    cond = jnp.zeros((COND_ROWS, D_MODEL), F32).at[0].set(c_ctx).at[1:1 + bs].set(c)
    mod = _modulation(cond.T, w_mod, b_mod).reshape(depth * COND_ROWS * 6, 1, D_MODEL)

    streams = [
        dict(x=x_prompt.reshape(bp * lp, D_MODEL), nb=bp, seq=lp, row_of=lambda tm: (lambda i: 0)),
        dict(x=x_sample.reshape(bs * ls, D_MODEL), nb=bs, seq=ls, row_of=lambda tm: (lambda i: 1 + (i * tm) // ls)),
    ]
    tm = FFN_ROWS

    e = 0
    w_in = w_in_even[e].astype(BF16)
    glu_b = s5_glu_b[e].reshape(1, S5_WIDTH)
    g_mix = norm_mix_g[0].reshape(1, D_MODEL)
    g_ffn0 = norm_ffn_g[0].reshape(1, D_MODEL)
    s5_ops = _s5_prep(s5_lam_re[e], s5_lam_im[e], s5_log_step[e], s5_b_re[e], s5_b_im[e], s5_c_re[e], s5_c_im[e])
    s5_dvec = s5_d[e].reshape(1, S5_WIDTH)
    bias = _na_bias(na_rpb[e], ls)
    layer0_w = [(ffn_w_gate, 0), (ffn_w_up, 0), (ffn_w_down, 0), (w_out_even, e), (s5_glu_w, e)]
    layer1_w = [(ffn_w_gate, 1), (ffn_w_up, 1), (ffn_w_down, 1), (w_in_odd, 0), (w_out_odd, 0)]

    new_k = new_v = new_s = None
    for si, st in enumerate(streams):
        nb, seq = st["nb"], st["seq"]
        if si == 0:
            a, u, new_k, new_v, *layer0_w = _inproj(st["x"], g_mix, mod, 0, st["row_of"](tm), w_in, tm,
                                                    ctx_seq=seq, casts=layer0_w)
            wg0, wu0, wd0, w_out, glu_w = layer0_w
            s0 = None
        else:
            q, k, v, u = _inproj(st["x"], g_mix, mod, 0, st["row_of"](tm), w_in, tm)
            a = _na_attention(q.reshape(nb, seq, NA_WIDTH), k.reshape(nb, seq, NA_WIDTH), v.reshape(nb, seq, NA_WIDTH),
                              cache_na_k, cache_na_v, e, bias).reshape(nb * seq, NA_WIDTH)
            s0 = state_s5[:, e].astype(F32)
        y, state = _s5_mixer(u, s0, s5_ops, nb, seq)
        if si == 0:
            new_s = state[:, None]
        tail = _even_tail(a, y, u, st["x"], s5_dvec, glu_w, glu_b, w_out, g_ffn0, wg0, wu0, wd0,
                          mod, 0, st["row_of"](tm), casts=layer1_w if si == 0 else ())
        st["x"] = tail[0]
        if si == 0:
            wg1, wu1, wd1, w_in1, w_out1 = tail[1:]

    for st in streams:
        st["x"] = _odd_layer(st["x"], norm_mix_g[1].reshape(1, D_MODEL), norm_ffn_g[1].reshape(1, D_MODEL),
                             final_norm_g.reshape(1, D_MODEL), mod, 1, st["row_of"],
                             w_in1, w_out1, wg1, wu1, wd1, st["seq"])

    y_prompt = streams[0]["x"].reshape(bp, lp, D_MODEL)
    y_sample = streams[1]["x"].reshape(bs, ls, D_MODEL)
    return (y_prompt, y_sample, new_k, new_v, new_s)
```

```python
import functools
import math

import numpy as np
import jax
import jax.numpy as jnp
from jax import lax
from jax.experimental import pallas as pl
from jax.experimental.pallas import tpu as pltpu

F32 = jnp.float32
BF16 = jnp.bfloat16

D_MODEL = 1024
NA_HEADS = 8
HEAD_DIM = 64
NA_WIDTH = NA_HEADS * HEAD_DIM
GRID_W = 64
WIN_R = 8
WIN_C = 16
S5_GROUP = 16
S5_GROUPS = 32
S5_STATE = 64
S5_WIDTH = S5_GROUPS * S5_GROUP
FNET_GROUPS = 4
FNET_GROUP_WIDTH = D_MODEL // FNET_GROUPS
D_FF = 2816
EPS = 1e-6

CHUNK = 16
CW = CHUNK * S5_GROUP
GROUPS_PER_BLOCK = 128 // S5_GROUP
N_COND = 3
COND_ROWS = 8
NEG = -1e30
VMEM_LIMIT = 56 * 2 ** 20

_NT = (((1,), (1,)), ((), ()))


def _params(sem, vmem=VMEM_LIMIT):
    return pltpu.CompilerParams(dimension_semantics=sem, vmem_limit_bytes=vmem)


def _dot(a, b):
    return jnp.dot(a, b, preferred_element_type=F32)


def _dot_nt(a, b):
    return lax.dot_general(a, b, _NT, preferred_element_type=F32)


def _norm_mod(x, g, shift, scale):
    ms = jnp.mean(x * x, axis=-1, keepdims=True)
    return (x * lax.rsqrt(ms + EPS) * g) * (1.0 + scale) + shift


def _mod_spec(layer, which, row_fn):
    return pl.BlockSpec((1, 1, D_MODEL),
                        lambda *idx: (layer * COND_ROWS * 6 + row_fn(*idx) * 6 + which, 0, 0))


def _full_spec(shape):
    nd = len(shape)
    return pl.BlockSpec(shape, lambda *idx: (0,) * nd)


def _resident_spec(shape):
    nd = len(shape)
    return pl.BlockSpec(shape, lambda *idx: (0,) * nd, pipeline_mode=pl.Buffered(1))


def _cast_plan(weights, steps):
    in_specs, out_specs, out_shape, operands = [], [], [], []
    for w in weights:
        w, layer = w if isinstance(w, tuple) else (w, None)
        rows, cols = w.shape[-2:]
        slab = rows // steps
        if layer is None:
            in_specs.append(pl.BlockSpec((slab, cols), lambda i: (i, 0)))
        else:
            in_specs.append(pl.BlockSpec((None, slab, cols), lambda i, layer=layer: (layer, i, 0)))
        out_specs.append(pl.BlockSpec((slab, cols), lambda i: (i, 0)))
        out_shape.append(jax.ShapeDtypeStruct((rows, cols), BF16))
        operands.append(w)
    return in_specs, out_specs, out_shape, operands


def _run_casts(src_refs, dst_refs):
    for s, d in zip(src_refs, dst_refs):
        d[...] = s[...].astype(d.dtype)


MOD_COLS = 2048


def _mod_kernel(ct_ref, w_ref, b_ref, o_ref):
    ct = ct_ref[...]
    st = ct * jax.nn.sigmoid(ct)
    w = w_ref[0]
    o_ref[0] = jnp.zeros(o_ref.shape[1:], F32)
    for r in range(N_COND):
        o_ref[0, r:r + 1, :] = jnp.sum(w * st[:, r:r + 1], axis=0, keepdims=True) + b_ref[0]


def _modulation(cond_t, w_mod, b_mod):
    depth, _, n = w_mod.shape
    tn = MOD_COLS
    return pl.pallas_call(
        _mod_kernel,
        grid=(depth, n // tn),
        in_specs=[_full_spec((D_MODEL, COND_ROWS)),
                  pl.BlockSpec((1, D_MODEL, tn), lambda l, j: (l, 0, j)),
                  pl.BlockSpec((1, 1, tn), lambda l, j: (l, 0, j))],
        out_specs=pl.BlockSpec((1, COND_ROWS, tn), lambda l, j: (l, 0, j)),
        out_shape=jax.ShapeDtypeStruct((depth, COND_ROWS, n), F32),
        compiler_params=_params(("parallel", "parallel")),
        name="modulation",
    )(cond_t, w_mod, b_mod.reshape(depth, 1, n))


def _ctx_attention_rows(q, k, v, seq):
    scale = HEAD_DIM ** -0.5
    nseq = q.shape[0] // seq
    qt = (q * scale).T.astype(BF16)
    vt = v.T.astype(BF16)
    kb = k.astype(BF16)
    probs = [(slice(i * seq, (i + 1) * seq), slice(h * HEAD_DIM, (h + 1) * HEAD_DIM))
             for i in range(nseq) for h in range(NA_HEADS)]
    sts = [_dot(kb[rows, sl], qt[sl, rows]) for rows, sl in probs]
    pts = [jnp.exp(st - jnp.max(st, axis=0, keepdims=True)) for st in sts]
    dens = [jnp.sum(pt, axis=0, keepdims=True) for pt in pts]
    outs = [_dot(vt[sl, rows], pt.astype(BF16)) / den for (rows, sl), pt, den in zip(probs, pts, dens)]
    per_seq = [jnp.concatenate(outs[i * NA_HEADS:(i + 1) * NA_HEADS], axis=0).T for i in range(nseq)]
    return jnp.concatenate(per_seq, axis=0)


def _inproj_kernel(x_ref, g_ref, sh_ref, sc_ref, w_ref, *refs, ctx_seq, n_cast):
    _run_casts(refs[:n_cast], refs[n_cast + 4:2 * n_cast + 4])
    refs = refs[n_cast:n_cast + 4] + refs[2 * n_cast + 4:]
    h = _norm_mod(x_ref[...], g_ref[...], sh_ref[0], sc_ref[0]).astype(BF16)
    z = _dot(h, w_ref[...])
    q, k, v, u = (z[:, i * NA_WIDTH:(i + 1) * NA_WIDTH] for i in range(4))
    if ctx_seq is None:
        for o, val in zip(refs, (q, k, v, u)):
            o[...] = val.astype(o.dtype)
        return
    a_ref, u_ref, kc_ref, vc_ref, rows_scr = refs
    a_ref[...] = _ctx_attention_rows(q, k, v, ctx_seq).astype(a_ref.dtype)
    u_ref[...] = u
    tm = z.shape[0]
    pair = 2 * HEAD_DIM
    for o, val in ((kc_ref, k), (vc_ref, v)):
        for j in range(NA_HEADS // 2):
            both = val[:, j * pair:(j + 1) * pair]
            rows_scr[pl.ds(2 * j, tm, stride=NA_HEADS), :] = both
            rows_scr[pl.ds(2 * j + 1, tm, stride=NA_HEADS), :] = pltpu.roll(both, HEAD_DIM, axis=1)
        for b in range(o.shape[0]):
            blk = rows_scr[b * ctx_seq * NA_HEADS:(b + 1) * ctx_seq * NA_HEADS, 0:HEAD_DIM]
            o[b, 0] = blk.reshape(ctx_seq, NA_HEADS, HEAD_DIM)


def _inproj(x, g, mod, layer, row_fn, w, tm, ctx_seq=None, casts=()):
    t = x.shape[0]
    c_in, c_out, c_shape, c_ops = _cast_plan(casts, t // tm)
    flat = pl.BlockSpec((tm, NA_WIDTH), lambda i: (i, 0))
    scratch = []
    if ctx_seq is None:
        out_specs = [flat] * 4
        out_shape = [jax.ShapeDtypeStruct((t, NA_WIDTH), BF16)] + [jax.ShapeDtypeStruct((t, NA_WIDTH), F32)] * 3
    else:
        per = tm // ctx_seq
        cache = pl.BlockSpec((per, 1, ctx_seq, NA_HEADS, HEAD_DIM), lambda i: (i, 0, 0, 0, 0))
        out_specs = [flat, flat, cache, cache]
        out_shape = ([jax.ShapeDtypeStruct((t, NA_WIDTH), BF16), jax.ShapeDtypeStruct((t, NA_WIDTH), F32)]
                     + [jax.ShapeDtypeStruct((t // ctx_seq, 1, ctx_seq, NA_HEADS, HEAD_DIM), F32)] * 2)
        scratch = [pltpu.VMEM((tm * NA_HEADS, 2 * HEAD_DIM), F32)]
    return pl.pallas_call(
        functools.partial(_inproj_kernel, ctx_seq=ctx_seq, n_cast=len(c_ops)),
        grid=(t // tm,),
        in_specs=[pl.BlockSpec((tm, D_MODEL), lambda i: (i, 0)),
                  _full_spec((1, D_MODEL)),
                  _mod_spec(layer, 0, row_fn), _mod_spec(layer, 1, row_fn),
                  _resident_spec(w.shape)] + c_in,
        out_specs=out_specs + c_out,
        out_shape=out_shape + c_shape,
        scratch_shapes=scratch,
        compiler_params=_params(("parallel",)),
        name="inproj_even",
    )(x, g, mod, mod, w, *c_ops)


def _na_bias_kernel(rpb_ref, o_ref, *, rows):
    h = pl.program_id(0)
    kr_win = min(WIN_R, rows)
    qc = lax.broadcasted_iota(jnp.int32, (GRID_W, 2 * GRID_W), 0)
    lane = lax.broadcasted_iota(jnp.int32, (GRID_W, 2 * GRID_W), 1)
    kc = lane & (GRID_W - 1)
    cs = jnp.clip(qc - WIN_C // 2, 0, GRID_W - WIN_C)
    col_ok = (kc >= cs) & (kc < cs + WIN_C)
    dc = kc - qc + (WIN_C - 1)
    left = lane < GRID_W
    neg = jnp.full((GRID_W, 2 * GRID_W), NEG, F32)
    toeplitz = []
    for dr in range(2 * WIN_R - 1):
        t = neg
        for d in range(2 * WIN_C - 1):
            t = jnp.where(dc == d, rpb_ref[h, dr, d], t)
        toeplitz.append(jnp.where(col_ok, t, neg))
    for r in range(rows):
        rs = min(max(r - kr_win // 2, 0), rows - kr_win)

        def blk(kr):
            return toeplitz[kr - r + WIN_R - 1] if rs <= kr < rs + kr_win else None

        for j in range(rows // 2):
            a, b = blk(2 * j), blk(2 * j + 1)
            if a is None and b is None:
                pair = neg
            else:
                pair = jnp.where(left, neg if a is None else a, neg if b is None else b)
            o_ref[0, r * GRID_W:(r + 1) * GRID_W, j * 2 * GRID_W:(j + 1) * 2 * GRID_W] = pair


def _na_bias(rpb, seq):
    rows = seq // GRID_W
    return pl.pallas_call(
        functools.partial(_na_bias_kernel, rows=rows),
        grid=(NA_HEADS,),
        in_specs=[pl.BlockSpec(memory_space=pltpu.SMEM)],
        out_specs=pl.BlockSpec((1, seq, seq), lambda h: (h, 0, 0)),
        out_shape=jax.ShapeDtypeStruct((NA_HEADS, seq, seq), F32),
        compiler_params=_params(("parallel",)),
        name="na_bias",
    )(rpb)


def _na_key_range(qb, tq, rows):
    kr_win = min(WIN_R, rows)
    r0, r1 = qb * tq // GRID_W, ((qb + 1) * tq - 1) // GRID_W
    rs0 = min(max(r0 - kr_win // 2, 0), rows - kr_win)
    rs1 = min(max(r1 - kr_win // 2, 0), rows - kr_win)
    lo, hi = rs0 * GRID_W, (rs1 + kr_win) * GRID_W
    return lo // 128 * 128, -(-hi // 128) * 128


def _na_attn_kernel(q_ref, k_ref, v_ref, ck_ref, cv_ref, bias_ref, o_ref, *, tq):
    scale = HEAD_DIM ** -0.5
    seq = q_ref.shape[1]
    probs = []
    for hl in range(2):
        sl = slice(hl * HEAD_DIM, (hl + 1) * HEAD_DIM)
        for qb in range(seq // tq):
            lo, hi = _na_key_range(qb, tq, seq // GRID_W)
            probs.append((hl, sl, slice(qb * tq, (qb + 1) * tq), slice(lo, hi)))
    head0 = pl.program_id(0) * 2
    cks = [ck_ref[0, 0, :, head0 + hl, :].astype(BF16) for hl in range(2)]
    cvs = [cv_ref[0, 0, :, head0 + hl, :].astype(BF16) for hl in range(2)]
    qs_all = [(q_ref[0, qs, sl] * scale).astype(BF16) for _, sl, qs, _ in probs]
    s_loc = [_dot_nt(q, k_ref[0, ks, sl].astype(BF16)) + bias_ref[hl, qs, ks]
             for q, (hl, sl, qs, ks) in zip(qs_all, probs)]
    s_ctx = [_dot_nt(q, cks[hl]) for q, (hl, _, _, _) in zip(qs_all, probs)]
    ms = [jnp.maximum(jnp.max(a, axis=-1, keepdims=True), jnp.max(b, axis=-1, keepdims=True))
          for a, b in zip(s_loc, s_ctx)]
    p_loc = [jnp.exp(a - m) for a, m in zip(s_loc, ms)]
    p_ctx = [jnp.exp(b - m) for b, m in zip(s_ctx, ms)]
    dens = [jnp.sum(a, axis=-1, keepdims=True) + jnp.sum(b, axis=-1, keepdims=True) for a, b in zip(p_loc, p_ctx)]
    for a, b, den, (hl, sl, qs, ks) in zip(p_loc, p_ctx, dens, probs):
        o = _dot(a.astype(BF16), v_ref[0, ks, sl].astype(BF16)) + _dot(b.astype(BF16), cvs[hl])
        o_ref[0, qs, sl] = o / den


def _na_attention(q, k, v, ck, cv, layer, bias):
    nb, seq, _ = q.shape
    lat = pl.BlockSpec((1, seq, 2 * HEAD_DIM), lambda hp, b: (b, 0, hp))
    ctx = pl.BlockSpec((1, 1) + ck.shape[2:], lambda hp, b: (b, layer, 0, 0, 0))
    return pl.pallas_call(
        functools.partial(_na_attn_kernel, tq=256),
        grid=(NA_HEADS // 2, nb),
        in_specs=[lat, lat, lat, ctx, ctx, pl.BlockSpec((2, seq, seq), lambda hp, b: (hp, 0, 0))],
        out_specs=lat,
        out_shape=jax.ShapeDtypeStruct((nb, seq, NA_WIDTH), F32),
        compiler_params=_params(("parallel", "arbitrary")),
        name="na_attention",
    )(q, k, v, ck, cv, bias)


S5_PARAM_ROWS = 4 * S5_GROUP + 8


def _s5_prep_group(i, p_ref, m_ref, wz_ref, wyt_ref, ar_ref, ai_ref):
    n = S5_GROUP
    lr = jnp.minimum(p_ref[i, 4 * n:4 * n + 1], -1e-4)
    li = p_ref[i, 4 * n + 1:4 * n + 2]
    dt = jnp.exp(p_ref[i, 4 * n + 2:4 * n + 3])
    a = lr * dt
    th = li * dt
    mag = jnp.exp(a)
    lbr = mag * jnp.cos(th)
    lbi = mag * jnp.sin(th)
    den = lr * lr + li * li
    nr = lbr - 1.0
    coef_r = (nr * lr + lbi * li) / den
    coef_i = (lbi * lr - nr * li) / den
    btr, bti = p_ref[i, 0:n], p_ref[i, n:2 * n]
    bbr = coef_r * btr - coef_i * bti
    bbi = coef_r * bti + coef_i * btr
    cr, ci = p_ref[i, 2 * n:3 * n], p_ref[i, 3 * n:4 * n]

    step = lax.broadcasted_iota(jnp.int32, (CHUNK, 2 * S5_STATE), 0).astype(F32)
    fwd = lax.broadcasted_iota(jnp.int32, (CHUNK, 2 * S5_STATE), 1) < S5_STATE

    def powers(e):
        pm = jnp.exp(a * e)
        return pm * jnp.cos(th * e), pm * jnp.sin(th * e)

    pzr, pzi = powers(jnp.where(fwd, CHUNK - 1 - step, step))
    pyr, pyi = powers(jnp.where(fwd, step + 1.0, CHUNK - step))
    p16r, p16i = powers(jnp.full((1, 2 * S5_STATE), float(CHUNK), F32))
    ar_ref[i] = p16r
    ai_ref[i] = p16i

    def outer(p, c):
        return (p[:, None, :] * c[None, :, :]).reshape(CW, 2 * S5_STATE)

    wzr = outer(pzr, bbr) - outer(pzi, bbi)
    wzi = outer(pzr, bbi) + outer(pzi, bbr)
    wz_ref[i, :, 0:128] = wzr.astype(wz_ref.dtype)
    wz_ref[i, :, 128:256] = wzi.astype(wz_ref.dtype)
    wyt_ref[i, :, 0:128] = (outer(pyr, cr) - outer(pyi, ci)).astype(wyt_ref.dtype)
    wyt_ref[i, :, 128:256] = (-(outer(pyi, cr) + outer(pyr, ci))).astype(wyt_ref.dtype)

    ones = jnp.ones((CHUNK, 2 * S5_STATE), F32)
    cwr = outer(ones, cr)
    cwi = outer(ones, ci)
    fwd_rows = lax.broadcasted_iota(jnp.int32, (CW, 2 * S5_STATE), 1) < S5_STATE
    zero = jnp.zeros((CW, 2 * S5_STATE), F32)

    def split(x):
        hi = x.astype(BF16)
        return hi, (x - hi.astype(F32)).astype(BF16)

    cw_hi, cw_lo = split(jnp.concatenate([cwr, -cwi], axis=1))

    def kk(sel):
        lhs_hi, lhs_lo = split(jnp.concatenate([jnp.where(sel, wzr, zero), jnp.where(sel, wzi, zero)], axis=1))
        return _dot_nt(lhs_hi, cw_hi) + (_dot_nt(lhs_hi, cw_lo) + _dot_nt(lhs_lo, cw_hi))

    kk_f = kk(fwd_rows)
    kk_b = kk(jnp.logical_not(fwd_rows))
    lanes = 2 * S5_STATE
    steps_per_tile = lanes // S5_GROUP
    col_step = lax.broadcasted_iota(jnp.int32, (CW, lanes), 1) // S5_GROUP
    for tile in range(CW // lanes):
        cols = slice(tile * lanes, (tile + 1) * lanes)
        kf, kb = kk_f[:, cols], kk_b[:, cols]
        m = jnp.zeros((CW, lanes), F32)
        for tt in range(steps_per_tile):
            t = tile * steps_per_tile + tt
            up = (CHUNK - 1 - t) * S5_GROUP
            dn = t * S5_GROUP
            piece_f = kf if up == 0 else jnp.concatenate([kf[up:], jnp.zeros((up, lanes), F32)], axis=0)
            piece_b = kb if dn == 0 else jnp.concatenate([jnp.zeros((dn, lanes), F32), kb[:CW - dn]], axis=0)
            m = jnp.where(col_step == tt, piece_f + piece_b, m)
        m_ref[i, :, cols] = m.astype(m_ref.dtype)


def _s5_prep_kernel(*refs):
    for i in range(refs[0].shape[0]):
        _s5_prep_group(i, *refs)


def _s5_prep(lam_re, lam_im, log_step, b_re, b_im, c_re, c_im):
    g = S5_GROUPS
    lanes = 2 * S5_STATE

    def dirs_last(x):
        return x.transpose(1, 0, 2).reshape(g, 1, lanes)

    lamr, lami = dirs_last(lam_re), dirs_last(lam_im)
    logdt = dirs_last(jnp.broadcast_to(log_step[:, :, None], (2, g, S5_STATE)))
    bt = [x.transpose(1, 3, 0, 2).reshape(g, S5_GROUP, lanes) for x in (b_re, b_im)]
    ct = [x.transpose(1, 2, 0, 3).reshape(g, S5_GROUP, lanes) for x in (c_re, c_im)]
    pad = jnp.zeros((g, S5_PARAM_ROWS - 4 * S5_GROUP - 3, lanes), F32)
    packed = jnp.concatenate(bt + ct + [lamr, lami, logdt, pad], axis=1)
    gp = 4
    vec = pl.BlockSpec((gp, 1, lanes), lambda i: (i, 0, 0))
    big = pl.BlockSpec((gp, CW, CW), lambda i: (i, 0, 0))
    return pl.pallas_call(
        _s5_prep_kernel,
        grid=(g // gp,),
        in_specs=[pl.BlockSpec((gp, S5_PARAM_ROWS, lanes), lambda i: (i, 0, 0))],
        out_specs=[big, big, big, vec, vec],
        out_shape=[jax.ShapeDtypeStruct((g, CW, CW), BF16)] * 3 + [jax.ShapeDtypeStruct((g, 1, lanes), F32)] * 2,
        compiler_params=_params(("parallel",)),
        name="s5_prep",
    )(packed)


def _s5_kernel(u_ref, m_ref, wz_ref, wyt_ref, ar_ref, ai_ref, s0_ref, perm_ref, permt_ref, y_ref, fin_ref,
               xt_scr, z_scr, sp_scr, *, nbp, nc):
    s = S5_STATE
    r = xt_scr.shape[2]
    for t in range(CHUNK):
        xt_scr[t] = u_ref[pl.ds(t, r, stride=CHUNK), :].T
    fwd = lax.broadcasted_iota(jnp.int32, (nbp, 2 * s), 1) < s
    groups = range(GROUPS_PER_BLOCK)
    chans = [slice(gl * S5_GROUP, (gl + 1) * S5_GROUP) for gl in groups]

    us = [jnp.concatenate([xt_scr[t, ch, :] for t in range(CHUNK)], axis=0).T.astype(BF16) for ch in chans]
    for gl in groups:
        z_scr[gl] = _dot(_dot(perm_ref[...], us[gl]).astype(BF16), wz_ref[gl])

    ars = [jnp.broadcast_to(ar_ref[gl], (nbp, 2 * s)) for gl in groups]
    ais = [jnp.broadcast_to(ai_ref[gl], (nbp, 2 * s)) for gl in groups]
    srs = [s0_ref[gl, :, 0:2 * s] for gl in groups]
    sis = [s0_ref[gl, :, 2 * s:4 * s] for gl in groups]
    for k in range(nc):
        up = slice(k * nbp, (k + 1) * nbp)
        dn = slice((nc - 1 - k) * nbp, (nc - k) * nbp)
        for gl in groups:
            sr, si = srs[gl], sis[gl]
            sp_scr[gl, up, 0:s] = sr[:, 0:s]
            sp_scr[gl, dn, s:2 * s] = sr[:, s:2 * s]
            sp_scr[gl, up, 2 * s:3 * s] = si[:, 0:s]
            sp_scr[gl, dn, 3 * s:4 * s] = si[:, s:2 * s]
            zr = jnp.where(fwd, z_scr[gl, up, 0:2 * s], z_scr[gl, dn, 0:2 * s])
            zi = jnp.where(fwd, z_scr[gl, up, 2 * s:4 * s], z_scr[gl, dn, 2 * s:4 * s])
            srs[gl] = ars[gl] * sr - ais[gl] * si + zr
            sis[gl] = ars[gl] * si + ais[gl] * sr + zi
    for gl in groups:
        fin_ref[gl, :, 0:2 * s] = srs[gl]
        fin_ref[gl, :, 2 * s:4 * s] = sis[gl]
    sps = [_dot(permt_ref[...], sp_scr[gl].astype(BF16)).astype(BF16) for gl in groups]
    for gl in groups:
        yt = (_dot(us[gl], m_ref[gl]) + _dot_nt(sps[gl], wyt_ref[gl])).T
        for t in range(CHUNK):
            xt_scr[t, chans[gl], :] = yt[t * S5_GROUP:(t + 1) * S5_GROUP, :]

    for t in range(CHUNK):
        y_ref[pl.ds(t, r, stride=CHUNK), :] = xt_scr[t].T


def _s5_mixer(u, s0, ops, nb, seq):
    m, wz, wyt, ar, ai = ops
    g, nc = S5_GROUPS, seq // CHUNK
    nbp = -(-nb // 8) * 8
    r, rp = nb * nc, nc * nbp
    gpb = GROUPS_PER_BLOCK
    s0l = jnp.zeros((g, nbp, CW), F32)
    if s0 is not None:
        s0l = s0l.at[:, :nb].set(s0.transpose(3, 0, 2, 1, 4).reshape(g, nb, CW))
    perm = np.zeros((nc, nbp, nb, nc), np.float32)
    for b in range(nb):
        perm[np.arange(nc), b, b, np.arange(nc)] = 1.0
    perm = jnp.asarray(perm.reshape(rp, r)).astype(BF16)
    col = pl.BlockSpec((nb * seq, gpb * S5_GROUP), lambda j: (0, j))
    wspec = pl.BlockSpec((gpb, CW, CW), lambda j: (j, 0, 0))
    cspec = pl.BlockSpec((gpb, 1, 2 * S5_STATE), lambda j: (j, 0, 0))
    sspec = pl.BlockSpec((gpb, nbp, CW), lambda j: (j, 0, 0))
    y, fin = pl.pallas_call(
        functools.partial(_s5_kernel, nbp=nbp, nc=nc),
        grid=(g // gpb,),
        in_specs=[col, wspec, wspec, wspec, cspec, cspec, sspec, _full_spec((rp, r)), _full_spec((r, rp))],
        out_specs=[col, sspec],
        out_shape=[jax.ShapeDtypeStruct((nb * seq, S5_WIDTH), F32), jax.ShapeDtypeStruct((g, nbp, CW), F32)],
        scratch_shapes=[pltpu.VMEM((CHUNK, gpb * S5_GROUP, r), F32),
                        pltpu.VMEM((gpb, rp, CW), F32), pltpu.VMEM((gpb, rp, CW), F32)],
        compiler_params=_params(("parallel",)),
        name="s5_scan",
    )(u, m, wz, wyt, ar, ai, s0l, perm, perm.T)
    state = fin[:, :nb].reshape(g, nb, 2, 2, S5_STATE).transpose(1, 3, 2, 0, 4)
    return y, state


FFN_ROWS = 512
FFN_CHUNK = 256


def _ffn_rows(x, g, shift, scale, gate, wg_ref, wu_ref, wd_ref):
    h = _norm_mod(x, g, shift, scale).astype(BF16)
    acc = jnp.zeros(x.shape, F32)
    for j in range(D_FF // FFN_CHUNK):
        sl = slice(j * FFN_CHUNK, (j + 1) * FFN_CHUNK)
        gg = _dot(h, wg_ref[:, sl])
        uu = _dot(h, wu_ref[:, sl])
        act = (gg * jax.nn.sigmoid(gg) * uu).astype(BF16)
        acc = acc + _dot(act, wd_ref[sl, :])
    return x + gate * acc


def _ffn_specs(layer, row_fn, wg, wu, wd):
    return [_full_spec((1, D_MODEL)),
            _mod_spec(layer, 3, row_fn), _mod_spec(layer, 4, row_fn), _mod_spec(layer, 5, row_fn),
            _resident_spec(wg.shape), _resident_spec(wu.shape), _resident_spec(wd.shape)]


def _even_tail_kernel(a_ref, y_ref, u_ref, x_ref, d_ref, gw_ref, gb_ref, wo_ref, gate_ref,
                      g2_ref, sh2_ref, sc2_ref, gate2_ref, wg_ref, wu_ref, wd_ref, *refs):
    n_cast = len(refs) // 2
    o_ref = refs[n_cast]
    _run_casts(refs[:n_cast], refs[n_cast + 1:])
    y = jax.nn.gelu(y_ref[...] + d_ref[...] * u_ref[...])
    y = y * jax.nn.sigmoid(_dot(y.astype(BF16), gw_ref[...]) + gb_ref[...])
    o = _dot(a_ref[...].astype(BF16), wo_ref[0:NA_WIDTH, :]) + _dot(y.astype(BF16), wo_ref[NA_WIDTH:, :])
    x = x_ref[...] + gate_ref[0] * o
    o_ref[...] = _ffn_rows(x, g2_ref[...], sh2_ref[0], sc2_ref[0], gate2_ref[0], wg_ref, wu_ref, wd_ref)


def _even_tail(a, y, u, x, d, glu_w, glu_b, w_out, g_ffn, wg, wu, wd, mod, layer, row_fn, casts=()):
    t = x.shape[0]
    tm = FFN_ROWS
    c_in, c_out, c_shape, c_ops = _cast_plan(casts, t // tm)
    half = pl.BlockSpec((tm, NA_WIDTH), lambda i: (i, 0))
    full = pl.BlockSpec((tm, D_MODEL), lambda i: (i, 0))
    return pl.pallas_call(
        _even_tail_kernel,
        grid=(t // tm,),
        in_specs=[half, half, half, full, _full_spec((1, S5_WIDTH)), _resident_spec(glu_w.shape),
                  _full_spec((1, S5_WIDTH)), _resident_spec(w_out.shape), _mod_spec(layer, 2, row_fn)]
                 + _ffn_specs(layer, row_fn, wg, wu, wd) + c_in,
        out_specs=[full] + c_out,
        out_shape=[jax.ShapeDtypeStruct((t, D_MODEL), F32)] + c_shape,
        compiler_params=_params(("parallel",)),
        name="even_tail",
    )(a, y, u, x, d, glu_w, glu_b, w_out, mod, g_ffn, mod, mod, mod, wg, wu, wd, *c_ops)


def _dft_consts(seq):
    def cs(n):
        k = np.arange(n)
        ang = 2.0 * np.pi * ((k[:, None] * k[None, :]) % n) / n
        return np.cos(ang), np.sin(ang)

    cc, sc = cs(FNET_GROUP_WIDTH)
    cl, sl = cs(seq)
    chan = jnp.asarray(np.concatenate([cc, sc], axis=1), dtype=F32)
    pos = jnp.asarray(np.concatenate([cl, -sl], axis=1), dtype=F32)
    return chan.astype(BF16), pos.astype(BF16)


def _fourier_rows(x, g, shift, scale, gate, win_ref, chan_ref, pos_ref, wout_ref):
    seq = x.shape[0]
    h = _norm_mod(x, g, shift, scale).astype(BF16)
    z = _dot(h, win_ref[...]).astype(BF16)
    w = FNET_GROUP_WIDTH
    pos = pos_ref[...]
    parts = []
    for c in range(FNET_GROUPS):
        ab = _dot(z[:, c * w:(c + 1) * w], chan_ref[...]).astype(BF16)
        stacked = jnp.concatenate([ab[:, :w], ab[:, w:]], axis=0)
        parts.append(_dot(pos, stacked))
    f = jnp.concatenate(parts, axis=1) * (1.0 / math.sqrt(seq * w))
    return x + gate * _dot(f.astype(BF16), wout_ref[...])


def _odd_layer_kernel(x_ref, g_ref, sh_ref, sc_ref, win_ref, chan_ref, pos_ref, wout_ref, gate_ref,
                      g2_ref, sh2_ref, sc2_ref, gate2_ref, wg_ref, wu_ref, wd_ref, fg_ref, o_ref, *, seq):
    tm = x_ref.shape[0]
    for r0 in range(0, tm, seq):
        rows = slice(r0, r0 + seq)
        o_ref[rows, :] = _fourier_rows(x_ref[rows, :], g_ref[...], sh_ref[0], sc_ref[0], gate_ref[0],
                                       win_ref, chan_ref, pos_ref, wout_ref)
    for r0 in range(0, tm, FFN_ROWS):
        rows = slice(r0, r0 + FFN_ROWS)
        y = _ffn_rows(o_ref[rows, :], g2_ref[...], sh2_ref[0], sc2_ref[0], gate2_ref[0], wg_ref, wu_ref, wd_ref)
        ms = jnp.mean(y * y, axis=-1, keepdims=True)
        o_ref[rows, :] = y * lax.rsqrt(ms + EPS) * fg_ref[...]


def _odd_layer(x, g_mix, g_ffn, final_g, mod, layer, row_of, w_in, w_out, wg, wu, wd, seq):
    t = x.shape[0]
    tm = max(seq, FFN_ROWS)
    row_fn = row_of(tm)
    chan, pos = _dft_consts(seq)
    blk = pl.BlockSpec((tm, D_MODEL), lambda i: (i, 0))
    return pl.pallas_call(
        functools.partial(_odd_layer_kernel, seq=seq),
        grid=(t // tm,),
        in_specs=[blk, _full_spec((1, D_MODEL)), _mod_spec(layer, 0, row_fn), _mod_spec(layer, 1, row_fn),
                  _resident_spec(w_in.shape), _resident_spec(chan.shape), _resident_spec(pos.shape),
                  _resident_spec(w_out.shape), _mod_spec(layer, 2, row_fn)]
                 + _ffn_specs(layer, row_fn, wg, wu, wd) + [_full_spec((1, D_MODEL))],
        out_specs=blk,
        out_shape=jax.ShapeDtypeStruct((t, D_MODEL), F32),
        compiler_params=_params(("parallel",)),
        name="odd_layer",
    )(x, g_mix, mod, mod, w_in, chan, pos, w_out, mod, g_ffn, mod, mod, mod, wg, wu, wd, final_g)


def kernel(x_prompt, x_sample, cache_na_k, cache_na_v, state_s5, c, c_ctx, w_mod, b_mod, norm_mix_g, norm_ffn_g, w_in_even, na_rpb, s5_lam_re, s5_lam_im, s5_log_step, s5_b_re, s5_b_im, s5_c_re, s5_c_im, s5_d, s5_glu_w, s5_glu_b, w_out_even, w_in_odd, w_out_odd, ffn_w_gate, ffn_w_up, ffn_w_down, final_norm_g):
    bp, lp, _ = x_prompt.shape
    bs, ls, _ = x_sample.shape
    depth = w_mod.shape[0]
    assert bs + 1 <= N_COND and depth == 2

    cond = jnp.concatenate([c_ctx[None], c, jnp.zeros((COND_ROWS - 1 - bs, D_MODEL), F32)], axis=0)
    mod = _modulation(cond.T, w_mod, b_mod).reshape(depth * COND_ROWS * 6, 1, D_MODEL)

    streams = [
        dict(x=x_prompt.reshape(bp * lp, D_MODEL), nb=bp, seq=lp, row_of=lambda tm: (lambda i: 0)),
        dict(x=x_sample.reshape(bs * ls, D_MODEL), nb=bs, seq=ls, row_of=lambda tm: (lambda i: 1 + (i * tm) // ls)),
    ]
    tm = FFN_ROWS

    e = 0
    w_in = w_in_even[e].astype(BF16)
    glu_b = s5_glu_b[e].reshape(1, S5_WIDTH)
    g_mix = norm_mix_g[0].reshape(1, D_MODEL)
    g_ffn0 = norm_ffn_g[0].reshape(1, D_MODEL)
    s5_ops = _s5_prep(s5_lam_re[e], s5_lam_im[e], s5_log_step[e], s5_b_re[e], s5_b_im[e], s5_c_re[e], s5_c_im[e])
    s5_dvec = s5_d[e].reshape(1, S5_WIDTH)
    bias = _na_bias(na_rpb[e], ls)
    layer0_w = [(ffn_w_gate, 0), (ffn_w_up, 0), (ffn_w_down, 0), (w_out_even, e), (s5_glu_w, e)]
    layer1_w = [(ffn_w_gate, 1), (ffn_w_up, 1), (ffn_w_down, 1), (w_in_odd, 0), (w_out_odd, 0)]

    new_k = new_v = new_s = None
    for si, st in enumerate(streams):
        nb, seq = st["nb"], st["seq"]
        if si == 0:
            a, u, new_k, new_v, *layer0_w = _inproj(st["x"], g_mix, mod, 0, st["row_of"](tm), w_in, tm,
                                                    ctx_seq=seq, casts=layer0_w)
            wg0, wu0, wd0, w_out, glu_w = layer0_w
            s0 = None
        else:
            q, k, v, u = _inproj(st["x"], g_mix, mod, 0, st["row_of"](tm), w_in, tm)
            a = _na_attention(q.reshape(nb, seq, NA_WIDTH), k.reshape(nb, seq, NA_WIDTH), v.reshape(nb, seq, NA_WIDTH),
                              cache_na_k, cache_na_v, e, bias).reshape(nb * seq, NA_WIDTH)
            s0 = state_s5[:, e].astype(F32)
        y, state = _s5_mixer(u, s0, s5_ops, nb, seq)
        if si == 0:
            new_s = state[:, None]
        tail = _even_tail(a, y, u, st["x"], s5_dvec, glu_w, glu_b, w_out, g_ffn0, wg0, wu0, wd0,
                          mod, 0, st["row_of"](tm), casts=layer1_w if si == 0 else ())
        st["x"] = tail[0]
        if si == 0:
            wg1, wu1, wd1, w_in1, w_out1 = tail[1:]

    for st in streams:
        st["x"] = _odd_layer(st["x"], norm_mix_g[1].reshape(1, D_MODEL), norm_ffn_g[1].reshape(1, D_MODEL),
                             final_norm_g.reshape(1, D_MODEL), mod, 1, st["row_of"],
                             w_in1, w_out1, wg1, wu1, wd1, st["seq"])

    y_prompt = streams[0]["x"].reshape(bp, lp, D_MODEL)
    y_sample = streams[1]["x"].reshape(bs, ls, D_MODEL)
    return (y_prompt, y_sample, new_k, new_v, new_s)
```

```python
import functools
import math

import numpy as np
import jax
import jax.numpy as jnp
from jax import lax
from jax.experimental import pallas as pl
from jax.experimental.pallas import tpu as pltpu

F32 = jnp.float32
BF16 = jnp.bfloat16

D_MODEL = 1024
NA_HEADS = 8
HEAD_DIM = 64
NA_WIDTH = NA_HEADS * HEAD_DIM
GRID_W = 64
WIN_R = 8
WIN_C = 16
S5_GROUP = 16
S5_GROUPS = 32
S5_STATE = 64
S5_WIDTH = S5_GROUPS * S5_GROUP
FNET_GROUPS = 4
FNET_GROUP_WIDTH = D_MODEL // FNET_GROUPS
D_FF = 2816
EPS = 1e-6

CHUNK = 16
CW = CHUNK * S5_GROUP
GROUPS_PER_BLOCK = 128 // S5_GROUP
N_COND = 3
COND_ROWS = 8
NEG = -1e30
VMEM_LIMIT = 56 * 2 ** 20
EVEN_TAIL_VMEM = 60 * 2 ** 20

_NT = (((1,), (1,)), ((), ()))


def _params(sem, vmem=VMEM_LIMIT):
    return pltpu.CompilerParams(dimension_semantics=sem, vmem_limit_bytes=vmem)


def _dot(a, b):
    return jnp.dot(a, b, preferred_element_type=F32)


def _dot_nt(a, b):
    return lax.dot_general(a, b, _NT, preferred_element_type=F32)


def _norm_mod(x, g, shift, scale):
    ms = jnp.mean(x * x, axis=-1, keepdims=True)
    return (x * lax.rsqrt(ms + EPS) * g) * (1.0 + scale) + shift


def _mod_spec(layer, which, row_fn):
    return pl.BlockSpec((1, 1, D_MODEL),
                        lambda *idx: (layer * COND_ROWS * 6 + row_fn(*idx) * 6 + which, 0, 0))


def _full_spec(shape):
    nd = len(shape)
    return pl.BlockSpec(shape, lambda *idx: (0,) * nd)


def _resident_spec(shape):
    nd = len(shape)
    return pl.BlockSpec(shape, lambda *idx: (0,) * nd, pipeline_mode=pl.Buffered(1))


def _cast_plan(weights, steps, slab_of=lambda i: i):
    in_specs, out_specs, out_shape, operands = [], [], [], []
    for w in weights:
        w, layer = w if isinstance(w, tuple) else (w, None)
        rows, cols = w.shape[-2:]
        slab = rows // steps
        if layer is None:
            in_specs.append(pl.BlockSpec((slab, cols), lambda i: (slab_of(i), 0)))
        else:
            in_specs.append(pl.BlockSpec((None, slab, cols), lambda i, layer=layer: (layer, slab_of(i), 0)))
        out_specs.append(pl.BlockSpec((slab, cols), lambda i: (slab_of(i), 0)))
        out_shape.append(jax.ShapeDtypeStruct((rows, cols), BF16))
        operands.append(w)
    return in_specs, out_specs, out_shape, operands


def _run_casts(src_refs, dst_refs):
    for s, d in zip(src_refs, dst_refs):
        d[...] = s[...].astype(d.dtype)


MOD_COLS = 2048


def _mod_kernel(ct_ref, w_ref, b_ref, o_ref):
    ct = ct_ref[...]
    st = ct * jax.nn.sigmoid(ct)
    w = w_ref[0]
    o_ref[0] = jnp.zeros(o_ref.shape[1:], F32)
    for r in range(N_COND):
        o_ref[0, r:r + 1, :] = jnp.sum(w * st[:, r:r + 1], axis=0, keepdims=True) + b_ref[0]


def _modulation(cond_t, w_mod, b_mod):
    depth, _, n = w_mod.shape
    tn = MOD_COLS
    return pl.pallas_call(
        _mod_kernel,
        grid=(depth, n // tn),
        in_specs=[_full_spec((D_MODEL, COND_ROWS)),
                  pl.BlockSpec((1, D_MODEL, tn), lambda l, j: (l, 0, j)),
                  pl.BlockSpec((1, 1, tn), lambda l, j: (l, 0, j))],
        out_specs=pl.BlockSpec((1, COND_ROWS, tn), lambda l, j: (l, 0, j)),
        out_shape=jax.ShapeDtypeStruct((depth, COND_ROWS, n), F32),
        compiler_params=_params(("parallel", "parallel")),
        name="modulation",
    )(cond_t, w_mod, b_mod.reshape(depth, 1, n))


def _ctx_attention_rows(q, k, v, seq):
    scale = HEAD_DIM ** -0.5
    nseq = q.shape[0] // seq
    qt = (q * scale).T.astype(BF16)
    vt = v.T.astype(BF16)
    kb = k.astype(BF16)
    probs = [(slice(i * seq, (i + 1) * seq), slice(h * HEAD_DIM, (h + 1) * HEAD_DIM))
             for i in range(nseq) for h in range(NA_HEADS)]
    sts = [_dot(kb[rows, sl], qt[sl, rows]) for rows, sl in probs]
    pts = [jnp.exp(st - jnp.max(st, axis=0, keepdims=True)) for st in sts]
    dens = [jnp.sum(pt, axis=0, keepdims=True) for pt in pts]
    outs = [_dot(vt[sl, rows], pt.astype(BF16)) / den for (rows, sl), pt, den in zip(probs, pts, dens)]
    per_seq = [jnp.concatenate(outs[i * NA_HEADS:(i + 1) * NA_HEADS], axis=0).T for i in range(nseq)]
    return jnp.concatenate(per_seq, axis=0)


def _inproj_kernel(x_ref, g_ref, sh_ref, sc_ref, w_ref, *refs, ctx_seq, n_cast):
    _run_casts(refs[:n_cast], refs[n_cast + 4:2 * n_cast + 4])
    refs = refs[n_cast:n_cast + 4] + refs[2 * n_cast + 4:]
    h = _norm_mod(x_ref[...], g_ref[...], sh_ref[0], sc_ref[0]).astype(BF16)
    z = _dot(h, w_ref[...])
    q, k, v, u = (z[:, i * NA_WIDTH:(i + 1) * NA_WIDTH] for i in range(4))
    if ctx_seq is None:
        for o, val in zip(refs, (q, k, v, u)):
            o[...] = val.astype(o.dtype)
        return
    a_ref, u_ref, kc_ref, vc_ref, rows_scr = refs
    a_ref[...] = _ctx_attention_rows(q, k, v, ctx_seq).astype(a_ref.dtype)
    u_ref[...] = u
    tm = z.shape[0]
    pair = 2 * HEAD_DIM
    for o, val in ((kc_ref, k), (vc_ref, v)):
        for j in range(NA_HEADS // 2):
            both = val[:, j * pair:(j + 1) * pair]
            rows_scr[pl.ds(2 * j, tm, stride=NA_HEADS), :] = both
            rows_scr[pl.ds(2 * j + 1, tm, stride=NA_HEADS), :] = pltpu.roll(both, HEAD_DIM, axis=1)
        for b in range(o.shape[0]):
            blk = rows_scr[b * ctx_seq * NA_HEADS:(b + 1) * ctx_seq * NA_HEADS, 0:HEAD_DIM]
            o[b, 0] = blk.reshape(ctx_seq, NA_HEADS, HEAD_DIM)


def _inproj(x, g, mod, layer, row_fn, w, tm, ctx_seq=None, casts=()):
    t = x.shape[0]
    c_in, c_out, c_shape, c_ops = _cast_plan(casts, t // tm)
    flat = pl.BlockSpec((tm, NA_WIDTH), lambda i: (i, 0))
    scratch = []
    if ctx_seq is None:
        out_specs = [flat] * 4
        out_shape = [jax.ShapeDtypeStruct((t, NA_WIDTH), BF16)] + [jax.ShapeDtypeStruct((t, NA_WIDTH), F32)] * 3
    else:
        per = tm // ctx_seq
        cache = pl.BlockSpec((per, 1, ctx_seq, NA_HEADS, HEAD_DIM), lambda i: (i, 0, 0, 0, 0))
        out_specs = [flat, flat, cache, cache]
        out_shape = ([jax.ShapeDtypeStruct((t, NA_WIDTH), BF16), jax.ShapeDtypeStruct((t, NA_WIDTH), F32)]
                     + [jax.ShapeDtypeStruct((t // ctx_seq, 1, ctx_seq, NA_HEADS, HEAD_DIM), F32)] * 2)
        scratch = [pltpu.VMEM((tm * NA_HEADS, 2 * HEAD_DIM), F32)]
    return pl.pallas_call(
        functools.partial(_inproj_kernel, ctx_seq=ctx_seq, n_cast=len(c_ops)),
        grid=(t // tm,),
        in_specs=[pl.BlockSpec((tm, D_MODEL), lambda i: (i, 0)),
                  _full_spec((1, D_MODEL)),
                  _mod_spec(layer, 0, row_fn), _mod_spec(layer, 1, row_fn),
                  _resident_spec(w.shape)] + c_in,
        out_specs=out_specs + c_out,
        out_shape=out_shape + c_shape,
        scratch_shapes=scratch,
        compiler_params=_params(("parallel",)),
        name="inproj_even",
    )(x, g, mod, mod, w, *c_ops)


def _na_bias_kernel(rpb_ref, o_ref, *, rows):
    h = pl.program_id(0)
    kr_win = min(WIN_R, rows)
    qc = lax.broadcasted_iota(jnp.int32, (GRID_W, 2 * GRID_W), 0)
    lane = lax.broadcasted_iota(jnp.int32, (GRID_W, 2 * GRID_W), 1)
    kc = lane & (GRID_W - 1)
    cs = jnp.clip(qc - WIN_C // 2, 0, GRID_W - WIN_C)
    col_ok = (kc >= cs) & (kc < cs + WIN_C)
    dc = kc - qc + (WIN_C - 1)
    left = lane < GRID_W
    neg = jnp.full((GRID_W, 2 * GRID_W), NEG, F32)
    toeplitz = []
    for dr in range(2 * WIN_R - 1):
        t = neg
        for d in range(2 * WIN_C - 1):
            t = jnp.where(dc == d, rpb_ref[h, dr, d], t)
        toeplitz.append(jnp.where(col_ok, t, neg))
    for r in range(rows):
        rs = min(max(r - kr_win // 2, 0), rows - kr_win)

        def blk(kr):
            return toeplitz[kr - r + WIN_R - 1] if rs <= kr < rs + kr_win else None

        for j in range(rows // 2):
            a, b = blk(2 * j), blk(2 * j + 1)
            if a is None and b is None:
                pair = neg
            else:
                pair = jnp.where(left, neg if a is None else a, neg if b is None else b)
            o_ref[0, r * GRID_W:(r + 1) * GRID_W, j * 2 * GRID_W:(j + 1) * 2 * GRID_W] = pair


def _na_bias(rpb, seq):
    rows = seq // GRID_W
    return pl.pallas_call(
        functools.partial(_na_bias_kernel, rows=rows),
        grid=(NA_HEADS,),
        in_specs=[pl.BlockSpec(memory_space=pltpu.SMEM)],
        out_specs=pl.BlockSpec((1, seq, seq), lambda h: (h, 0, 0)),
        out_shape=jax.ShapeDtypeStruct((NA_HEADS, seq, seq), F32),
        compiler_params=_params(("parallel",)),
        name="na_bias",
    )(rpb)


def _na_key_range(qb, tq, rows):
    kr_win = min(WIN_R, rows)
    r0, r1 = qb * tq // GRID_W, ((qb + 1) * tq - 1) // GRID_W
    rs0 = min(max(r0 - kr_win // 2, 0), rows - kr_win)
    rs1 = min(max(r1 - kr_win // 2, 0), rows - kr_win)
    lo, hi = rs0 * GRID_W, (rs1 + kr_win) * GRID_W
    return lo // 128 * 128, -(-hi // 128) * 128


def _na_attn_kernel(q_ref, k_ref, v_ref, ck_ref, cv_ref, bias_ref, o_ref, *, tq):
    scale = HEAD_DIM ** -0.5
    seq = q_ref.shape[1]
    probs = []
    for hl in range(2):
        sl = slice(hl * HEAD_DIM, (hl + 1) * HEAD_DIM)
        for qb in range(seq // tq):
            lo, hi = _na_key_range(qb, tq, seq // GRID_W)
            probs.append((hl, sl, slice(qb * tq, (qb + 1) * tq), slice(lo, hi)))
    head0 = pl.program_id(0) * 2
    cks = [ck_ref[0, 0, :, head0 + hl, :].astype(BF16) for hl in range(2)]
    cvs = [cv_ref[0, 0, :, head0 + hl, :].astype(BF16) for hl in range(2)]
    qs_all = [(q_ref[0, qs, sl] * scale).astype(BF16) for _, sl, qs, _ in probs]
    s_loc = [_dot_nt(q, k_ref[0, ks, sl].astype(BF16)) + bias_ref[hl, qs, ks]
             for q, (hl, sl, qs, ks) in zip(qs_all, probs)]
    s_ctx = [_dot_nt(q, cks[hl]) for q, (hl, _, _, _) in zip(qs_all, probs)]
    ms = [jnp.maximum(jnp.max(a, axis=-1, keepdims=True), jnp.max(b, axis=-1, keepdims=True))
          for a, b in zip(s_loc, s_ctx)]
    p_loc = [jnp.exp(a - m) for a, m in zip(s_loc, ms)]
    p_ctx = [jnp.exp(b - m) for b, m in zip(s_ctx, ms)]
    dens = [jnp.sum(a, axis=-1, keepdims=True) + jnp.sum(b, axis=-1, keepdims=True) for a, b in zip(p_loc, p_ctx)]
    for a, b, den, (hl, sl, qs, ks) in zip(p_loc, p_ctx, dens, probs):
        o = _dot(a.astype(BF16), v_ref[0, ks, sl].astype(BF16)) + _dot(b.astype(BF16), cvs[hl])
        o_ref[0, qs, sl] = o / den


def _na_attention(q, k, v, ck, cv, layer, bias):
    nb, seq, _ = q.shape
    lat = pl.BlockSpec((1, seq, 2 * HEAD_DIM), lambda hp, b: (b, 0, hp))
    ctx = pl.BlockSpec((1, 1) + ck.shape[2:], lambda hp, b: (b, layer, 0, 0, 0))
    return pl.pallas_call(
        functools.partial(_na_attn_kernel, tq=256),
        grid=(NA_HEADS // 2, nb),
        in_specs=[lat, lat, lat, ctx, ctx, pl.BlockSpec((2, seq, seq), lambda hp, b: (hp, 0, 0))],
        out_specs=lat,
        out_shape=jax.ShapeDtypeStruct((nb, seq, NA_WIDTH), F32),
        compiler_params=_params(("parallel", "arbitrary")),
        name="na_attention",
    )(q, k, v, ck, cv, bias)


S5_PARAM_ROWS = 4 * S5_GROUP + 8


def _s5_prep_group(i, p_ref, m_ref, wz_ref, wyt_ref, ar_ref, ai_ref):
    n = S5_GROUP
    lr = jnp.minimum(p_ref[i, 4 * n:4 * n + 1], -1e-4)
    li = p_ref[i, 4 * n + 1:4 * n + 2]
    dt = jnp.exp(p_ref[i, 4 * n + 2:4 * n + 3])
    a = lr * dt
    th = li * dt
    mag = jnp.exp(a)
    lbr = mag * jnp.cos(th)
    lbi = mag * jnp.sin(th)
    den = lr * lr + li * li
    nr = lbr - 1.0
    coef_r = (nr * lr + lbi * li) / den
    coef_i = (lbi * lr - nr * li) / den
    btr, bti = p_ref[i, 0:n], p_ref[i, n:2 * n]
    bbr = coef_r * btr - coef_i * bti
    bbi = coef_r * bti + coef_i * btr
    cr, ci = p_ref[i, 2 * n:3 * n], p_ref[i, 3 * n:4 * n]

    step = lax.broadcasted_iota(jnp.int32, (CHUNK, 2 * S5_STATE), 0).astype(F32)
    fwd = lax.broadcasted_iota(jnp.int32, (CHUNK, 2 * S5_STATE), 1) < S5_STATE

    def powers(e):
        pm = jnp.exp(a * e)
        return pm * jnp.cos(th * e), pm * jnp.sin(th * e)

    pzr, pzi = powers(jnp.where(fwd, CHUNK - 1 - step, step))
    pyr, pyi = powers(jnp.where(fwd, step + 1.0, CHUNK - step))
    p16r, p16i = powers(jnp.full((1, 2 * S5_STATE), float(CHUNK), F32))
    ar_ref[i] = p16r
    ai_ref[i] = p16i

    def outer(p, c):
        return (p[:, None, :] * c[None, :, :]).reshape(CW, 2 * S5_STATE)

    wzr = outer(pzr, bbr) - outer(pzi, bbi)
    wzi = outer(pzr, bbi) + outer(pzi, bbr)
    wz_ref[i, :, 0:128] = wzr.astype(wz_ref.dtype)
    wz_ref[i, :, 128:256] = wzi.astype(wz_ref.dtype)
    wyt_ref[i, :, 0:128] = (outer(pyr, cr) - outer(pyi, ci)).astype(wyt_ref.dtype)
    wyt_ref[i, :, 128:256] = (-(outer(pyi, cr) + outer(pyr, ci))).astype(wyt_ref.dtype)

    ones = jnp.ones((CHUNK, 2 * S5_STATE), F32)
    cwr = outer(ones, cr)
    cwi = outer(ones, ci)
    fwd_rows = lax.broadcasted_iota(jnp.int32, (CW, 2 * S5_STATE), 1) < S5_STATE
    zero = jnp.zeros((CW, 2 * S5_STATE), F32)

    def split(x):
        hi = x.astype(BF16)
        return hi, (x - hi.astype(F32)).astype(BF16)

    cw_hi, cw_lo = split(jnp.concatenate([cwr, -cwi], axis=1))

    def kk(sel):
        lhs_hi, lhs_lo = split(jnp.concatenate([jnp.where(sel, wzr, zero), jnp.where(sel, wzi, zero)], axis=1))
        return _dot_nt(lhs_hi, cw_hi) + (_dot_nt(lhs_hi, cw_lo) + _dot_nt(lhs_lo, cw_hi))

    kk_f = kk(fwd_rows)
    kk_b = kk(jnp.logical_not(fwd_rows))
    lanes = 2 * S5_STATE
    steps_per_tile = lanes // S5_GROUP
    col_step = lax.broadcasted_iota(jnp.int32, (CW, lanes), 1) // S5_GROUP
    for tile in range(CW // lanes):
        cols = slice(tile * lanes, (tile + 1) * lanes)
        kf, kb = kk_f[:, cols], kk_b[:, cols]
        m = jnp.zeros((CW, lanes), F32)
        for tt in range(steps_per_tile):
            t = tile * steps_per_tile + tt
            up = (CHUNK - 1 - t) * S5_GROUP
            dn = t * S5_GROUP
            piece_f = kf if up == 0 else jnp.concatenate([kf[up:], jnp.zeros((up, lanes), F32)], axis=0)
            piece_b = kb if dn == 0 else jnp.concatenate([jnp.zeros((dn, lanes), F32), kb[:CW - dn]], axis=0)
            m = jnp.where(col_step == tt, piece_f + piece_b, m)
        m_ref[i, :, cols] = m.astype(m_ref.dtype)


def _s5_prep_kernel(*refs):
    for i in range(refs[0].shape[0]):
        _s5_prep_group(i, *refs)


def _s5_prep(lam_re, lam_im, log_step, b_re, b_im, c_re, c_im):
    g = S5_GROUPS
    lanes = 2 * S5_STATE

    def dirs_last(x):
        return x.transpose(1, 0, 2).reshape(g, 1, lanes)

    lamr, lami = dirs_last(lam_re), dirs_last(lam_im)
    logdt = dirs_last(jnp.broadcast_to(log_step[:, :, None], (2, g, S5_STATE)))
    bt = [x.transpose(1, 3, 0, 2).reshape(g, S5_GROUP, lanes) for x in (b_re, b_im)]
    ct = [x.transpose(1, 2, 0, 3).reshape(g, S5_GROUP, lanes) for x in (c_re, c_im)]
    pad = jnp.zeros((g, S5_PARAM_ROWS - 4 * S5_GROUP - 3, lanes), F32)
    packed = jnp.concatenate(bt + ct + [lamr, lami, logdt, pad], axis=1)
    gp = 4
    vec = pl.BlockSpec((gp, 1, lanes), lambda i: (i, 0, 0))
    big = pl.BlockSpec((gp, CW, CW), lambda i: (i, 0, 0))
    return pl.pallas_call(
        _s5_prep_kernel,
        grid=(g // gp,),
        in_specs=[pl.BlockSpec((gp, S5_PARAM_ROWS, lanes), lambda i: (i, 0, 0))],
        out_specs=[big, big, big, vec, vec],
        out_shape=[jax.ShapeDtypeStruct((g, CW, CW), BF16)] * 3 + [jax.ShapeDtypeStruct((g, 1, lanes), F32)] * 2,
        compiler_params=_params(("parallel",)),
        name="s5_prep",
    )(packed)


def _s5_kernel(u_ref, m_ref, wz_ref, wyt_ref, ar_ref, ai_ref, s0_ref, perm_ref, permt_ref, y_ref, fin_ref,
               xt_scr, z_scr, sp_scr, *, nbp, nc):
    s = S5_STATE
    r = xt_scr.shape[2]
    for t in range(CHUNK):
        xt_scr[t] = u_ref[pl.ds(t, r, stride=CHUNK), :].T
    fwd = lax.broadcasted_iota(jnp.int32, (nbp, 2 * s), 1) < s
    groups = range(GROUPS_PER_BLOCK)
    chans = [slice(gl * S5_GROUP, (gl + 1) * S5_GROUP) for gl in groups]

    us = [jnp.concatenate([xt_scr[t, ch, :] for t in range(CHUNK)], axis=0).T.astype(BF16) for ch in chans]
    for gl in groups:
        z_scr[gl] = _dot(_dot(perm_ref[...], us[gl]).astype(BF16), wz_ref[gl])

    ars = [jnp.broadcast_to(ar_ref[gl], (nbp, 2 * s)) for gl in groups]
    ais = [jnp.broadcast_to(ai_ref[gl], (nbp, 2 * s)) for gl in groups]
    srs = [s0_ref[gl, :, 0:2 * s] for gl in groups]
    sis = [s0_ref[gl, :, 2 * s:4 * s] for gl in groups]
    for k in range(nc):
        up = slice(k * nbp, (k + 1) * nbp)
        dn = slice((nc - 1 - k) * nbp, (nc - k) * nbp)
        for gl in groups:
            sr, si = srs[gl], sis[gl]
            sp_scr[gl, up, 0:s] = sr[:, 0:s]
            sp_scr[gl, dn, s:2 * s] = sr[:, s:2 * s]
            sp_scr[gl, up, 2 * s:3 * s] = si[:, 0:s]
            sp_scr[gl, dn, 3 * s:4 * s] = si[:, s:2 * s]
            zr = jnp.where(fwd, z_scr[gl, up, 0:2 * s], z_scr[gl, dn, 0:2 * s])
            zi = jnp.where(fwd, z_scr[gl, up, 2 * s:4 * s], z_scr[gl, dn, 2 * s:4 * s])
            srs[gl] = ars[gl] * sr - ais[gl] * si + zr
            sis[gl] = ars[gl] * si + ais[gl] * sr + zi
    for gl in groups:
        fin_ref[gl, :, 0:2 * s] = srs[gl]
        fin_ref[gl, :, 2 * s:4 * s] = sis[gl]
    sps = [_dot(permt_ref[...], sp_scr[gl].astype(BF16)).astype(BF16) for gl in groups]
    for gl in groups:
        yt = (_dot(us[gl], m_ref[gl]) + _dot_nt(sps[gl], wyt_ref[gl])).T
        for t in range(CHUNK):
            xt_scr[t, chans[gl], :] = yt[t * S5_GROUP:(t + 1) * S5_GROUP, :]

    for t in range(CHUNK):
        y_ref[pl.ds(t, r, stride=CHUNK), :] = xt_scr[t].T


def _s5_mixer(u, s0, ops, nb, seq):
    m, wz, wyt, ar, ai = ops
    g, nc = S5_GROUPS, seq // CHUNK
    nbp = -(-nb // 8) * 8
    r, rp = nb * nc, nc * nbp
    gpb = GROUPS_PER_BLOCK
    s0l = jnp.zeros((g, nbp, CW), F32)
    if s0 is not None:
        s0l = s0l.at[:, :nb].set(s0.transpose(3, 0, 2, 1, 4).reshape(g, nb, CW))
    perm = np.zeros((nc, nbp, nb, nc), np.float32)
    for b in range(nb):
        perm[np.arange(nc), b, b, np.arange(nc)] = 1.0
    perm = jnp.asarray(perm.reshape(rp, r)).astype(BF16)
    col = pl.BlockSpec((nb * seq, gpb * S5_GROUP), lambda j: (0, j))
    wspec = pl.BlockSpec((gpb, CW, CW), lambda j: (j, 0, 0))
    cspec = pl.BlockSpec((gpb, 1, 2 * S5_STATE), lambda j: (j, 0, 0))
    sspec = pl.BlockSpec((gpb, nbp, CW), lambda j: (j, 0, 0))
    y, fin = pl.pallas_call(
        functools.partial(_s5_kernel, nbp=nbp, nc=nc),
        grid=(g // gpb,),
        in_specs=[col, wspec, wspec, wspec, cspec, cspec, sspec, _full_spec((rp, r)), _full_spec((r, rp))],
        out_specs=[col, sspec],
        out_shape=[jax.ShapeDtypeStruct((nb * seq, S5_WIDTH), F32), jax.ShapeDtypeStruct((g, nbp, CW), F32)],
        scratch_shapes=[pltpu.VMEM((CHUNK, gpb * S5_GROUP, r), F32),
                        pltpu.VMEM((gpb, rp, CW), F32), pltpu.VMEM((gpb, rp, CW), F32)],
        compiler_params=_params(("parallel",)),
        name="s5_scan",
    )(u, m, wz, wyt, ar, ai, s0l, perm, perm.T)
    state = fin[:, :nb].reshape(g, nb, 2, 2, S5_STATE).transpose(1, 3, 2, 0, 4)
    return y, state


FFN_ROWS = 512
FFN_CHUNK = 256


def _ffn_rows(x, g, shift, scale, gate, wg_ref, wu_ref, wd_ref):
    h = _norm_mod(x, g, shift, scale).astype(BF16)
    acc = jnp.zeros(x.shape, F32)
    for j in range(D_FF // FFN_CHUNK):
        sl = slice(j * FFN_CHUNK, (j + 1) * FFN_CHUNK)
        gg = _dot(h, wg_ref[:, sl])
        uu = _dot(h, wu_ref[:, sl])
        act = (gg * jax.nn.sigmoid(gg) * uu).astype(BF16)
        acc = acc + _dot(act, wd_ref[sl, :])
    return x + gate * acc


def _ffn_specs(layer, row_fn, wg, wu, wd):
    return [_full_spec((1, D_MODEL)),
            _mod_spec(layer, 3, row_fn), _mod_spec(layer, 4, row_fn), _mod_spec(layer, 5, row_fn),
            _resident_spec(wg.shape), _resident_spec(wu.shape), _resident_spec(wd.shape)]


def _even_tail_kernel(*refs, steps_ctx):
    ctx_in, lat_in = refs[0:4], refs[4:8]
    d_ref, gw_ref, gb_ref, wo_ref, gate_ref, g2_ref, sh2_ref, sc2_ref, gate2_ref, wg_ref, wu_ref, wd_ref = refs[8:20]
    n_cast = (len(refs) - 22) // 2
    o_ctx, o_lat = refs[20 + n_cast:22 + n_cast]
    _run_casts(refs[20:20 + n_cast], refs[22 + n_cast:])

    def tail(a_ref, y_ref, u_ref, x_ref, o_ref):
        y = jax.nn.gelu(y_ref[...] + d_ref[...] * u_ref[...])
        y = y * jax.nn.sigmoid(_dot(y.astype(BF16), gw_ref[...]) + gb_ref[...])
        o = _dot(a_ref[...].astype(BF16), wo_ref[0:NA_WIDTH, :]) + _dot(y.astype(BF16), wo_ref[NA_WIDTH:, :])
        x = x_ref[...] + gate_ref[0] * o
        o_ref[...] = _ffn_rows(x, g2_ref[...], sh2_ref[0], sc2_ref[0], gate2_ref[0], wg_ref, wu_ref, wd_ref)

    i = pl.program_id(0)

    @pl.when(i < steps_ctx)
    def _():
        tail(*ctx_in, o_ctx)

    @pl.when(i >= steps_ctx)
    def _():
        tail(*lat_in, o_lat)


def _even_tail(ctx, lat, lat_seq, d, glu_w, glu_b, w_out, g_ffn, wg, wu, wd, mod, layer, casts):
    tm = FFN_ROWS
    sc, sl = ctx[3].shape[0] // tm, lat[3].shape[0] // tm
    at_ctx = lambda i: jnp.minimum(i, sc - 1)
    at_lat = lambda i: jnp.maximum(i - sc, 0)
    row_fn = lambda i: jnp.where(i < sc, 0, 1 + (at_lat(i) * tm) // lat_seq)
    c_in, c_out, c_shape, c_ops = _cast_plan(casts, sc, at_ctx)

    def specs(at):
        half = pl.BlockSpec((tm, NA_WIDTH), lambda i: (at(i), 0))
        full = pl.BlockSpec((tm, D_MODEL), lambda i: (at(i), 0))
        return [half, half, half, full], full

    in_ctx, out_ctx = specs(at_ctx)
    in_lat, out_lat = specs(at_lat)
    return pl.pallas_call(
        functools.partial(_even_tail_kernel, steps_ctx=sc),
        grid=(sc + sl,),
        in_specs=in_ctx + in_lat
                 + [_full_spec((1, S5_WIDTH)), _resident_spec(glu_w.shape), _full_spec((1, S5_WIDTH)),
                    _resident_spec(w_out.shape), _mod_spec(layer, 2, row_fn)]
                 + _ffn_specs(layer, row_fn, wg, wu, wd) + c_in,
        out_specs=[out_ctx, out_lat] + c_out,
        out_shape=[jax.ShapeDtypeStruct(ctx[3].shape, F32), jax.ShapeDtypeStruct(lat[3].shape, F32)] + c_shape,
        compiler_params=_params(("arbitrary",), vmem=EVEN_TAIL_VMEM),
        name="even_tail",
    )(*ctx, *lat, d, glu_w, glu_b, w_out, mod, g_ffn, mod, mod, mod, wg, wu, wd, *c_ops)


def _dft_consts(seq):
    def cs(n):
        k = np.arange(n)
        ang = 2.0 * np.pi * ((k[:, None] * k[None, :]) % n) / n
        return np.cos(ang), np.sin(ang)

    cc, sc = cs(FNET_GROUP_WIDTH)
    cl, sl = cs(seq)
    chan = jnp.asarray(np.concatenate([cc, sc], axis=1), dtype=F32)
    pos = jnp.asarray(np.concatenate([cl, -sl], axis=1), dtype=F32)
    return chan.astype(BF16), pos.astype(BF16)


def _fourier_rows(x, g, shift, scale, gate, win_ref, chan_ref, pos_ref, wout_ref):
    seq = x.shape[0]
    h = _norm_mod(x, g, shift, scale).astype(BF16)
    z = _dot(h, win_ref[...]).astype(BF16)
    w = FNET_GROUP_WIDTH
    pos = pos_ref[...]
    parts = []
    for c in range(FNET_GROUPS):
        ab = _dot(z[:, c * w:(c + 1) * w], chan_ref[...]).astype(BF16)
        stacked = jnp.concatenate([ab[:, :w], ab[:, w:]], axis=0)
        parts.append(_dot(pos, stacked))
    f = jnp.concatenate(parts, axis=1) * (1.0 / math.sqrt(seq * w))
    return x + gate * _dot(f.astype(BF16), wout_ref[...])


def _odd_layer_kernel(x_ref, g_ref, sh_ref, sc_ref, win_ref, chan_ref, pos_ref, wout_ref, gate_ref,
                      g2_ref, sh2_ref, sc2_ref, gate2_ref, wg_ref, wu_ref, wd_ref, fg_ref, o_ref, *, seq):
    tm = x_ref.shape[0]
    for r0 in range(0, tm, seq):
        rows = slice(r0, r0 + seq)
        o_ref[rows, :] = _fourier_rows(x_ref[rows, :], g_ref[...], sh_ref[0], sc_ref[0], gate_ref[0],
                                       win_ref, chan_ref, pos_ref, wout_ref)
    for r0 in range(0, tm, FFN_ROWS):
        rows = slice(r0, r0 + FFN_ROWS)
        y = _ffn_rows(o_ref[rows, :], g2_ref[...], sh2_ref[0], sc2_ref[0], gate2_ref[0], wg_ref, wu_ref, wd_ref)
        ms = jnp.mean(y * y, axis=-1, keepdims=True)
        o_ref[rows, :] = y * lax.rsqrt(ms + EPS) * fg_ref[...]


def _odd_layer(x, g_mix, g_ffn, final_g, mod, layer, row_of, w_in, w_out, wg, wu, wd, seq):
    t = x.shape[0]
    tm = max(seq, FFN_ROWS)
    row_fn = row_of(tm)
    chan, pos = _dft_consts(seq)
    blk = pl.BlockSpec((tm, D_MODEL), lambda i: (i, 0))
    return pl.pallas_call(
        functools.partial(_odd_layer_kernel, seq=seq),
        grid=(t // tm,),
        in_specs=[blk, _full_spec((1, D_MODEL)), _mod_spec(layer, 0, row_fn), _mod_spec(layer, 1, row_fn),
                  _resident_spec(w_in.shape), _resident_spec(chan.shape), _resident_spec(pos.shape),
                  _resident_spec(w_out.shape), _mod_spec(layer, 2, row_fn)]
                 + _ffn_specs(layer, row_fn, wg, wu, wd) + [_full_spec((1, D_MODEL))],
        out_specs=blk,
        out_shape=jax.ShapeDtypeStruct((t, D_MODEL), F32),
        compiler_params=_params(("parallel",)),
        name="odd_layer",
    )(x, g_mix, mod, mod, w_in, chan, pos, w_out, mod, g_ffn, mod, mod, mod, wg, wu, wd, final_g)


def kernel(x_prompt, x_sample, cache_na_k, cache_na_v, state_s5, c, c_ctx, w_mod, b_mod, norm_mix_g, norm_ffn_g, w_in_even, na_rpb, s5_lam_re, s5_lam_im, s5_log_step, s5_b_re, s5_b_im, s5_c_re, s5_c_im, s5_d, s5_glu_w, s5_glu_b, w_out_even, w_in_odd, w_out_odd, ffn_w_gate, ffn_w_up, ffn_w_down, final_norm_g):
    bp, lp, _ = x_prompt.shape
    bs, ls, _ = x_sample.shape
    depth = w_mod.shape[0]
    assert bs + 1 <= N_COND and depth == 2

    cond = jnp.concatenate([c_ctx[None], c, jnp.zeros((COND_ROWS - 1 - bs, D_MODEL), F32)], axis=0)
    mod = _modulation(cond.T, w_mod, b_mod).reshape(depth * COND_ROWS * 6, 1, D_MODEL)

    streams = [
        dict(x=x_prompt.reshape(bp * lp, D_MODEL), nb=bp, seq=lp, row_of=lambda tm: (lambda i: 0)),
        dict(x=x_sample.reshape(bs * ls, D_MODEL), nb=bs, seq=ls, row_of=lambda tm: (lambda i: 1 + (i * tm) // ls)),
    ]
    tm = FFN_ROWS

    e = 0
    w_in = w_in_even[e].astype(BF16)
    glu_b = s5_glu_b[e].reshape(1, S5_WIDTH)
    g_mix = norm_mix_g[0].reshape(1, D_MODEL)
    g_ffn0 = norm_ffn_g[0].reshape(1, D_MODEL)
    s5_ops = _s5_prep(s5_lam_re[e], s5_lam_im[e], s5_log_step[e], s5_b_re[e], s5_b_im[e], s5_c_re[e], s5_c_im[e])
    s5_dvec = s5_d[e].reshape(1, S5_WIDTH)
    bias = _na_bias(na_rpb[e], ls)
    layer0_w = [(ffn_w_gate, 0), (ffn_w_up, 0), (ffn_w_down, 0), (w_out_even, e), (s5_glu_w, e)]
    layer1_w = [(ffn_w_gate, 1), (ffn_w_up, 1), (ffn_w_down, 1), (w_in_odd, 0), (w_out_odd, 0)]

    new_k = new_v = new_s = None
    for si, st in enumerate(streams):
        nb, seq = st["nb"], st["seq"]
        if si == 0:
            a, u, new_k, new_v, *layer0_w = _inproj(st["x"], g_mix, mod, 0, st["row_of"](tm), w_in, tm,
                                                    ctx_seq=seq, casts=layer0_w)
            wg0, wu0, wd0, w_out, glu_w = layer0_w
            s0 = None
        else:
            q, k, v, u = _inproj(st["x"], g_mix, mod, 0, st["row_of"](tm), w_in, tm)
            a = _na_attention(q.reshape(nb, seq, NA_WIDTH), k.reshape(nb, seq, NA_WIDTH), v.reshape(nb, seq, NA_WIDTH),
                              cache_na_k, cache_na_v, e, bias).reshape(nb * seq, NA_WIDTH)
            s0 = state_s5[:, e].astype(F32)
        y, state = _s5_mixer(u, s0, s5_ops, nb, seq)
        if si == 0:
            new_s = state[:, None]
        st["mixed"] = (a, y, u, st["x"])

    tail = _even_tail(streams[0]["mixed"], streams[1]["mixed"], ls, s5_dvec, glu_w, glu_b, w_out, g_ffn0,
                      wg0, wu0, wd0, mod, 0, layer1_w)
    streams[0]["x"], streams[1]["x"] = tail[:2]
    wg1, wu1, wd1, w_in1, w_out1 = tail[2:]

    for st in streams:
        st["x"] = _odd_layer(st["x"], norm_mix_g[1].reshape(1, D_MODEL), norm_ffn_g[1].reshape(1, D_MODEL),
                             final_norm_g.reshape(1, D_MODEL), mod, 1, st["row_of"],
                             w_in1, w_out1, wg1, wu1, wd1, st["seq"])

    y_prompt = streams[0]["x"].reshape(bp, lp, D_MODEL)
    y_sample = streams[1]["x"].reshape(bs, ls, D_MODEL)
    return (y_prompt, y_sample, new_k, new_v, new_s)
```

```python
import functools
import math

import numpy as np
import jax
import jax.numpy as jnp
from jax import lax
from jax.experimental import pallas as pl
from jax.experimental.pallas import tpu as pltpu

F32 = jnp.float32
BF16 = jnp.bfloat16

D_MODEL = 1024
NA_HEADS = 8
HEAD_DIM = 64
NA_WIDTH = NA_HEADS * HEAD_DIM
GRID_W = 64
WIN_R = 8
WIN_C = 16
S5_GROUP = 16
S5_GROUPS = 32
S5_STATE = 64
S5_WIDTH = S5_GROUPS * S5_GROUP
FNET_GROUPS = 4
FNET_GROUP_WIDTH = D_MODEL // FNET_GROUPS
D_FF = 2816
EPS = 1e-6

LANES = 128
CHUNK = 16
CW = CHUNK * S5_GROUP
GROUPS_PER_BLOCK = LANES // S5_GROUP
S5_PREP_GROUPS = 4
NA_Q_BLOCK = 256
N_COND = 3
COND_ROWS = 8
NEG = -1e30
VMEM_LIMIT = 56 * 2 ** 20
EVEN_TAIL_VMEM = 60 * 2 ** 20

_NT = (((1,), (1,)), ((), ()))


def _params(sem, vmem=VMEM_LIMIT):
    return pltpu.CompilerParams(dimension_semantics=sem, vmem_limit_bytes=vmem)


def _dot(a, b):
    return jnp.dot(a, b, preferred_element_type=F32)


def _dot_nt(a, b):
    return lax.dot_general(a, b, _NT, preferred_element_type=F32)


def _norm_mod(x, g, shift, scale):
    ms = jnp.mean(x * x, axis=-1, keepdims=True)
    return (x * lax.rsqrt(ms + EPS) * g) * (1.0 + scale) + shift


def _mod_spec(layer, which, row_fn):
    return pl.BlockSpec((1, 1, D_MODEL),
                        lambda *idx: (layer * COND_ROWS * 6 + row_fn(*idx) * 6 + which, 0, 0))


def _full_spec(shape):
    nd = len(shape)
    return pl.BlockSpec(shape, lambda *idx: (0,) * nd)


def _resident_spec(shape):
    nd = len(shape)
    return pl.BlockSpec(shape, lambda *idx: (0,) * nd, pipeline_mode=pl.Buffered(1))


def _cast_plan(weights, steps, slab_of=lambda i: i):
    in_specs, out_specs, out_shape, operands = [], [], [], []
    for w in weights:
        w, layer = w if isinstance(w, tuple) else (w, None)
        rows, cols = w.shape[-2:]
        slab = rows // steps
        if layer is None:
            in_specs.append(pl.BlockSpec((slab, cols), lambda i: (slab_of(i), 0)))
        else:
            in_specs.append(pl.BlockSpec((None, slab, cols), lambda i, layer=layer: (layer, slab_of(i), 0)))
        out_specs.append(pl.BlockSpec((slab, cols), lambda i: (slab_of(i), 0)))
        out_shape.append(jax.ShapeDtypeStruct((rows, cols), BF16))
        operands.append(w)
    return in_specs, out_specs, out_shape, operands


def _run_casts(src_refs, dst_refs):
    for s, d in zip(src_refs, dst_refs):
        d[...] = s[...].astype(d.dtype)


MOD_COLS = 2048


def _mod_kernel(ct_ref, w_ref, b_ref, o_ref):
    ct = ct_ref[...]
    st = ct * jax.nn.sigmoid(ct)
    w = w_ref[0]
    o_ref[0] = jnp.zeros(o_ref.shape[1:], F32)
    for r in range(N_COND):
        o_ref[0, r:r + 1, :] = jnp.sum(w * st[:, r:r + 1], axis=0, keepdims=True) + b_ref[0]


def _modulation(cond_t, w_mod, b_mod):
    depth, _, n = w_mod.shape
    tn = MOD_COLS
    return pl.pallas_call(
        _mod_kernel,
        grid=(depth, n // tn),
        in_specs=[_full_spec((D_MODEL, COND_ROWS)),
                  pl.BlockSpec((1, D_MODEL, tn), lambda l, j: (l, 0, j)),
                  pl.BlockSpec((1, 1, tn), lambda l, j: (l, 0, j))],
        out_specs=pl.BlockSpec((1, COND_ROWS, tn), lambda l, j: (l, 0, j)),
        out_shape=jax.ShapeDtypeStruct((depth, COND_ROWS, n), F32),
        compiler_params=_params(("parallel", "parallel")),
        name="modulation",
    )(cond_t, w_mod, b_mod.reshape(depth, 1, n))


def _ctx_attention_rows(q, k, v, seq):
    scale = HEAD_DIM ** -0.5
    nseq = q.shape[0] // seq
    qt = (q * scale).T.astype(BF16)
    vt = v.T.astype(BF16)
    kb = k.astype(BF16)
    probs = [(slice(i * seq, (i + 1) * seq), slice(h * HEAD_DIM, (h + 1) * HEAD_DIM))
             for i in range(nseq) for h in range(NA_HEADS)]
    sts = [_dot(kb[rows, sl], qt[sl, rows]) for rows, sl in probs]
    pts = [jnp.exp(st - jnp.max(st, axis=0, keepdims=True)) for st in sts]
    dens = [jnp.sum(pt, axis=0, keepdims=True) for pt in pts]
    outs = [_dot(vt[sl, rows], pt.astype(BF16)) / den for (rows, sl), pt, den in zip(probs, pts, dens)]
    per_seq = [jnp.concatenate(outs[i * NA_HEADS:(i + 1) * NA_HEADS], axis=0).T for i in range(nseq)]
    return jnp.concatenate(per_seq, axis=0)


def _inproj_kernel(x_ref, g_ref, sh_ref, sc_ref, w_ref, *refs, ctx_seq, n_cast):
    _run_casts(refs[:n_cast], refs[n_cast + 4:2 * n_cast + 4])
    refs = refs[n_cast:n_cast + 4] + refs[2 * n_cast + 4:]
    h = _norm_mod(x_ref[...], g_ref[...], sh_ref[0], sc_ref[0]).astype(BF16)
    z = _dot(h, w_ref[...])
    q, k, v, u = (z[:, i * NA_WIDTH:(i + 1) * NA_WIDTH] for i in range(4))
    if ctx_seq is None:
        for o, val in zip(refs, (q, k, v, u)):
            o[...] = val.astype(o.dtype)
        return
    a_ref, u_ref, kc_ref, vc_ref, rows_scr = refs
    a_ref[...] = _ctx_attention_rows(q, k, v, ctx_seq).astype(a_ref.dtype)
    u_ref[...] = u
    tm = z.shape[0]
    pair = 2 * HEAD_DIM
    for o, val in ((kc_ref, k), (vc_ref, v)):
        for j in range(NA_HEADS // 2):
            both = val[:, j * pair:(j + 1) * pair]
            rows_scr[pl.ds(2 * j, tm, stride=NA_HEADS), :] = both
            rows_scr[pl.ds(2 * j + 1, tm, stride=NA_HEADS), :] = pltpu.roll(both, HEAD_DIM, axis=1)
        for b in range(o.shape[0]):
            blk = rows_scr[b * ctx_seq * NA_HEADS:(b + 1) * ctx_seq * NA_HEADS, 0:HEAD_DIM]
            o[b, 0] = blk.reshape(ctx_seq, NA_HEADS, HEAD_DIM)


def _inproj(x, g, mod, layer, row_fn, w, tm, ctx_seq=None, casts=()):
    t = x.shape[0]
    c_in, c_out, c_shape, c_ops = _cast_plan(casts, t // tm)
    flat = pl.BlockSpec((tm, NA_WIDTH), lambda i: (i, 0))
    scratch = []
    if ctx_seq is None:
        out_specs = [flat] * 4
        out_shape = [jax.ShapeDtypeStruct((t, NA_WIDTH), BF16)] + [jax.ShapeDtypeStruct((t, NA_WIDTH), F32)] * 3
    else:
        per = tm // ctx_seq
        cache = pl.BlockSpec((per, 1, ctx_seq, NA_HEADS, HEAD_DIM), lambda i: (i, 0, 0, 0, 0))
        out_specs = [flat, flat, cache, cache]
        out_shape = ([jax.ShapeDtypeStruct((t, NA_WIDTH), BF16), jax.ShapeDtypeStruct((t, NA_WIDTH), F32)]
                     + [jax.ShapeDtypeStruct((t // ctx_seq, 1, ctx_seq, NA_HEADS, HEAD_DIM), F32)] * 2)
        scratch = [pltpu.VMEM((tm * NA_HEADS, 2 * HEAD_DIM), F32)]
    return pl.pallas_call(
        functools.partial(_inproj_kernel, ctx_seq=ctx_seq, n_cast=len(c_ops)),
        grid=(t // tm,),
        in_specs=[pl.BlockSpec((tm, D_MODEL), lambda i: (i, 0)),
                  _full_spec((1, D_MODEL)),
                  _mod_spec(layer, 0, row_fn), _mod_spec(layer, 1, row_fn),
                  _resident_spec(w.shape)] + c_in,
        out_specs=out_specs + c_out,
        out_shape=out_shape + c_shape,
        scratch_shapes=scratch,
        compiler_params=_params(("parallel",)),
        name="inproj_even",
    )(x, g, mod, mod, w, *c_ops)


def _na_bias_kernel(rpb_ref, o_ref, *, rows):
    h = pl.program_id(0)
    kr_win = min(WIN_R, rows)
    qc = lax.broadcasted_iota(jnp.int32, (GRID_W, 2 * GRID_W), 0)
    lane = lax.broadcasted_iota(jnp.int32, (GRID_W, 2 * GRID_W), 1)
    kc = lane & (GRID_W - 1)
    cs = jnp.clip(qc - WIN_C // 2, 0, GRID_W - WIN_C)
    col_ok = (kc >= cs) & (kc < cs + WIN_C)
    dc = kc - qc + (WIN_C - 1)
    left = lane < GRID_W
    neg = jnp.full((GRID_W, 2 * GRID_W), NEG, F32)
    toeplitz = []
    for dr in range(2 * WIN_R - 1):
        t = neg
        for d in range(2 * WIN_C - 1):
            t = jnp.where(dc == d, rpb_ref[h, dr, d], t)
        toeplitz.append(jnp.where(col_ok, t, neg))
    for r in range(rows):
        rs = min(max(r - kr_win // 2, 0), rows - kr_win)

        def blk(kr):
            return toeplitz[kr - r + WIN_R - 1] if rs <= kr < rs + kr_win else None

        for j in range(rows // 2):
            a, b = blk(2 * j), blk(2 * j + 1)
            if a is None and b is None:
                pair = neg
            else:
                pair = jnp.where(left, neg if a is None else a, neg if b is None else b)
            o_ref[0, r * GRID_W:(r + 1) * GRID_W, j * 2 * GRID_W:(j + 1) * 2 * GRID_W] = pair


def _na_bias(rpb, seq):
    rows = seq // GRID_W
    return pl.pallas_call(
        functools.partial(_na_bias_kernel, rows=rows),
        grid=(NA_HEADS,),
        in_specs=[pl.BlockSpec(memory_space=pltpu.SMEM)],
        out_specs=pl.BlockSpec((1, seq, seq), lambda h: (h, 0, 0)),
        out_shape=jax.ShapeDtypeStruct((NA_HEADS, seq, seq), F32),
        compiler_params=_params(("parallel",)),
        name="na_bias",
    )(rpb)


def _na_key_range(qb, tq, rows):
    kr_win = min(WIN_R, rows)
    r0, r1 = qb * tq // GRID_W, ((qb + 1) * tq - 1) // GRID_W
    rs0 = min(max(r0 - kr_win // 2, 0), rows - kr_win)
    rs1 = min(max(r1 - kr_win // 2, 0), rows - kr_win)
    lo, hi = rs0 * GRID_W, (rs1 + kr_win) * GRID_W
    return lo // LANES * LANES, -(-hi // LANES) * LANES


def _na_attn_kernel(q_ref, k_ref, v_ref, ck_ref, cv_ref, bias_ref, o_ref, *, tq):
    scale = HEAD_DIM ** -0.5
    seq = q_ref.shape[1]
    probs = []
    for hl in range(2):
        sl = slice(hl * HEAD_DIM, (hl + 1) * HEAD_DIM)
        for qb in range(seq // tq):
            lo, hi = _na_key_range(qb, tq, seq // GRID_W)
            probs.append((hl, sl, slice(qb * tq, (qb + 1) * tq), slice(lo, hi)))
    head0 = pl.program_id(0) * 2
    cks = [ck_ref[0, 0, :, head0 + hl, :].astype(BF16) for hl in range(2)]
    cvs = [cv_ref[0, 0, :, head0 + hl, :].astype(BF16) for hl in range(2)]
    qs_all = [(q_ref[0, qs, sl] * scale).astype(BF16) for _, sl, qs, _ in probs]
    s_loc = [_dot_nt(q, k_ref[0, ks, sl].astype(BF16)) + bias_ref[hl, qs, ks]
             for q, (hl, sl, qs, ks) in zip(qs_all, probs)]
    s_ctx = [_dot_nt(q, cks[hl]) for q, (hl, _, _, _) in zip(qs_all, probs)]
    ms = [jnp.maximum(jnp.max(a, axis=-1, keepdims=True), jnp.max(b, axis=-1, keepdims=True))
          for a, b in zip(s_loc, s_ctx)]
    p_loc = [jnp.exp(a - m) for a, m in zip(s_loc, ms)]
    p_ctx = [jnp.exp(b - m) for b, m in zip(s_ctx, ms)]
    dens = [jnp.sum(a, axis=-1, keepdims=True) + jnp.sum(b, axis=-1, keepdims=True) for a, b in zip(p_loc, p_ctx)]
    for a, b, den, (hl, sl, qs, ks) in zip(p_loc, p_ctx, dens, probs):
        o = _dot(a.astype(BF16), v_ref[0, ks, sl].astype(BF16)) + _dot(b.astype(BF16), cvs[hl])
        o_ref[0, qs, sl] = o / den


def _na_attention(q, k, v, ck, cv, layer, bias):
    nb, seq, _ = q.shape
    lat = pl.BlockSpec((1, seq, 2 * HEAD_DIM), lambda hp, b: (b, 0, hp))
    ctx = pl.BlockSpec((1, 1) + ck.shape[2:], lambda hp, b: (b, layer, 0, 0, 0))
    return pl.pallas_call(
        functools.partial(_na_attn_kernel, tq=NA_Q_BLOCK),
        grid=(NA_HEADS // 2, nb),
        in_specs=[lat, lat, lat, ctx, ctx, pl.BlockSpec((2, seq, seq), lambda hp, b: (hp, 0, 0))],
        out_specs=lat,
        out_shape=jax.ShapeDtypeStruct((nb, seq, NA_WIDTH), F32),
        compiler_params=_params(("parallel", "arbitrary")),
        name="na_attention",
    )(q, k, v, ck, cv, bias)


S5_PARAM_ROWS = 4 * S5_GROUP + 8


def _s5_prep_group(i, p_ref, m_ref, wz_ref, wyt_ref, ar_ref, ai_ref):
    n = S5_GROUP
    lr = jnp.minimum(p_ref[i, 4 * n:4 * n + 1], -1e-4)
    li = p_ref[i, 4 * n + 1:4 * n + 2]
    dt = jnp.exp(p_ref[i, 4 * n + 2:4 * n + 3])
    a = lr * dt
    th = li * dt
    mag = jnp.exp(a)
    lbr = mag * jnp.cos(th)
    lbi = mag * jnp.sin(th)
    den = lr * lr + li * li
    nr = lbr - 1.0
    coef_r = (nr * lr + lbi * li) / den
    coef_i = (lbi * lr - nr * li) / den
    btr, bti = p_ref[i, 0:n], p_ref[i, n:2 * n]
    bbr = coef_r * btr - coef_i * bti
    bbi = coef_r * bti + coef_i * btr
    cr, ci = p_ref[i, 2 * n:3 * n], p_ref[i, 3 * n:4 * n]

    step = lax.broadcasted_iota(jnp.int32, (CHUNK, 2 * S5_STATE), 0).astype(F32)
    fwd = lax.broadcasted_iota(jnp.int32, (CHUNK, 2 * S5_STATE), 1) < S5_STATE

    def powers(e):
        pm = jnp.exp(a * e)
        return pm * jnp.cos(th * e), pm * jnp.sin(th * e)

    pzr, pzi = powers(jnp.where(fwd, CHUNK - 1 - step, step))
    pyr, pyi = powers(jnp.where(fwd, step + 1.0, CHUNK - step))
    p16r, p16i = powers(jnp.full((1, 2 * S5_STATE), float(CHUNK), F32))
    ar_ref[i] = p16r
    ai_ref[i] = p16i

    def outer(p, c):
        return (p[:, None, :] * c[None, :, :]).reshape(CW, 2 * S5_STATE)

    wzr = outer(pzr, bbr) - outer(pzi, bbi)
    wzi = outer(pzr, bbi) + outer(pzi, bbr)
    re, im = slice(0, 2 * S5_STATE), slice(2 * S5_STATE, 4 * S5_STATE)
    wz_ref[i, :, re] = wzr.astype(wz_ref.dtype)
    wz_ref[i, :, im] = wzi.astype(wz_ref.dtype)
    wyt_ref[i, :, re] = (outer(pyr, cr) - outer(pyi, ci)).astype(wyt_ref.dtype)
    wyt_ref[i, :, im] = (-(outer(pyi, cr) + outer(pyr, ci))).astype(wyt_ref.dtype)

    ones = jnp.ones((CHUNK, 2 * S5_STATE), F32)
    cwr = outer(ones, cr)
    cwi = outer(ones, ci)
    fwd_rows = lax.broadcasted_iota(jnp.int32, (CW, 2 * S5_STATE), 1) < S5_STATE
    zero = jnp.zeros((CW, 2 * S5_STATE), F32)

    def split(x):
        hi = x.astype(BF16)
        return hi, (x - hi.astype(F32)).astype(BF16)

    cw_hi, cw_lo = split(jnp.concatenate([cwr, -cwi], axis=1))

    def kk(sel):
        lhs_hi, lhs_lo = split(jnp.concatenate([jnp.where(sel, wzr, zero), jnp.where(sel, wzi, zero)], axis=1))
        return _dot_nt(lhs_hi, cw_hi) + (_dot_nt(lhs_hi, cw_lo) + _dot_nt(lhs_lo, cw_hi))

    kk_f = kk(fwd_rows)
    kk_b = kk(jnp.logical_not(fwd_rows))
    lanes = 2 * S5_STATE
    steps_per_tile = lanes // S5_GROUP
    col_step = lax.broadcasted_iota(jnp.int32, (CW, lanes), 1) // S5_GROUP
    for tile in range(CW // lanes):
        cols = slice(tile * lanes, (tile + 1) * lanes)
        kf, kb = kk_f[:, cols], kk_b[:, cols]
        m = jnp.zeros((CW, lanes), F32)
        for tt in range(steps_per_tile):
            t = tile * steps_per_tile + tt
            up = (CHUNK - 1 - t) * S5_GROUP
            dn = t * S5_GROUP
            piece_f = kf if up == 0 else jnp.concatenate([kf[up:], jnp.zeros((up, lanes), F32)], axis=0)
            piece_b = kb if dn == 0 else jnp.concatenate([jnp.zeros((dn, lanes), F32), kb[:CW - dn]], axis=0)
            m = jnp.where(col_step == tt, piece_f + piece_b, m)
        m_ref[i, :, cols] = m.astype(m_ref.dtype)


def _s5_prep_kernel(*refs):
    for i in range(refs[0].shape[0]):
        _s5_prep_group(i, *refs)


def _s5_prep(lam_re, lam_im, log_step, b_re, b_im, c_re, c_im):
    g = S5_GROUPS
    lanes = 2 * S5_STATE

    def dirs_last(x):
        return x.transpose(1, 0, 2).reshape(g, 1, lanes)

    lamr, lami = dirs_last(lam_re), dirs_last(lam_im)
    logdt = dirs_last(jnp.broadcast_to(log_step[:, :, None], (2, g, S5_STATE)))
    bt = [x.transpose(1, 3, 0, 2).reshape(g, S5_GROUP, lanes) for x in (b_re, b_im)]
    ct = [x.transpose(1, 2, 0, 3).reshape(g, S5_GROUP, lanes) for x in (c_re, c_im)]
    pad = jnp.zeros((g, S5_PARAM_ROWS - 4 * S5_GROUP - 3, lanes), F32)
    packed = jnp.concatenate(bt + ct + [lamr, lami, logdt, pad], axis=1)
    gp = S5_PREP_GROUPS
    vec = pl.BlockSpec((gp, 1, lanes), lambda i: (i, 0, 0))
    big = pl.BlockSpec((gp, CW, CW), lambda i: (i, 0, 0))
    return pl.pallas_call(
        _s5_prep_kernel,
        grid=(g // gp,),
        in_specs=[pl.BlockSpec((gp, S5_PARAM_ROWS, lanes), lambda i: (i, 0, 0))],
        out_specs=[big, big, big, vec, vec],
        out_shape=[jax.ShapeDtypeStruct((g, CW, CW), BF16)] * 3 + [jax.ShapeDtypeStruct((g, 1, lanes), F32)] * 2,
        compiler_params=_params(("parallel",)),
        name="s5_prep",
    )(packed)


def _s5_kernel(u_ref, m_ref, wz_ref, wyt_ref, ar_ref, ai_ref, s0_ref, perm_ref, permt_ref, y_ref, fin_ref,
               xt_scr, z_scr, sp_scr, *, nbp, nc):
    s = S5_STATE
    r = xt_scr.shape[2]
    for t in range(CHUNK):
        xt_scr[t] = u_ref[pl.ds(t, r, stride=CHUNK), :].T
    fwd = lax.broadcasted_iota(jnp.int32, (nbp, 2 * s), 1) < s
    groups = range(GROUPS_PER_BLOCK)
    chans = [slice(gl * S5_GROUP, (gl + 1) * S5_GROUP) for gl in groups]

    us = [jnp.concatenate([xt_scr[t, ch, :] for t in range(CHUNK)], axis=0).T.astype(BF16) for ch in chans]
    for gl in groups:
        z_scr[gl] = _dot(_dot(perm_ref[...], us[gl]).astype(BF16), wz_ref[gl])

    ars = [jnp.broadcast_to(ar_ref[gl], (nbp, 2 * s)) for gl in groups]
    ais = [jnp.broadcast_to(ai_ref[gl], (nbp, 2 * s)) for gl in groups]
    srs = [s0_ref[gl, :, 0:2 * s] for gl in groups]
    sis = [s0_ref[gl, :, 2 * s:4 * s] for gl in groups]
    for k in range(nc):
        up = slice(k * nbp, (k + 1) * nbp)
        dn = slice((nc - 1 - k) * nbp, (nc - k) * nbp)
        for gl in groups:
            sr, si = srs[gl], sis[gl]
            sp_scr[gl, up, 0:s] = sr[:, 0:s]
            sp_scr[gl, dn, s:2 * s] = sr[:, s:2 * s]
            sp_scr[gl, up, 2 * s:3 * s] = si[:, 0:s]
            sp_scr[gl, dn, 3 * s:4 * s] = si[:, s:2 * s]
            zr = jnp.where(fwd, z_scr[gl, up, 0:2 * s], z_scr[gl, dn, 0:2 * s])
            zi = jnp.where(fwd, z_scr[gl, up, 2 * s:4 * s], z_scr[gl, dn, 2 * s:4 * s])
            srs[gl] = ars[gl] * sr - ais[gl] * si + zr
            sis[gl] = ars[gl] * si + ais[gl] * sr + zi
    for gl in groups:
        fin_ref[gl, :, 0:2 * s] = srs[gl]
        fin_ref[gl, :, 2 * s:4 * s] = sis[gl]
    sps = [_dot(permt_ref[...], sp_scr[gl].astype(BF16)).astype(BF16) for gl in groups]
    for gl in groups:
        yt = (_dot(us[gl], m_ref[gl]) + _dot_nt(sps[gl], wyt_ref[gl])).T
        for t in range(CHUNK):
            xt_scr[t, chans[gl], :] = yt[t * S5_GROUP:(t + 1) * S5_GROUP, :]

    for t in range(CHUNK):
        y_ref[pl.ds(t, r, stride=CHUNK), :] = xt_scr[t].T


def _s5_mixer(u, s0, ops, nb, seq):
    m, wz, wyt, ar, ai = ops
    g, nc = S5_GROUPS, seq // CHUNK
    nbp = -(-nb // 8) * 8
    r, rp = nb * nc, nc * nbp
    gpb = GROUPS_PER_BLOCK
    s0l = jnp.zeros((g, nbp, CW), F32)
    if s0 is not None:
        s0l = s0l.at[:, :nb].set(s0.transpose(3, 0, 2, 1, 4).reshape(g, nb, CW))
    perm = np.zeros((nc, nbp, nb, nc), np.float32)
    for b in range(nb):
        perm[np.arange(nc), b, b, np.arange(nc)] = 1.0
    perm = jnp.asarray(perm.reshape(rp, r)).astype(BF16)
    col = pl.BlockSpec((nb * seq, gpb * S5_GROUP), lambda j: (0, j))
    wspec = pl.BlockSpec((gpb, CW, CW), lambda j: (j, 0, 0))
    cspec = pl.BlockSpec((gpb, 1, 2 * S5_STATE), lambda j: (j, 0, 0))
    sspec = pl.BlockSpec((gpb, nbp, CW), lambda j: (j, 0, 0))
    y, fin = pl.pallas_call(
        functools.partial(_s5_kernel, nbp=nbp, nc=nc),
        grid=(g // gpb,),
        in_specs=[col, wspec, wspec, wspec, cspec, cspec, sspec, _full_spec((rp, r)), _full_spec((r, rp))],
        out_specs=[col, sspec],
        out_shape=[jax.ShapeDtypeStruct((nb * seq, S5_WIDTH), F32), jax.ShapeDtypeStruct((g, nbp, CW), F32)],
        scratch_shapes=[pltpu.VMEM((CHUNK, gpb * S5_GROUP, r), F32),
                        pltpu.VMEM((gpb, rp, CW), F32), pltpu.VMEM((gpb, rp, CW), F32)],
        compiler_params=_params(("parallel",)),
        name="s5_scan",
    )(u, m, wz, wyt, ar, ai, s0l, perm, perm.T)
    state = fin[:, :nb].reshape(g, nb, 2, 2, S5_STATE).transpose(1, 3, 2, 0, 4)
    return y, state


FFN_ROWS = 512
FFN_CHUNK = 256


def _ffn_rows(x, g, shift, scale, gate, wg_ref, wu_ref, wd_ref):
    h = _norm_mod(x, g, shift, scale).astype(BF16)
    acc = jnp.zeros(x.shape, F32)
    for j in range(D_FF // FFN_CHUNK):
        sl = slice(j * FFN_CHUNK, (j + 1) * FFN_CHUNK)
        gg = _dot(h, wg_ref[:, sl])
        uu = _dot(h, wu_ref[:, sl])
        act = (gg * jax.nn.sigmoid(gg) * uu).astype(BF16)
        acc = acc + _dot(act, wd_ref[sl, :])
    return x + gate * acc


def _ffn_specs(layer, row_fn, wg, wu, wd):
    return [_full_spec((1, D_MODEL)),
            _mod_spec(layer, 3, row_fn), _mod_spec(layer, 4, row_fn), _mod_spec(layer, 5, row_fn),
            _resident_spec(wg.shape), _resident_spec(wu.shape), _resident_spec(wd.shape)]


def _even_tail_kernel(*refs, steps_ctx):
    ctx_in, lat_in = refs[0:4], refs[4:8]
    d_ref, gw_ref, gb_ref, wo_ref, gate_ref, g2_ref, sh2_ref, sc2_ref, gate2_ref, wg_ref, wu_ref, wd_ref = refs[8:20]
    n_cast = (len(refs) - 22) // 2
    o_ctx, o_lat = refs[20 + n_cast:22 + n_cast]
    _run_casts(refs[20:20 + n_cast], refs[22 + n_cast:])

    def tail(a_ref, y_ref, u_ref, x_ref, o_ref):
        y = jax.nn.gelu(y_ref[...] + d_ref[...] * u_ref[...])
        y = y * jax.nn.sigmoid(_dot(y.astype(BF16), gw_ref[...]) + gb_ref[...])
        o = _dot(a_ref[...].astype(BF16), wo_ref[0:NA_WIDTH, :]) + _dot(y.astype(BF16), wo_ref[NA_WIDTH:, :])
        x = x_ref[...] + gate_ref[0] * o
        o_ref[...] = _ffn_rows(x, g2_ref[...], sh2_ref[0], sc2_ref[0], gate2_ref[0], wg_ref, wu_ref, wd_ref)

    i = pl.program_id(0)

    @pl.when(i < steps_ctx)
    def _():
        tail(*ctx_in, o_ctx)

    @pl.when(i >= steps_ctx)
    def _():
        tail(*lat_in, o_lat)


def _even_tail(ctx, lat, lat_seq, d, glu_w, glu_b, w_out, g_ffn, wg, wu, wd, mod, layer, casts):
    tm = FFN_ROWS
    sc, sl = ctx[3].shape[0] // tm, lat[3].shape[0] // tm
    at_ctx = lambda i: jnp.minimum(i, sc - 1)
    at_lat = lambda i: jnp.maximum(i - sc, 0)
    row_fn = lambda i: jnp.where(i < sc, 0, 1 + (at_lat(i) * tm) // lat_seq)
    c_in, c_out, c_shape, c_ops = _cast_plan(casts, sc, at_ctx)

    def specs(at):
        half = pl.BlockSpec((tm, NA_WIDTH), lambda i: (at(i), 0))
        full = pl.BlockSpec((tm, D_MODEL), lambda i: (at(i), 0))
        return [half, half, half, full], full

    in_ctx, out_ctx = specs(at_ctx)
    in_lat, out_lat = specs(at_lat)
    return pl.pallas_call(
        functools.partial(_even_tail_kernel, steps_ctx=sc),
        grid=(sc + sl,),
        in_specs=in_ctx + in_lat
                 + [_full_spec((1, S5_WIDTH)), _resident_spec(glu_w.shape), _full_spec((1, S5_WIDTH)),
                    _resident_spec(w_out.shape), _mod_spec(layer, 2, row_fn)]
                 + _ffn_specs(layer, row_fn, wg, wu, wd) + c_in,
        out_specs=[out_ctx, out_lat] + c_out,
        out_shape=[jax.ShapeDtypeStruct(ctx[3].shape, F32), jax.ShapeDtypeStruct(lat[3].shape, F32)] + c_shape,
        compiler_params=_params(("arbitrary",), vmem=EVEN_TAIL_VMEM),
        name="even_tail",
    )(*ctx, *lat, d, glu_w, glu_b, w_out, mod, g_ffn, mod, mod, mod, wg, wu, wd, *c_ops)


def _dft_consts(seq):
    def cs(n):
        k = np.arange(n)
        ang = 2.0 * np.pi * ((k[:, None] * k[None, :]) % n) / n
        return np.cos(ang), np.sin(ang)

    cc, sc = cs(FNET_GROUP_WIDTH)
    cl, sl = cs(seq)
    chan = jnp.asarray(np.concatenate([cc, sc], axis=1), dtype=F32)
    pos = jnp.asarray(np.concatenate([cl, -sl], axis=1), dtype=F32)
    return chan.astype(BF16), pos.astype(BF16)


def _fourier_rows(x, g, shift, scale, gate, win_ref, chan_ref, pos_ref, wout_ref):
    seq = x.shape[0]
    h = _norm_mod(x, g, shift, scale).astype(BF16)
    z = _dot(h, win_ref[...]).astype(BF16)
    w = FNET_GROUP_WIDTH
    pos = pos_ref[...]
    parts = []
    for c in range(FNET_GROUPS):
        ab = _dot(z[:, c * w:(c + 1) * w], chan_ref[...]).astype(BF16)
        stacked = jnp.concatenate([ab[:, :w], ab[:, w:]], axis=0)
        parts.append(_dot(pos, stacked))
    f = jnp.concatenate(parts, axis=1) * (1.0 / math.sqrt(seq * w))
    return x + gate * _dot(f.astype(BF16), wout_ref[...])


def _odd_layer_kernel(x_ref, g_ref, sh_ref, sc_ref, win_ref, chan_ref, pos_ref, wout_ref, gate_ref,
                      g2_ref, sh2_ref, sc2_ref, gate2_ref, wg_ref, wu_ref, wd_ref, fg_ref, o_ref, *, seq):
    tm = x_ref.shape[0]
    for r0 in range(0, tm, seq):
        rows = slice(r0, r0 + seq)
        o_ref[rows, :] = _fourier_rows(x_ref[rows, :], g_ref[...], sh_ref[0], sc_ref[0], gate_ref[0],
                                       win_ref, chan_ref, pos_ref, wout_ref)
    for r0 in range(0, tm, FFN_ROWS):
        rows = slice(r0, r0 + FFN_ROWS)
        y = _ffn_rows(o_ref[rows, :], g2_ref[...], sh2_ref[0], sc2_ref[0], gate2_ref[0], wg_ref, wu_ref, wd_ref)
        ms = jnp.mean(y * y, axis=-1, keepdims=True)
        o_ref[rows, :] = y * lax.rsqrt(ms + EPS) * fg_ref[...]


def _odd_layer(x, g_mix, g_ffn, final_g, mod, layer, row_of, w_in, w_out, wg, wu, wd, seq):
    t = x.shape[0]
    tm = max(seq, FFN_ROWS)
    row_fn = row_of(tm)
    chan, pos = _dft_consts(seq)
    blk = pl.BlockSpec((tm, D_MODEL), lambda i: (i, 0))
    return pl.pallas_call(
        functools.partial(_odd_layer_kernel, seq=seq),
        grid=(t // tm,),
        in_specs=[blk, _full_spec((1, D_MODEL)), _mod_spec(layer, 0, row_fn), _mod_spec(layer, 1, row_fn),
                  _resident_spec(w_in.shape), _resident_spec(chan.shape), _resident_spec(pos.shape),
                  _resident_spec(w_out.shape), _mod_spec(layer, 2, row_fn)]
                 + _ffn_specs(layer, row_fn, wg, wu, wd) + [_full_spec((1, D_MODEL))],
        out_specs=blk,
        out_shape=jax.ShapeDtypeStruct((t, D_MODEL), F32),
        compiler_params=_params(("parallel",)),
        name="odd_layer",
    )(x, g_mix, mod, mod, w_in, chan, pos, w_out, mod, g_ffn, mod, mod, mod, wg, wu, wd, final_g)


def kernel(x_prompt, x_sample, cache_na_k, cache_na_v, state_s5, c, c_ctx, w_mod, b_mod, norm_mix_g, norm_ffn_g, w_in_even, na_rpb, s5_lam_re, s5_lam_im, s5_log_step, s5_b_re, s5_b_im, s5_c_re, s5_c_im, s5_d, s5_glu_w, s5_glu_b, w_out_even, w_in_odd, w_out_odd, ffn_w_gate, ffn_w_up, ffn_w_down, final_norm_g):
    bp, lp, _ = x_prompt.shape
    bs, ls, _ = x_sample.shape
    depth = w_mod.shape[0]
    assert bs + 1 <= N_COND and depth == 2

    cond = jnp.concatenate([c_ctx[None], c, jnp.zeros((COND_ROWS - 1 - bs, D_MODEL), F32)], axis=0)
    mod = _modulation(cond.T, w_mod, b_mod).reshape(depth * COND_ROWS * 6, 1, D_MODEL)

    streams = [
        dict(x=x_prompt.reshape(bp * lp, D_MODEL), nb=bp, seq=lp, row_of=lambda tm: (lambda i: 0)),
        dict(x=x_sample.reshape(bs * ls, D_MODEL), nb=bs, seq=ls, row_of=lambda tm: (lambda i: 1 + (i * tm) // ls)),
    ]
    tm = FFN_ROWS

    e = 0
    w_in = w_in_even[e].astype(BF16)
    glu_b = s5_glu_b[e].reshape(1, S5_WIDTH)
    g_mix = norm_mix_g[0].reshape(1, D_MODEL)
    g_ffn0 = norm_ffn_g[0].reshape(1, D_MODEL)
    s5_ops = _s5_prep(s5_lam_re[e], s5_lam_im[e], s5_log_step[e], s5_b_re[e], s5_b_im[e], s5_c_re[e], s5_c_im[e])
    s5_dvec = s5_d[e].reshape(1, S5_WIDTH)
    bias = _na_bias(na_rpb[e], ls)
    layer0_w = [(ffn_w_gate, 0), (ffn_w_up, 0), (ffn_w_down, 0), (w_out_even, e), (s5_glu_w, e)]
    layer1_w = [(ffn_w_gate, 1), (ffn_w_up, 1), (ffn_w_down, 1), (w_in_odd, 0), (w_out_odd, 0)]

    new_k = new_v = new_s = None
    for si, st in enumerate(streams):
        nb, seq = st["nb"], st["seq"]
        if si == 0:
            a, u, new_k, new_v, *layer0_w = _inproj(st["x"], g_mix, mod, 0, st["row_of"](tm), w_in, tm,
                                                    ctx_seq=seq, casts=layer0_w)
            wg0, wu0, wd0, w_out, glu_w = layer0_w
            s0 = None
        else:
            q, k, v, u = _inproj(st["x"], g_mix, mod, 0, st["row_of"](tm), w_in, tm)
            a = _na_attention(q.reshape(nb, seq, NA_WIDTH), k.reshape(nb, seq, NA_WIDTH), v.reshape(nb, seq, NA_WIDTH),
                              cache_na_k, cache_na_v, e, bias).reshape(nb * seq, NA_WIDTH)
            s0 = state_s5[:, e].astype(F32)
        y, state = _s5_mixer(u, s0, s5_ops, nb, seq)
        if si == 0:
            new_s = state[:, None]
        st["mixed"] = (a, y, u, st["x"])

    tail = _even_tail(streams[0]["mixed"], streams[1]["mixed"], ls, s5_dvec, glu_w, glu_b, w_out, g_ffn0,
                      wg0, wu0, wd0, mod, 0, layer1_w)
    streams[0]["x"], streams[1]["x"] = tail[:2]
    wg1, wu1, wd1, w_in1, w_out1 = tail[2:]

    for st in streams:
        st["x"] = _odd_layer(st["x"], norm_mix_g[1].reshape(1, D_MODEL), norm_ffn_g[1].reshape(1, D_MODEL),
                             final_norm_g.reshape(1, D_MODEL), mod, 1, st["row_of"],
                             w_in1, w_out1, wg1, wu1, wd1, st["seq"])

    y_prompt = streams[0]["x"].reshape(bp, lp, D_MODEL)
    y_sample = streams[1]["x"].reshape(bs, ls, D_MODEL)
    return (y_prompt, y_sample, new_k, new_v, new_s)
```

```python
import functools
import math

import numpy as np
import jax
import jax.numpy as jnp
from jax import lax
from jax.experimental import pallas as pl
from jax.experimental.pallas import tpu as pltpu

F32 = jnp.float32
BF16 = jnp.bfloat16

D_MODEL = 1024
NA_HEADS = 8
HEAD_DIM = 64
NA_WIDTH = NA_HEADS * HEAD_DIM
GRID_W = 64
WIN_R = 8
WIN_C = 16
S5_GROUP = 16
S5_GROUPS = 32
S5_STATE = 64
S5_WIDTH = S5_GROUPS * S5_GROUP
FNET_GROUPS = 4
FNET_GROUP_WIDTH = D_MODEL // FNET_GROUPS
D_FF = 2816
EPS = 1e-6

LANES = 128
CHUNK = 16
CW = CHUNK * S5_GROUP
GROUPS_PER_BLOCK = LANES // S5_GROUP
S5_PREP_GROUPS = 4
NA_Q_BLOCK = 256
N_COND = 3
COND_ROWS = 8
NEG = -1e30
VMEM_LIMIT = 56 * 2 ** 20
EVEN_TAIL_VMEM = 60 * 2 ** 20

_NT = (((1,), (1,)), ((), ()))


def _params(sem, vmem=VMEM_LIMIT):
    return pltpu.CompilerParams(dimension_semantics=sem, vmem_limit_bytes=vmem)


def _dot(a, b):
    return jnp.dot(a, b, preferred_element_type=F32)


def _dot_nt(a, b):
    return lax.dot_general(a, b, _NT, preferred_element_type=F32)


def _norm_mod(x, g, shift, scale):
    ms = jnp.mean(x * x, axis=-1, keepdims=True)
    return (x * lax.rsqrt(ms + EPS) * g) * (1.0 + scale) + shift


def _mod_spec(layer, which, row_fn):
    return pl.BlockSpec((1, 1, D_MODEL),
                        lambda *idx: (layer * COND_ROWS * 6 + row_fn(*idx) * 6 + which, 0, 0))


def _full_spec(shape):
    nd = len(shape)
    return pl.BlockSpec(shape, lambda *idx: (0,) * nd)


def _resident_spec(shape):
    nd = len(shape)
    return pl.BlockSpec(shape, lambda *idx: (0,) * nd, pipeline_mode=pl.Buffered(1))


def _cast_plan(weights, steps, slab_of=lambda i: i):
    in_specs, out_specs, out_shape, operands = [], [], [], []
    for w in weights:
        w, layer = w if isinstance(w, tuple) else (w, None)
        rows, cols = w.shape[-2:]
        slab = rows // steps
        if layer is None:
            in_specs.append(pl.BlockSpec((slab, cols), lambda i: (slab_of(i), 0)))
        else:
            in_specs.append(pl.BlockSpec((None, slab, cols), lambda i, layer=layer: (layer, slab_of(i), 0)))
        out_specs.append(pl.BlockSpec((slab, cols), lambda i: (slab_of(i), 0)))
        out_shape.append(jax.ShapeDtypeStruct((rows, cols), BF16))
        operands.append(w)
    return in_specs, out_specs, out_shape, operands


def _run_casts(src_refs, dst_refs):
    for s, d in zip(src_refs, dst_refs):
        d[...] = s[...].astype(d.dtype)


MOD_COLS = 1024
MOD_BUFFERS = 3


def _mod_kernel(ct_ref, w_hbm, b_ref, o_ref, w_buf, sem):
    depth, _, n = w_hbm.shape
    tiles = n // MOD_COLS
    blocks = depth * tiles

    def fetch(blk):
        layer, j = divmod(blk, tiles)
        slot = blk % MOD_BUFFERS
        return pltpu.make_async_copy(w_hbm.at[layer, :, pl.ds(j * MOD_COLS, MOD_COLS)], w_buf.at[slot], sem.at[slot])

    for blk in range(min(MOD_BUFFERS, blocks)):
        fetch(blk).start()
    ct = ct_ref[...]
    st = ct * jax.nn.sigmoid(ct)
    o_ref[...] = jnp.zeros(o_ref.shape, F32)
    for blk in range(blocks):
        layer, j = divmod(blk, tiles)
        cols = slice(j * MOD_COLS, (j + 1) * MOD_COLS)
        fetch(blk).wait()
        w = w_buf[blk % MOD_BUFFERS]
        for r in range(N_COND):
            o_ref[layer, r:r + 1, cols] = jnp.sum(w * st[:, r:r + 1], axis=0, keepdims=True) + b_ref[layer, :, cols]
        if blk + MOD_BUFFERS < blocks:
            fetch(blk + MOD_BUFFERS).start()


def _modulation(cond_t, w_mod, b_mod):
    depth, d, n = w_mod.shape
    vmem = pl.BlockSpec(memory_space=pltpu.VMEM)
    return pl.pallas_call(
        _mod_kernel,
        in_specs=[vmem, pl.BlockSpec(memory_space=pl.ANY), vmem],
        out_specs=vmem,
        out_shape=jax.ShapeDtypeStruct((depth, COND_ROWS, n), F32),
        scratch_shapes=[pltpu.VMEM((MOD_BUFFERS, d, MOD_COLS), F32), pltpu.SemaphoreType.DMA((MOD_BUFFERS,))],
        compiler_params=pltpu.CompilerParams(vmem_limit_bytes=VMEM_LIMIT),
        name="modulation",
    )(cond_t, w_mod, b_mod.reshape(depth, 1, n))


def _ctx_attention_rows(q, k, v, seq):
    scale = HEAD_DIM ** -0.5
    nseq = q.shape[0] // seq
    qt = (q * scale).T.astype(BF16)
    vt = v.T.astype(BF16)
    kb = k.astype(BF16)
    probs = [(slice(i * seq, (i + 1) * seq), slice(h * HEAD_DIM, (h + 1) * HEAD_DIM))
             for i in range(nseq) for h in range(NA_HEADS)]
    sts = [_dot(kb[rows, sl], qt[sl, rows]) for rows, sl in probs]
    pts = [jnp.exp(st - jnp.max(st, axis=0, keepdims=True)) for st in sts]
    dens = [jnp.sum(pt, axis=0, keepdims=True) for pt in pts]
    outs = [_dot(vt[sl, rows], pt.astype(BF16)) / den for (rows, sl), pt, den in zip(probs, pts, dens)]
    per_seq = [jnp.concatenate(outs[i * NA_HEADS:(i + 1) * NA_HEADS], axis=0).T for i in range(nseq)]
    return jnp.concatenate(per_seq, axis=0)


def _inproj_kernel(x_ref, g_ref, sh_ref, sc_ref, w_ref, *refs, ctx_seq, n_cast):
    _run_casts(refs[:n_cast], refs[n_cast + 4:2 * n_cast + 4])
    refs = refs[n_cast:n_cast + 4] + refs[2 * n_cast + 4:]
    h = _norm_mod(x_ref[...], g_ref[...], sh_ref[0], sc_ref[0]).astype(BF16)
    z = _dot(h, w_ref[...])
    q, k, v, u = (z[:, i * NA_WIDTH:(i + 1) * NA_WIDTH] for i in range(4))
    if ctx_seq is None:
        for o, val in zip(refs, (q, k, v, u)):
            o[...] = val.astype(o.dtype)
        return
    a_ref, u_ref, kc_ref, vc_ref, rows_scr = refs
    a_ref[...] = _ctx_attention_rows(q, k, v, ctx_seq).astype(a_ref.dtype)
    u_ref[...] = u
    tm = z.shape[0]
    pair = 2 * HEAD_DIM
    for o, val in ((kc_ref, k), (vc_ref, v)):
        for j in range(NA_HEADS // 2):
            both = val[:, j * pair:(j + 1) * pair]
            rows_scr[pl.ds(2 * j, tm, stride=NA_HEADS), :] = both
            rows_scr[pl.ds(2 * j + 1, tm, stride=NA_HEADS), :] = pltpu.roll(both, HEAD_DIM, axis=1)
        for b in range(o.shape[0]):
            blk = rows_scr[b * ctx_seq * NA_HEADS:(b + 1) * ctx_seq * NA_HEADS, 0:HEAD_DIM]
            o[b, 0] = blk.reshape(ctx_seq, NA_HEADS, HEAD_DIM)


def _inproj(x, g, mod, layer, row_fn, w, tm, ctx_seq=None, casts=()):
    t = x.shape[0]
    c_in, c_out, c_shape, c_ops = _cast_plan(casts, t // tm)
    flat = pl.BlockSpec((tm, NA_WIDTH), lambda i: (i, 0))
    scratch = []
    if ctx_seq is None:
        out_specs = [flat] * 4
        out_shape = [jax.ShapeDtypeStruct((t, NA_WIDTH), BF16)] + [jax.ShapeDtypeStruct((t, NA_WIDTH), F32)] * 3
    else:
        per = tm // ctx_seq
        cache = pl.BlockSpec((per, 1, ctx_seq, NA_HEADS, HEAD_DIM), lambda i: (i, 0, 0, 0, 0))
        out_specs = [flat, flat, cache, cache]
        out_shape = ([jax.ShapeDtypeStruct((t, NA_WIDTH), BF16), jax.ShapeDtypeStruct((t, NA_WIDTH), F32)]
                     + [jax.ShapeDtypeStruct((t // ctx_seq, 1, ctx_seq, NA_HEADS, HEAD_DIM), F32)] * 2)
        scratch = [pltpu.VMEM((tm * NA_HEADS, 2 * HEAD_DIM), F32)]
    return pl.pallas_call(
        functools.partial(_inproj_kernel, ctx_seq=ctx_seq, n_cast=len(c_ops)),
        grid=(t // tm,),
        in_specs=[pl.BlockSpec((tm, D_MODEL), lambda i: (i, 0)),
                  _full_spec((1, D_MODEL)),
                  _mod_spec(layer, 0, row_fn), _mod_spec(layer, 1, row_fn),
                  _resident_spec(w.shape)] + c_in,
        out_specs=out_specs + c_out,
        out_shape=out_shape + c_shape,
        scratch_shapes=scratch,
        compiler_params=_params(("parallel",)),
        name="inproj_even",
    )(x, g, mod, mod, w, *c_ops)


def _na_bias_kernel(rpb_ref, o_ref, *, rows):
    h = pl.program_id(0)
    kr_win = min(WIN_R, rows)
    qc = lax.broadcasted_iota(jnp.int32, (GRID_W, 2 * GRID_W), 0)
    lane = lax.broadcasted_iota(jnp.int32, (GRID_W, 2 * GRID_W), 1)
    kc = lane & (GRID_W - 1)
    cs = jnp.clip(qc - WIN_C // 2, 0, GRID_W - WIN_C)
    col_ok = (kc >= cs) & (kc < cs + WIN_C)
    dc = kc - qc + (WIN_C - 1)
    left = lane < GRID_W
    neg = jnp.full((GRID_W, 2 * GRID_W), NEG, F32)
    toeplitz = []
    for dr in range(2 * WIN_R - 1):
        t = neg
        for d in range(2 * WIN_C - 1):
            t = jnp.where(dc == d, rpb_ref[h, dr, d], t)
        toeplitz.append(jnp.where(col_ok, t, neg))
    for r in range(rows):
        rs = min(max(r - kr_win // 2, 0), rows - kr_win)

        def blk(kr):
            return toeplitz[kr - r + WIN_R - 1] if rs <= kr < rs + kr_win else None

        for j in range(rows // 2):
            a, b = blk(2 * j), blk(2 * j + 1)
            if a is None and b is None:
                pair = neg
            else:
                pair = jnp.where(left, neg if a is None else a, neg if b is None else b)
            o_ref[0, r * GRID_W:(r + 1) * GRID_W, j * 2 * GRID_W:(j + 1) * 2 * GRID_W] = pair


def _na_bias(rpb, seq):
    rows = seq // GRID_W
    return pl.pallas_call(
        functools.partial(_na_bias_kernel, rows=rows),
        grid=(NA_HEADS,),
        in_specs=[pl.BlockSpec(memory_space=pltpu.SMEM)],
        out_specs=pl.BlockSpec((1, seq, seq), lambda h: (h, 0, 0)),
        out_shape=jax.ShapeDtypeStruct((NA_HEADS, seq, seq), F32),
        compiler_params=_params(("parallel",)),
        name="na_bias",
    )(rpb)


def _na_key_range(qb, tq, rows):
    kr_win = min(WIN_R, rows)
    r0, r1 = qb * tq // GRID_W, ((qb + 1) * tq - 1) // GRID_W
    rs0 = min(max(r0 - kr_win // 2, 0), rows - kr_win)
    rs1 = min(max(r1 - kr_win // 2, 0), rows - kr_win)
    lo, hi = rs0 * GRID_W, (rs1 + kr_win) * GRID_W
    return lo // LANES * LANES, -(-hi // LANES) * LANES


def _na_attn_kernel(q_ref, k_ref, v_ref, ck_ref, cv_ref, bias_ref, o_ref, *, tq):
    scale = HEAD_DIM ** -0.5
    seq = q_ref.shape[1]
    probs = []
    for hl in range(2):
        sl = slice(hl * HEAD_DIM, (hl + 1) * HEAD_DIM)
        for qb in range(seq // tq):
            lo, hi = _na_key_range(qb, tq, seq // GRID_W)
            probs.append((hl, sl, slice(qb * tq, (qb + 1) * tq), slice(lo, hi)))
    head0 = pl.program_id(0) * 2
    cks = [ck_ref[0, 0, :, head0 + hl, :].astype(BF16) for hl in range(2)]
    cvs = [cv_ref[0, 0, :, head0 + hl, :].astype(BF16) for hl in range(2)]
    qs_all = [(q_ref[0, qs, sl] * scale).astype(BF16) for _, sl, qs, _ in probs]
    s_loc = [_dot_nt(q, k_ref[0, ks, sl].astype(BF16)) + bias_ref[hl, qs, ks]
             for q, (hl, sl, qs, ks) in zip(qs_all, probs)]
    s_ctx = [_dot_nt(q, cks[hl]) for q, (hl, _, _, _) in zip(qs_all, probs)]
    ms = [jnp.maximum(jnp.max(a, axis=-1, keepdims=True), jnp.max(b, axis=-1, keepdims=True))
          for a, b in zip(s_loc, s_ctx)]
    p_loc = [jnp.exp(a - m) for a, m in zip(s_loc, ms)]
    p_ctx = [jnp.exp(b - m) for b, m in zip(s_ctx, ms)]
    dens = [jnp.sum(a, axis=-1, keepdims=True) + jnp.sum(b, axis=-1, keepdims=True) for a, b in zip(p_loc, p_ctx)]
    for a, b, den, (hl, sl, qs, ks) in zip(p_loc, p_ctx, dens, probs):
        o = _dot(a.astype(BF16), v_ref[0, ks, sl].astype(BF16)) + _dot(b.astype(BF16), cvs[hl])
        o_ref[0, qs, sl] = o / den


def _na_attention(q, k, v, ck, cv, layer, bias):
    nb, seq, _ = q.shape
    lat = pl.BlockSpec((1, seq, 2 * HEAD_DIM), lambda hp, b: (b, 0, hp))
    ctx = pl.BlockSpec((1, 1) + ck.shape[2:], lambda hp, b: (b, layer, 0, 0, 0))
    return pl.pallas_call(
        functools.partial(_na_attn_kernel, tq=NA_Q_BLOCK),
        grid=(NA_HEADS // 2, nb),
        in_specs=[lat, lat, lat, ctx, ctx, pl.BlockSpec((2, seq, seq), lambda hp, b: (hp, 0, 0))],
        out_specs=lat,
        out_shape=jax.ShapeDtypeStruct((nb, seq, NA_WIDTH), F32),
        compiler_params=_params(("parallel", "arbitrary")),
        name="na_attention",
    )(q, k, v, ck, cv, bias)


S5_PARAM_ROWS = 4 * S5_GROUP + 8


def _s5_prep_group(i, p_ref, m_ref, wz_ref, wyt_ref, ar_ref, ai_ref):
    n = S5_GROUP
    lr = jnp.minimum(p_ref[i, 4 * n:4 * n + 1], -1e-4)
    li = p_ref[i, 4 * n + 1:4 * n + 2]
    dt = jnp.exp(p_ref[i, 4 * n + 2:4 * n + 3])
    a = lr * dt
    th = li * dt
    mag = jnp.exp(a)
    lbr = mag * jnp.cos(th)
    lbi = mag * jnp.sin(th)
    den = lr * lr + li * li
    nr = lbr - 1.0
    coef_r = (nr * lr + lbi * li) / den
    coef_i = (lbi * lr - nr * li) / den
    btr, bti = p_ref[i, 0:n], p_ref[i, n:2 * n]
    bbr = coef_r * btr - coef_i * bti
    bbi = coef_r * bti + coef_i * btr
    cr, ci = p_ref[i, 2 * n:3 * n], p_ref[i, 3 * n:4 * n]

    step = lax.broadcasted_iota(jnp.int32, (CHUNK, 2 * S5_STATE), 0).astype(F32)
    fwd = lax.broadcasted_iota(jnp.int32, (CHUNK, 2 * S5_STATE), 1) < S5_STATE

    def powers(e):
        pm = jnp.exp(a * e)
        return pm * jnp.cos(th * e), pm * jnp.sin(th * e)

    pzr, pzi = powers(jnp.where(fwd, CHUNK - 1 - step, step))
    pyr, pyi = powers(jnp.where(fwd, step + 1.0, CHUNK - step))
    p16r, p16i = powers(jnp.full((1, 2 * S5_STATE), float(CHUNK), F32))
    ar_ref[i] = p16r
    ai_ref[i] = p16i

    def outer(p, c):
        return (p[:, None, :] * c[None, :, :]).reshape(CW, 2 * S5_STATE)

    wzr = outer(pzr, bbr) - outer(pzi, bbi)
    wzi = outer(pzr, bbi) + outer(pzi, bbr)
    re, im = slice(0, 2 * S5_STATE), slice(2 * S5_STATE, 4 * S5_STATE)
    wz_ref[i, :, re] = wzr.astype(wz_ref.dtype)
    wz_ref[i, :, im] = wzi.astype(wz_ref.dtype)
    wyt_ref[i, :, re] = (outer(pyr, cr) - outer(pyi, ci)).astype(wyt_ref.dtype)
    wyt_ref[i, :, im] = (-(outer(pyi, cr) + outer(pyr, ci))).astype(wyt_ref.dtype)

    ones = jnp.ones((CHUNK, 2 * S5_STATE), F32)
    cwr = outer(ones, cr)
    cwi = outer(ones, ci)
    fwd_rows = lax.broadcasted_iota(jnp.int32, (CW, 2 * S5_STATE), 1) < S5_STATE
    zero = jnp.zeros((CW, 2 * S5_STATE), F32)

    def split(x):
        hi = x.astype(BF16)
        return hi, (x - hi.astype(F32)).astype(BF16)

    cw_hi, cw_lo = split(jnp.concatenate([cwr, -cwi], axis=1))

    def kk(sel):
        lhs_hi, lhs_lo = split(jnp.concatenate([jnp.where(sel, wzr, zero), jnp.where(sel, wzi, zero)], axis=1))
        return _dot_nt(lhs_hi, cw_hi) + (_dot_nt(lhs_hi, cw_lo) + _dot_nt(lhs_lo, cw_hi))

    kk_f = kk(fwd_rows)
    kk_b = kk(jnp.logical_not(fwd_rows))
    lanes = 2 * S5_STATE
    steps_per_tile = lanes // S5_GROUP
    col_step = lax.broadcasted_iota(jnp.int32, (CW, lanes), 1) // S5_GROUP
    for tile in range(CW // lanes):
        cols = slice(tile * lanes, (tile + 1) * lanes)
        kf, kb = kk_f[:, cols], kk_b[:, cols]
        m = jnp.zeros((CW, lanes), F32)
        for tt in range(steps_per_tile):
            t = tile * steps_per_tile + tt
            up = (CHUNK - 1 - t) * S5_GROUP
            dn = t * S5_GROUP
            piece_f = kf if up == 0 else jnp.concatenate([kf[up:], jnp.zeros((up, lanes), F32)], axis=0)
            piece_b = kb if dn == 0 else jnp.concatenate([jnp.zeros((dn, lanes), F32), kb[:CW - dn]], axis=0)
            m = jnp.where(col_step == tt, piece_f + piece_b, m)
        m_ref[i, :, cols] = m.astype(m_ref.dtype)


def _s5_prep_kernel(*refs):
    for i in range(refs[0].shape[0]):
        _s5_prep_group(i, *refs)


def _s5_prep(lam_re, lam_im, log_step, b_re, b_im, c_re, c_im):
    g = S5_GROUPS
    lanes = 2 * S5_STATE

    def dirs_last(x):
        return x.transpose(1, 0, 2).reshape(g, 1, lanes)

    lamr, lami = dirs_last(lam_re), dirs_last(lam_im)
    logdt = dirs_last(jnp.broadcast_to(log_step[:, :, None], (2, g, S5_STATE)))
    bt = [x.transpose(1, 3, 0, 2).reshape(g, S5_GROUP, lanes) for x in (b_re, b_im)]
    ct = [x.transpose(1, 2, 0, 3).reshape(g, S5_GROUP, lanes) for x in (c_re, c_im)]
    pad = jnp.zeros((g, S5_PARAM_ROWS - 4 * S5_GROUP - 3, lanes), F32)
    packed = jnp.concatenate(bt + ct + [lamr, lami, logdt, pad], axis=1)
    gp = S5_PREP_GROUPS
    vec = pl.BlockSpec((gp, 1, lanes), lambda i: (i, 0, 0))
    big = pl.BlockSpec((gp, CW, CW), lambda i: (i, 0, 0))
    return pl.pallas_call(
        _s5_prep_kernel,
        grid=(g // gp,),
        in_specs=[pl.BlockSpec((gp, S5_PARAM_ROWS, lanes), lambda i: (i, 0, 0))],
        out_specs=[big, big, big, vec, vec],
        out_shape=[jax.ShapeDtypeStruct((g, CW, CW), BF16)] * 3 + [jax.ShapeDtypeStruct((g, 1, lanes), F32)] * 2,
        compiler_params=_params(("parallel",)),
        name="s5_prep",
    )(packed)


def _s5_kernel(u_ref, m_ref, wz_ref, wyt_ref, ar_ref, ai_ref, s0_ref, perm_ref, permt_ref, y_ref, fin_ref,
               xt_scr, z_scr, sp_scr, *, nbp, nc):
    s = S5_STATE
    r = xt_scr.shape[2]
    for t in range(CHUNK):
        xt_scr[t] = u_ref[pl.ds(t, r, stride=CHUNK), :].T
    fwd = lax.broadcasted_iota(jnp.int32, (nbp, 2 * s), 1) < s
    groups = range(GROUPS_PER_BLOCK)
    chans = [slice(gl * S5_GROUP, (gl + 1) * S5_GROUP) for gl in groups]

    us = [jnp.concatenate([xt_scr[t, ch, :] for t in range(CHUNK)], axis=0).T.astype(BF16) for ch in chans]
    for gl in groups:
        z_scr[gl] = _dot(_dot(perm_ref[...], us[gl]).astype(BF16), wz_ref[gl])

    ars = [jnp.broadcast_to(ar_ref[gl], (nbp, 2 * s)) for gl in groups]
    ais = [jnp.broadcast_to(ai_ref[gl], (nbp, 2 * s)) for gl in groups]
    srs = [s0_ref[gl, :, 0:2 * s] for gl in groups]
    sis = [s0_ref[gl, :, 2 * s:4 * s] for gl in groups]
    for k in range(nc):
        up = slice(k * nbp, (k + 1) * nbp)
        dn = slice((nc - 1 - k) * nbp, (nc - k) * nbp)
        for gl in groups:
            sr, si = srs[gl], sis[gl]
            sp_scr[gl, up, 0:s] = sr[:, 0:s]
            sp_scr[gl, dn, s:2 * s] = sr[:, s:2 * s]
            sp_scr[gl, up, 2 * s:3 * s] = si[:, 0:s]
            sp_scr[gl, dn, 3 * s:4 * s] = si[:, s:2 * s]
            zr = jnp.where(fwd, z_scr[gl, up, 0:2 * s], z_scr[gl, dn, 0:2 * s])
            zi = jnp.where(fwd, z_scr[gl, up, 2 * s:4 * s], z_scr[gl, dn, 2 * s:4 * s])
            srs[gl] = ars[gl] * sr - ais[gl] * si + zr
            sis[gl] = ars[gl] * si + ais[gl] * sr + zi
    for gl in groups:
        fin_ref[gl, :, 0:2 * s] = srs[gl]
        fin_ref[gl, :, 2 * s:4 * s] = sis[gl]
    sps = [_dot(permt_ref[...], sp_scr[gl].astype(BF16)).astype(BF16) for gl in groups]
    for gl in groups:
        yt = (_dot(us[gl], m_ref[gl]) + _dot_nt(sps[gl], wyt_ref[gl])).T
        for t in range(CHUNK):
            xt_scr[t, chans[gl], :] = yt[t * S5_GROUP:(t + 1) * S5_GROUP, :]

    for t in range(CHUNK):
        y_ref[pl.ds(t, r, stride=CHUNK), :] = xt_scr[t].T


def _s5_mixer(u, s0, ops, nb, seq):
    m, wz, wyt, ar, ai = ops
    g, nc = S5_GROUPS, seq // CHUNK
    nbp = -(-nb // 8) * 8
    r, rp = nb * nc, nc * nbp
    gpb = GROUPS_PER_BLOCK
    s0l = jnp.zeros((g, nbp, CW), F32)
    if s0 is not None:
        s0l = s0l.at[:, :nb].set(s0.transpose(3, 0, 2, 1, 4).reshape(g, nb, CW))
    perm = np.zeros((nc, nbp, nb, nc), np.float32)
    for b in range(nb):
        perm[np.arange(nc), b, b, np.arange(nc)] = 1.0
    perm = jnp.asarray(perm.reshape(rp, r)).astype(BF16)
    col = pl.BlockSpec((nb * seq, gpb * S5_GROUP), lambda j: (0, j))
    wspec = pl.BlockSpec((gpb, CW, CW), lambda j: (j, 0, 0))
    cspec = pl.BlockSpec((gpb, 1, 2 * S5_STATE), lambda j: (j, 0, 0))
    sspec = pl.BlockSpec((gpb, nbp, CW), lambda j: (j, 0, 0))
    y, fin = pl.pallas_call(
        functools.partial(_s5_kernel, nbp=nbp, nc=nc),
        grid=(g // gpb,),
        in_specs=[col, wspec, wspec, wspec, cspec, cspec, sspec, _full_spec((rp, r)), _full_spec((r, rp))],
        out_specs=[col, sspec],
        out_shape=[jax.ShapeDtypeStruct((nb * seq, S5_WIDTH), F32), jax.ShapeDtypeStruct((g, nbp, CW), F32)],
        scratch_shapes=[pltpu.VMEM((CHUNK, gpb * S5_GROUP, r), F32),
                        pltpu.VMEM((gpb, rp, CW), F32), pltpu.VMEM((gpb, rp, CW), F32)],
        compiler_params=_params(("parallel",)),
        name="s5_scan",
    )(u, m, wz, wyt, ar, ai, s0l, perm, perm.T)
    state = fin[:, :nb].reshape(g, nb, 2, 2, S5_STATE).transpose(1, 3, 2, 0, 4)
    return y, state


FFN_ROWS = 512
FFN_CHUNK = 256


def _ffn_rows(x, g, shift, scale, gate, wg_ref, wu_ref, wd_ref):
    h = _norm_mod(x, g, shift, scale).astype(BF16)
    acc = jnp.zeros(x.shape, F32)
    for j in range(D_FF // FFN_CHUNK):
        sl = slice(j * FFN_CHUNK, (j + 1) * FFN_CHUNK)
        gg = _dot(h, wg_ref[:, sl])
        uu = _dot(h, wu_ref[:, sl])
        act = (gg * jax.nn.sigmoid(gg) * uu).astype(BF16)
        acc = acc + _dot(act, wd_ref[sl, :])
    return x + gate * acc


def _ffn_specs(layer, row_fn, wg, wu, wd):
    return [_full_spec((1, D_MODEL)),
            _mod_spec(layer, 3, row_fn), _mod_spec(layer, 4, row_fn), _mod_spec(layer, 5, row_fn),
            _resident_spec(wg.shape), _resident_spec(wu.shape), _resident_spec(wd.shape)]


def _even_tail_kernel(*refs, steps_ctx):
    ctx_in, lat_in = refs[0:4], refs[4:8]
    d_ref, gw_ref, gb_ref, wo_ref, gate_ref, g2_ref, sh2_ref, sc2_ref, gate2_ref, wg_ref, wu_ref, wd_ref = refs[8:20]
    n_cast = (len(refs) - 22) // 2
    o_ctx, o_lat = refs[20 + n_cast:22 + n_cast]
    _run_casts(refs[20:20 + n_cast], refs[22 + n_cast:])

    def tail(a_ref, y_ref, u_ref, x_ref, o_ref):
        y = jax.nn.gelu(y_ref[...] + d_ref[...] * u_ref[...])
        y = y * jax.nn.sigmoid(_dot(y.astype(BF16), gw_ref[...]) + gb_ref[...])
        o = _dot(a_ref[...].astype(BF16), wo_ref[0:NA_WIDTH, :]) + _dot(y.astype(BF16), wo_ref[NA_WIDTH:, :])
        x = x_ref[...] + gate_ref[0] * o
        o_ref[...] = _ffn_rows(x, g2_ref[...], sh2_ref[0], sc2_ref[0], gate2_ref[0], wg_ref, wu_ref, wd_ref)

    i = pl.program_id(0)

    @pl.when(i < steps_ctx)
    def _():
        tail(*ctx_in, o_ctx)

    @pl.when(i >= steps_ctx)
    def _():
        tail(*lat_in, o_lat)


def _even_tail(ctx, lat, lat_seq, d, glu_w, glu_b, w_out, g_ffn, wg, wu, wd, mod, layer, casts):
    tm = FFN_ROWS
    sc, sl = ctx[3].shape[0] // tm, lat[3].shape[0] // tm
    at_ctx = lambda i: jnp.minimum(i, sc - 1)
    at_lat = lambda i: jnp.maximum(i - sc, 0)
    row_fn = lambda i: jnp.where(i < sc, 0, 1 + (at_lat(i) * tm) // lat_seq)
    c_in, c_out, c_shape, c_ops = _cast_plan(casts, sc, at_ctx)

    def specs(at):
        half = pl.BlockSpec((tm, NA_WIDTH), lambda i: (at(i), 0))
        full = pl.BlockSpec((tm, D_MODEL), lambda i: (at(i), 0))
        return [half, half, half, full], full

    in_ctx, out_ctx = specs(at_ctx)
    in_lat, out_lat = specs(at_lat)
    return pl.pallas_call(
        functools.partial(_even_tail_kernel, steps_ctx=sc),
        grid=(sc + sl,),
        in_specs=in_ctx + in_lat
                 + [_full_spec((1, S5_WIDTH)), _resident_spec(glu_w.shape), _full_spec((1, S5_WIDTH)),
                    _resident_spec(w_out.shape), _mod_spec(layer, 2, row_fn)]
                 + _ffn_specs(layer, row_fn, wg, wu, wd) + c_in,
        out_specs=[out_ctx, out_lat] + c_out,
        out_shape=[jax.ShapeDtypeStruct(ctx[3].shape, F32), jax.ShapeDtypeStruct(lat[3].shape, F32)] + c_shape,
        compiler_params=_params(("arbitrary",), vmem=EVEN_TAIL_VMEM),
        name="even_tail",
    )(*ctx, *lat, d, glu_w, glu_b, w_out, mod, g_ffn, mod, mod, mod, wg, wu, wd, *c_ops)


def _dft_consts(seq):
    def cs(n):
        k = np.arange(n)
        ang = 2.0 * np.pi * ((k[:, None] * k[None, :]) % n) / n
        return np.cos(ang), np.sin(ang)

    cc, sc = cs(FNET_GROUP_WIDTH)
    cl, sl = cs(seq)
    chan = jnp.asarray(np.concatenate([cc, sc], axis=1), dtype=F32)
    pos = jnp.asarray(np.concatenate([cl, -sl], axis=1), dtype=F32)
    return chan.astype(BF16), pos.astype(BF16)


def _fourier_rows(x, g, shift, scale, gate, win_ref, chan_ref, pos_ref, wout_ref):
    seq = x.shape[0]
    h = _norm_mod(x, g, shift, scale).astype(BF16)
    z = _dot(h, win_ref[...]).astype(BF16)
    w = FNET_GROUP_WIDTH
    pos = pos_ref[...]
    parts = []
    for c in range(FNET_GROUPS):
        ab = _dot(z[:, c * w:(c + 1) * w], chan_ref[...]).astype(BF16)
        stacked = jnp.concatenate([ab[:, :w], ab[:, w:]], axis=0)
        parts.append(_dot(pos, stacked))
    f = jnp.concatenate(parts, axis=1) * (1.0 / math.sqrt(seq * w))
    return x + gate * _dot(f.astype(BF16), wout_ref[...])


def _odd_layer_kernel(x_ref, g_ref, sh_ref, sc_ref, win_ref, chan_ref, pos_ref, wout_ref, gate_ref,
                      g2_ref, sh2_ref, sc2_ref, gate2_ref, wg_ref, wu_ref, wd_ref, fg_ref, o_ref, *, seq):
    tm = x_ref.shape[0]
    for r0 in range(0, tm, seq):
        rows = slice(r0, r0 + seq)
        o_ref[rows, :] = _fourier_rows(x_ref[rows, :], g_ref[...], sh_ref[0], sc_ref[0], gate_ref[0],
                                       win_ref, chan_ref, pos_ref, wout_ref)
    for r0 in range(0, tm, FFN_ROWS):
        rows = slice(r0, r0 + FFN_ROWS)
        y = _ffn_rows(o_ref[rows, :], g2_ref[...], sh2_ref[0], sc2_ref[0], gate2_ref[0], wg_ref, wu_ref, wd_ref)
        ms = jnp.mean(y * y, axis=-1, keepdims=True)
        o_ref[rows, :] = y * lax.rsqrt(ms + EPS) * fg_ref[...]


def _odd_layer(x, g_mix, g_ffn, final_g, mod, layer, row_of, w_in, w_out, wg, wu, wd, seq):
    t = x.shape[0]
    tm = max(seq, FFN_ROWS)
    row_fn = row_of(tm)
    chan, pos = _dft_consts(seq)
    blk = pl.BlockSpec((tm, D_MODEL), lambda i: (i, 0))
    return pl.pallas_call(
        functools.partial(_odd_layer_kernel, seq=seq),
        grid=(t // tm,),
        in_specs=[blk, _full_spec((1, D_MODEL)), _mod_spec(layer, 0, row_fn), _mod_spec(layer, 1, row_fn),
                  _resident_spec(w_in.shape), _resident_spec(chan.shape), _resident_spec(pos.shape),
                  _resident_spec(w_out.shape), _mod_spec(layer, 2, row_fn)]
                 + _ffn_specs(layer, row_fn, wg, wu, wd) + [_full_spec((1, D_MODEL))],
        out_specs=blk,
        out_shape=jax.ShapeDtypeStruct((t, D_MODEL), F32),
        compiler_params=_params(("parallel",)),
        name="odd_layer",
    )(x, g_mix, mod, mod, w_in, chan, pos, w_out, mod, g_ffn, mod, mod, mod, wg, wu, wd, final_g)


def kernel(x_prompt, x_sample, cache_na_k, cache_na_v, state_s5, c, c_ctx, w_mod, b_mod, norm_mix_g, norm_ffn_g, w_in_even, na_rpb, s5_lam_re, s5_lam_im, s5_log_step, s5_b_re, s5_b_im, s5_c_re, s5_c_im, s5_d, s5_glu_w, s5_glu_b, w_out_even, w_in_odd, w_out_odd, ffn_w_gate, ffn_w_up, ffn_w_down, final_norm_g):
    bp, lp, _ = x_prompt.shape
    bs, ls, _ = x_sample.shape
    depth = w_mod.shape[0]
    assert bs + 1 <= N_COND and depth == 2

    cond = jnp.concatenate([c_ctx[None], c, jnp.zeros((COND_ROWS - 1 - bs, D_MODEL), F32)], axis=0)
    mod = _modulation(cond.T, w_mod, b_mod).reshape(depth * COND_ROWS * 6, 1, D_MODEL)

    streams = [
        dict(x=x_prompt.reshape(bp * lp, D_MODEL), nb=bp, seq=lp, row_of=lambda tm: (lambda i: 0)),
        dict(x=x_sample.reshape(bs * ls, D_MODEL), nb=bs, seq=ls, row_of=lambda tm: (lambda i: 1 + (i * tm) // ls)),
    ]
    tm = FFN_ROWS

    e = 0
    w_in = w_in_even[e].astype(BF16)
    glu_b = s5_glu_b[e].reshape(1, S5_WIDTH)
    g_mix = norm_mix_g[0].reshape(1, D_MODEL)
    g_ffn0 = norm_ffn_g[0].reshape(1, D_MODEL)
    s5_ops = _s5_prep(s5_lam_re[e], s5_lam_im[e], s5_log_step[e], s5_b_re[e], s5_b_im[e], s5_c_re[e], s5_c_im[e])
    s5_dvec = s5_d[e].reshape(1, S5_WIDTH)
    bias = _na_bias(na_rpb[e], ls)
    layer0_w = [(ffn_w_gate, 0), (ffn_w_up, 0), (ffn_w_down, 0), (w_out_even, e), (s5_glu_w, e)]
    layer1_w = [(ffn_w_gate, 1), (ffn_w_up, 1), (ffn_w_down, 1), (w_in_odd, 0), (w_out_odd, 0)]

    new_k = new_v = new_s = None
    for si, st in enumerate(streams):
        nb, seq = st["nb"], st["seq"]
        if si == 0:
            a, u, new_k, new_v, *layer0_w = _inproj(st["x"], g_mix, mod, 0, st["row_of"](tm), w_in, tm,
                                                    ctx_seq=seq, casts=layer0_w)
            wg0, wu0, wd0, w_out, glu_w = layer0_w
            s0 = None
        else:
            q, k, v, u = _inproj(st["x"], g_mix, mod, 0, st["row_of"](tm), w_in, tm)
            a = _na_attention(q.reshape(nb, seq, NA_WIDTH), k.reshape(nb, seq, NA_WIDTH), v.reshape(nb, seq, NA_WIDTH),
                              cache_na_k, cache_na_v, e, bias).reshape(nb * seq, NA_WIDTH)
            s0 = state_s5[:, e].astype(F32)
        y, state = _s5_mixer(u, s0, s5_ops, nb, seq)
        if si == 0:
            new_s = state[:, None]
        st["mixed"] = (a, y, u, st["x"])

    tail = _even_tail(streams[0]["mixed"], streams[1]["mixed"], ls, s5_dvec, glu_w, glu_b, w_out, g_ffn0,
                      wg0, wu0, wd0, mod, 0, layer1_w)
    streams[0]["x"], streams[1]["x"] = tail[:2]
    wg1, wu1, wd1, w_in1, w_out1 = tail[2:]

    for st in streams:
        st["x"] = _odd_layer(st["x"], norm_mix_g[1].reshape(1, D_MODEL), norm_ffn_g[1].reshape(1, D_MODEL),
                             final_norm_g.reshape(1, D_MODEL), mod, 1, st["row_of"],
                             w_in1, w_out1, wg1, wu1, wd1, st["seq"])

    y_prompt = streams[0]["x"].reshape(bp, lp, D_MODEL)
    y_sample = streams[1]["x"].reshape(bs, ls, D_MODEL)
    return (y_prompt, y_sample, new_k, new_v, new_s)
```
